```python
import math
import jax, jax.numpy as jnp
from jax import lax
import numpy as np

D_MODEL = 1024
BATCH = 2
SEQ = 8192
DEPTH = 1
DEC_BATCH = 128
DEC_SEQ = 8
PAST_LEN = 8192
PAGE_SIZE = 128

D_MIX = D_MODEL
D_ATTN = D_MIX // 2
D_SSM = D_MIX // 4
D_MEM = D_MIX - D_ATTN - D_SSM
HEAD_DIM = 64
N_HEADS_A = D_ATTN // HEAD_DIM
MOBA_BLOCK = 256
MOBA_TOPK = 3
Q_CHUNK = 64
SSM_CH = 16
SSM_GROUPS = D_SSM // SSM_CH
SSM_STATE = 64
MEM_HEADS = 4
MEM_HEAD_DIM = D_MEM // MEM_HEADS
N_MEM = 256
D_IN = 3 * D_ATTN + D_SSM + D_MEM + D_MIX
DEEPNORM_ALPHA = (2.0 * DEPTH) ** 0.25
DEEPNORM_BETA = (8.0 * DEPTH) ** -0.25
LN_EPS = 1e-5
NEG_INF = -1e30
DT_MIN = 1e-3
DT_MAX = 1e-1

kernel_name = 'hymba_moba_s5_memxattn_step'

F32 = jnp.float32


def _layernorm(x, g, b):
    xf = x.astype(F32)
    mu = jnp.mean(xf, axis=-1, keepdims=True)
    var = jnp.mean(jnp.square(xf - mu), axis=-1, keepdims=True)
    return ((xf - mu) * lax.rsqrt(var + LN_EPS) * g.astype(F32) + b.astype(F32)).astype(x.dtype)


def _split_proj(x, w_in):
    b, t = x.shape[:2]
    h = jnp.einsum('btd,de->bte', x, w_in)
    cuts = np.cumsum([D_ATTN, D_ATTN, D_ATTN, D_SSM, D_MEM]).tolist()
    q, k, v, u, qm, gate = jnp.split(h, cuts, axis=-1)
    q = q.reshape(b, t, N_HEADS_A, HEAD_DIM)
    k = k.reshape(b, t, N_HEADS_A, HEAD_DIM)
    v = v.reshape(b, t, N_HEADS_A, HEAD_DIM)
    qm = qm.reshape(b, t, MEM_HEADS, MEM_HEAD_DIM)
    return q, k, v, u, qm, gate


def _select_blocks(q, means, n_full, ks):
    s = jnp.einsum('bthd,bnhd->bthn', q.astype(F32), means)
    nb = means.shape[1]
    allowed = jnp.arange(nb)[None, None, None, :] < n_full[None, :, None, None]
    s = jnp.where(allowed, s, NEG_INF)
    _, idx = lax.top_k(s, ks)
    valid = idx < n_full[None, :, None, None]
    return idx, valid


def _moba_core(q, k_own, v_own, own_mask, k_sel=None, v_sel=None, sel_valid=None):
    scale = HEAD_DIM ** -0.5
    lo = jnp.einsum('bthd,blhd->bthl', q, k_own, preferred_element_type=F32) * scale
    lo = jnp.where(own_mask[None, :, None, :], lo, NEG_INF)
    if k_sel is None:
        p = jax.nn.softmax(lo, axis=-1).astype(v_own.dtype)
        return jnp.einsum('bthl,blhd->bthd', p, v_own)
    ls = jnp.einsum('bthd,bthjkd->bthjk', q, k_sel, preferred_element_type=F32) * scale
    ls = jnp.where(sel_valid[..., None], ls, NEG_INF)
    b, t, h, ks, blk = ls.shape
    p = jax.nn.softmax(jnp.concatenate([ls.reshape(b, t, h, ks * blk), lo], axis=-1), axis=-1)
    p = p.astype(v_own.dtype)
    ps = p[..., :ks * blk].reshape(b, t, h, ks, blk)
    po = p[..., ks * blk:]
    return jnp.einsum('bthjk,bthjkd->bthd', ps, v_sel) + jnp.einsum('bthl,blhd->bthd', po, v_own)


def _moba_prompt(q, k, v):
    b, s = q.shape[:2]
    nb = -(-s // MOBA_BLOCK)
    pad = nb * MOBA_BLOCK - s
    kp = jnp.pad(k, ((0, 0), (0, pad), (0, 0), (0, 0)))
    vp = jnp.pad(v, ((0, 0), (0, pad), (0, 0), (0, 0)))
    kb = kp.reshape(b, nb, MOBA_BLOCK, N_HEADS_A, HEAD_DIM)
    vb = vp.reshape(b, nb, MOBA_BLOCK, N_HEADS_A, HEAD_DIM)
    ks = min(MOBA_TOPK, nb - 1)
    n_chunks = s // Q_CHUNK
    q_c = q.reshape(b, n_chunks, Q_CHUNK, N_HEADS_A, HEAD_DIM).swapaxes(0, 1)
    chunk_ids = jnp.arange(n_chunks)

    def own_block(c):
        start = c * Q_CHUNK
        blk_start = (start // MOBA_BLOCK) * MOBA_BLOCK
        k_own = lax.dynamic_slice_in_dim(kp, blk_start, MOBA_BLOCK, axis=1)
        v_own = lax.dynamic_slice_in_dim(vp, blk_start, MOBA_BLOCK, axis=1)
        q_pos = start + jnp.arange(Q_CHUNK)
        own_pos = blk_start + jnp.arange(MOBA_BLOCK)
        return k_own, v_own, own_pos[None, :] <= q_pos[:, None]

    if ks > 0:
        means = kb[:, :nb - 1].astype(F32).mean(axis=2)
        idx, valid = _select_blocks(q, means, jnp.arange(s) // MOBA_BLOCK, ks)
        idx_c = idx.reshape(b, n_chunks, Q_CHUNK, N_HEADS_A, ks).swapaxes(0, 1)
        valid_c = valid.reshape(b, n_chunks, Q_CHUNK, N_HEADS_A, ks).swapaxes(0, 1)
        kbh = kb.transpose(0, 3, 1, 2, 4)
        vbh = vb.transpose(0, 3, 1, 2, 4)
        bi = jnp.arange(b)[:, None, None, None]
        hi = jnp.arange(N_HEADS_A)[None, None, :, None]

        def chunk(args):
            c, qc, idc, vac = args
            k_own, v_own, own_mask = own_block(c)
            k_sel = kbh[bi, hi, idc]
            v_sel = vbh[bi, hi, idc]
            return _moba_core(qc, k_own, v_own, own_mask, k_sel, v_sel, vac)

        out = lax.map(chunk, (chunk_ids, q_c, idx_c, valid_c))
    else:
        def chunk(args):
            c, qc = args
            k_own, v_own, own_mask = own_block(c)
            return _moba_core(qc, k_own, v_own, own_mask)

        out = lax.map(chunk, (chunk_ids, q_c))
    return out.swapaxes(0, 1).reshape(b, s, N_HEADS_A, HEAD_DIM)


def _moba_sample(q, k_new, v_new, cache_k, cache_v, page_table):
    db, t = q.shape[:2]
    n_pages = page_table.shape[1]
    past_len = n_pages * PAGE_SIZE
    n_full = past_len // MOBA_BLOCK
    own_start = n_full * MOBA_BLOCK
    own_past = past_len - own_start
    ks = min(MOBA_TOPK, n_full)
    pages_per_block = MOBA_BLOCK // PAGE_SIZE
    own_mask = jnp.concatenate([jnp.ones((t, own_past), dtype=bool),
                                jnp.tril(jnp.ones((t, t), dtype=bool))], axis=1)
    hi = jnp.arange(N_HEADS_A)[None, :, None]

    def one_seq(args):
        q1, kn1, vn1, pt = args
        k_past = cache_k[pt].reshape(past_len, N_HEADS_A, HEAD_DIM)
        k_own = jnp.concatenate([k_past[own_start:], kn1], axis=0)
        v_own_past = cache_v[pt[own_start // PAGE_SIZE:]].reshape(own_past, N_HEADS_A, HEAD_DIM)
        v_own = jnp.concatenate([v_own_past, vn1], axis=0)
        if ks > 0:
            kb = k_past[:own_start].reshape(n_full, MOBA_BLOCK, N_HEADS_A, HEAD_DIM)
            means = kb.astype(F32).mean(axis=1)
            idx, valid = _select_blocks(q1[None], means[None], jnp.full((t,), n_full), ks)
            idx, valid = idx[0], valid[0]
            k_sel = kb.transpose(2, 0, 1, 3)[hi, idx]
            phys = pt[idx[..., None] * pages_per_block + jnp.arange(pages_per_block)]
            v_sel = cache_v[phys, :, hi[..., None]].reshape(t, N_HEADS_A, ks, MOBA_BLOCK, HEAD_DIM)
            out = _moba_core(q1[None], k_own[None], v_own[None], own_mask,
                             k_sel[None], v_sel[None], valid[None])
        else:
            out = _moba_core(q1[None], k_own[None], v_own[None], own_mask)
        return out[0]

    return lax.map(one_seq, (q, k_new, v_new, page_table))


def _ssm_discretize(a_re, a_im, log_dt, b_re, b_im):
    a_re = a_re.astype(F32)
    a_im = a_im.astype(F32)
    dt = jnp.exp(log_dt.astype(F32))[:, None]
    mag = jnp.exp(a_re * dt)
    ang = a_im * dt
    abar_re = mag * jnp.cos(ang)
    abar_im = mag * jnp.sin(ang)
    den = a_re * a_re + a_im * a_im
    f_re = ((abar_re - 1.0) * a_re + abar_im * a_im) / den
    f_im = (abar_im * a_re - (abar_re - 1.0) * a_im) / den
    b_re = b_re.astype(F32)
    b_im = b_im.astype(F32)
    bb_re = f_re[..., None] * b_re - f_im[..., None] * b_im
    bb_im = f_re[..., None] * b_im + f_im[..., None] * b_re
    return abar_re, abar_im, bb_re, bb_im


def _complex_affine_combine(e1, e2):
    a1r, a1i, b1r, b1i = e1
    a2r, a2i, b2r, b2i = e2
    return (a2r * a1r - a2i * a1i,
            a2r * a1i + a2i * a1r,
            a2r * b1r - a2i * b1i + b2r,
            a2r * b1i + a2i * b1r + b2i)


def _ssm_branch(u, h0_re, h0_im, disc, c_re, c_im, d_skip, w_glu, b_glu):
    abar_re, abar_im, bb_re, bb_im = disc
    b, t = u.shape[:2]
    uf = u.astype(F32).reshape(b, t, SSM_GROUPS, SSM_CH)
    h0_re = h0_re.astype(F32)
    h0_im = h0_im.astype(F32)
    bu_re = jnp.einsum('btgc,gpc->btgp', uf, bb_re)
    bu_im = jnp.einsum('btgc,gpc->btgp', uf, bb_im)
    bu_re = bu_re.at[:, 0].add(abar_re * h0_re - abar_im * h0_im)
    bu_im = bu_im.at[:, 0].add(abar_re * h0_im + abar_im * h0_re)
    a_re_t = jnp.broadcast_to(abar_re, bu_re.shape)
    a_im_t = jnp.broadcast_to(abar_im, bu_im.shape)
    _, _, h_re, h_im = lax.associative_scan(_complex_affine_combine,
                                            (a_re_t, a_im_t, bu_re, bu_im), axis=1)
    y = (jnp.einsum('btgp,gcp->btgc', h_re, c_re.astype(F32))
         - jnp.einsum('btgp,gcp->btgc', h_im, c_im.astype(F32))
         + d_skip.astype(F32).reshape(SSM_GROUPS, SSM_CH) * uf)
    y = jax.nn.gelu(y.reshape(b, t, D_SSM))
    z = jnp.einsum('bte,ef->btf', y, w_glu.astype(F32)) + b_glu.astype(F32)
    z_a, z_g = jnp.split(z, 2, axis=-1)
    return (z_a * jax.nn.sigmoid(z_g)).astype(u.dtype), h_re[:, -1], h_im[:, -1]


def _mem_kv(mem, w_mem_kv):
    b, m = mem.shape[:2]
    kv = jnp.einsum('bmd,de->bme', mem, w_mem_kv)
    mk, mv = jnp.split(kv, 2, axis=-1)
    return mk.reshape(b, m, MEM_HEADS, MEM_HEAD_DIM), mv.reshape(b, m, MEM_HEADS, MEM_HEAD_DIM)


def _mem_attn(qm, mk, mv):
    logits = jnp.einsum('bthd,bmhd->bhtm', qm, mk, preferred_element_type=F32) * MEM_HEAD_DIM ** -0.5
    p = jax.nn.softmax(logits, axis=-1).astype(mv.dtype)
    return jnp.einsum('bhtm,bmhd->bthd', p, mv)


def _merge(x, attn, ssm_y, mem_o, gate, w_out, ln_g, ln_b):
    b, t = x.shape[:2]
    mixed = jnp.concatenate([attn.reshape(b, t, D_ATTN), ssm_y,
                             mem_o.reshape(b, t, D_MEM)], axis=-1) * jax.nn.silu(gate)
    out = jnp.einsum('btm,md->btd', mixed, w_out)
    return _layernorm(DEEPNORM_ALPHA * x + out, ln_g, ln_b)


def setup_inputs(seed: int = 0) -> dict:
    key = jax.random.key(seed)
    ks = jax.random.split(key, 25)
    nrm = jax.random.normal
    n_pages = PAST_LEN // PAGE_SIZE
    n_phys = (DEC_BATCH * n_pages * 5) // 4
    x_prompt = nrm(ks[0], (BATCH, SEQ, D_MODEL), F32)
    x_sample = nrm(ks[1], (DEC_BATCH, DEC_SEQ, D_MODEL), F32)
    mem_prompt = nrm(ks[2], (BATCH, N_MEM, D_MODEL), F32)
    cache_k = nrm(ks[3], (n_phys, PAGE_SIZE, N_HEADS_A, HEAD_DIM), F32)
    cache_v = nrm(ks[4], (n_phys, PAGE_SIZE, N_HEADS_A, HEAD_DIM), F32)
    state_ssm_re = 0.5 * nrm(ks[5], (DEC_BATCH, SSM_GROUPS, SSM_STATE), F32)
    state_ssm_im = 0.5 * nrm(ks[6], (DEC_BATCH, SSM_GROUPS, SSM_STATE), F32)
    cache_mem_k = nrm(ks[7], (DEC_BATCH, N_MEM, MEM_HEADS, MEM_HEAD_DIM), F32)
    cache_mem_v = nrm(ks[8], (DEC_BATCH, N_MEM, MEM_HEADS, MEM_HEAD_DIM), F32)
    page_table = jax.random.permutation(ks[9], n_phys)[:DEC_BATCH * n_pages]
    page_table = page_table.reshape(DEC_BATCH, n_pages).astype(jnp.int32)
    w_in = nrm(ks[10], (D_MODEL, D_IN), F32) * D_MODEL ** -0.5
    w_in = w_in.at[:, 2 * D_ATTN:3 * D_ATTN].multiply(DEEPNORM_BETA)
    w_mem_kv = nrm(ks[11], (D_MODEL, 2 * D_MEM), F32) * D_MODEL ** -0.5
    w_mem_kv = w_mem_kv.at[:, D_MEM:].multiply(DEEPNORM_BETA)
    a_re = -0.5 + 0.01 * nrm(ks[12], (SSM_GROUPS, SSM_STATE), F32)
    a_im = math.pi * jnp.arange(SSM_STATE, dtype=F32)[None, :] + 0.01 * nrm(ks[13], (SSM_GROUPS, SSM_STATE), F32)
    log_dt = jax.random.uniform(ks[14], (SSM_GROUPS,), F32, minval=math.log(DT_MIN), maxval=math.log(DT_MAX))
    b_re = nrm(ks[15], (SSM_GROUPS, SSM_STATE, SSM_CH), F32) * (2.0 * SSM_CH) ** -0.5
    b_im = nrm(ks[16], (SSM_GROUPS, SSM_STATE, SSM_CH), F32) * (2.0 * SSM_CH) ** -0.5
    c_re = nrm(ks[17], (SSM_GROUPS, SSM_CH, SSM_STATE), F32) * SSM_STATE ** -0.5
    c_im = nrm(ks[18], (SSM_GROUPS, SSM_CH, SSM_STATE), F32) * SSM_STATE ** -0.5
    d_skip = 0.5 * nrm(ks[19], (D_SSM,), F32)
    w_glu = nrm(ks[20], (D_SSM, 2 * D_SSM), F32) * D_SSM ** -0.5
    b_glu = 0.01 * nrm(ks[21], (2 * D_SSM,), F32)
    w_out = nrm(ks[22], (D_MIX, D_MODEL), F32) * D_MIX ** -0.5 * DEEPNORM_BETA
    ln_g = 1.0 + 0.01 * nrm(ks[23], (D_MODEL,), F32)
    ln_b = 0.01 * nrm(ks[24], (D_MODEL,), F32)
    return {'x_prompt': x_prompt, 'x_sample': x_sample, 'mem_prompt': mem_prompt,
            'cache_k': cache_k, 'cache_v': cache_v,
            'state_ssm_re': state_ssm_re, 'state_ssm_im': state_ssm_im,
            'cache_mem_k': cache_mem_k, 'cache_mem_v': cache_mem_v,
            'page_table': page_table,
            'w_in': w_in, 'w_mem_kv': w_mem_kv, 'a_re': a_re, 'a_im': a_im, 'log_dt': log_dt,
            'b_re': b_re, 'b_im': b_im, 'c_re': c_re, 'c_im': c_im, 'd_skip': d_skip,
            'w_glu': w_glu, 'b_glu': b_glu, 'w_out': w_out, 'ln_g': ln_g, 'ln_b': ln_b}


def reference(x_prompt, x_sample, mem_prompt, cache_k, cache_v, state_ssm_re, state_ssm_im,
              cache_mem_k, cache_mem_v, page_table, w_in, w_mem_kv, a_re, a_im, log_dt,
              b_re, b_im, c_re, c_im, d_skip, w_glu, b_glu, w_out, ln_g, ln_b):
    disc = _ssm_discretize(a_re, a_im, log_dt, b_re, b_im)
    y_prompt, y_sample = x_prompt, x_sample
    for _layer in range(DEPTH):
        q_p, k_p, v_p, u_p, qm_p, gate_p = _split_proj(y_prompt, w_in)
        attn_p = _moba_prompt(q_p, k_p, v_p)
        h0 = jnp.zeros((y_prompt.shape[0], SSM_GROUPS, SSM_STATE), F32)
        ssm_p, ssm_re_p, ssm_im_p = _ssm_branch(u_p, h0, h0, disc, c_re, c_im, d_skip, w_glu, b_glu)
        mk_p, mv_p = _mem_kv(mem_prompt, w_mem_kv)
        mem_p = _mem_attn(qm_p, mk_p, mv_p)
        y_prompt = _merge(y_prompt, attn_p, ssm_p, mem_p, gate_p, w_out, ln_g, ln_b)
        q_s, k_s, v_s, u_s, qm_s, gate_s = _split_proj(y_sample, w_in)
        attn_s = _moba_sample(q_s, k_s, v_s, cache_k, cache_v, page_table)
        ssm_s, ssm_re_s, ssm_im_s = _ssm_branch(u_s, state_ssm_re, state_ssm_im, disc,
                                                c_re, c_im, d_skip, w_glu, b_glu)
        mem_s = _mem_attn(qm_s, cache_mem_k, cache_mem_v)
        y_sample = _merge(y_sample, attn_s, ssm_s, mem_s, gate_s, w_out, ln_g, ln_b)
    return (y_prompt, y_sample, k_p, v_p, ssm_re_p, ssm_im_p, mk_p, mv_p,
            k_s, v_s, ssm_re_s, ssm_im_s)
```

```python
import functools
import math

import jax
import jax.numpy as jnp
from jax import lax
from jax.experimental import pallas as pl
from jax.experimental.pallas import tpu as pltpu

F32 = jnp.float32
BF16 = jnp.bfloat16

D_MODEL = 1024
D_ATTN = 512
D_SSM = 256
D_MEM = 256
D_MIX = D_ATTN + D_SSM + D_MEM
HEAD_DIM = 64
N_HEADS_A = D_ATTN // HEAD_DIM
MOBA_BLOCK = 256
MOBA_TOPK = 3
SSM_CH = 16
SSM_GROUPS = D_SSM // SSM_CH
SSM_STATE = 64
SSM_LANES = SSM_GROUPS * SSM_STATE
MEM_HEADS = 4
D_IN = 3 * D_ATTN + D_SSM + D_MEM + D_MIX
DEPTH = 1
DEEPNORM_ALPHA = (2.0 * DEPTH) ** 0.25
LN_EPS = 1e-5
NEG_INF = -1e30

LANES = 128
SUBLANES = 8
SSM_CHAINS = SUBLANES

_NT = (((1,), (1,)), ((), ()))


def _vmem_limit(nbytes):
    return pltpu.CompilerParams(vmem_limit_bytes=int(nbytes))


def _proj_kernel(x_ref, w_ref, *out_refs, with_prompt_extras):
    xb = x_ref[...].astype(BF16)

    def seg(lo, hi):
        return jnp.dot(xb, w_ref[:, lo:hi], preferred_element_type=F32)

    c0, c1, c2, c3, c4 = D_ATTN, 2 * D_ATTN, 3 * D_ATTN, 3 * D_ATTN + D_SSM, 3 * D_ATTN + D_SSM + D_MEM
    if with_prompt_extras:
        q_ref, k_ref, v_ref, u_ref, qm_ref, gate_ref, kb_ref, vb_ref, kmean_ref = out_refs
    else:
        q_ref, k_ref, v_ref, u_ref, qm_ref, gate_ref = out_refs
    q_ref[...] = seg(0, c0)
    k = seg(c0, c1)
    k_ref[...] = k
    v = seg(c1, c2)
    v_ref[...] = v
    u_ref[...] = seg(c2, c3)
    qm_ref[...] = seg(c3, c4).astype(qm_ref.dtype)
    gate_ref[...] = seg(c4, D_IN)
    if with_prompt_extras:
        kb_ref[...] = k.astype(BF16)
        vb_ref[...] = v.astype(BF16)
        tm = k.shape[0]
        kmean_ref[0] = jnp.sum(k.reshape(tm // MOBA_BLOCK, MOBA_BLOCK, D_ATTN), axis=1) * (1.0 / MOBA_BLOCK)


def _in_proj_prompt(x2d, w_bf, batch, seq, tm):
    m = x2d.shape[0]
    n_seg = SSM_CHAINS // batch
    seg_len = seq // n_seg
    tiles_per_seq = seq // tm
    tiles_per_seg = seg_len // tm
    row = lambda i: (i, 0)

    def u_map(i):
        b = i // tiles_per_seq
        t = i % tiles_per_seq
        return (t % tiles_per_seg, b * n_seg + t // tiles_per_seg)

    out_shape = (
        jax.ShapeDtypeStruct((m, D_ATTN), F32),
        jax.ShapeDtypeStruct((m, D_ATTN), F32),
        jax.ShapeDtypeStruct((m, D_ATTN), F32),
        jax.ShapeDtypeStruct((seg_len, SSM_CHAINS * D_SSM), F32),
        jax.ShapeDtypeStruct((m, D_MEM), BF16),
        jax.ShapeDtypeStruct((m, D_MIX), F32),
        jax.ShapeDtypeStruct((m, D_ATTN), BF16),
        jax.ShapeDtypeStruct((m, D_ATTN), BF16),
        jax.ShapeDtypeStruct((m // tm, tm // MOBA_BLOCK, D_ATTN), F32),
    )
    out_specs = (
        pl.BlockSpec((tm, D_ATTN), row),
        pl.BlockSpec((tm, D_ATTN), row),
        pl.BlockSpec((tm, D_ATTN), row),
        pl.BlockSpec((tm, D_SSM), u_map),
        pl.BlockSpec((tm, D_MEM), row),
        pl.BlockSpec((tm, D_MIX), row),
        pl.BlockSpec((tm, D_ATTN), row),
        pl.BlockSpec((tm, D_ATTN), row),
        pl.BlockSpec((1, tm // MOBA_BLOCK, D_ATTN), lambda i: (i, 0, 0)),
    )
    return pl.pallas_call(
        functools.partial(_proj_kernel, with_prompt_extras=True),
        grid=(m // tm,),
        in_specs=[pl.BlockSpec((tm, D_MODEL), row),
                  pl.BlockSpec((D_MODEL, D_IN), lambda i: (0, 0), pipeline_mode=pl.Buffered(1))],
        out_specs=out_specs,
        out_shape=out_shape,
        compiler_params=_vmem_limit(52 * 2**20),
        name="in_proj_prompt",
    )(x2d, w_bf)


def _in_proj_sample(x2d, w_bf, tm):
    m = x2d.shape[0]
    row = lambda i: (i, 0)
    out_shape = (
        jax.ShapeDtypeStruct((m, D_ATTN), F32),
        jax.ShapeDtypeStruct((m, D_ATTN), F32),
        jax.ShapeDtypeStruct((m, D_ATTN), F32),
        jax.ShapeDtypeStruct((m, D_SSM), F32),
        jax.ShapeDtypeStruct((m, D_MEM), BF16),
        jax.ShapeDtypeStruct((m, D_MIX), F32),
    )
    widths = (D_ATTN, D_ATTN, D_ATTN, D_SSM, D_MEM, D_MIX)
    return pl.pallas_call(
        functools.partial(_proj_kernel, with_prompt_extras=False),
        grid=(m // tm,),
        in_specs=[pl.BlockSpec((tm, D_MODEL), row),
                  pl.BlockSpec((D_MODEL, D_IN), lambda i: (0, 0), pipeline_mode=pl.Buffered(1))],
        out_specs=tuple(pl.BlockSpec((tm, w), row) for w in widths),
        out_shape=out_shape,
        compiler_params=_vmem_limit(52 * 2**20),
        name="in_proj_sample",
    )(x2d, w_bf)


def _mem_kv_kernel(mem_ref, w_ref, mk_ref, mv_ref):
    kv = jnp.dot(mem_ref[...].astype(BF16), w_ref[...], preferred_element_type=F32)
    mk_ref[...] = kv[:, :D_MEM]
    mv_ref[...] = kv[:, D_MEM:]


def _mem_kv(mem2d, w_bf):
    m = mem2d.shape[0]
    tm = min(m, 512)
    row = lambda i: (i, 0)
    return pl.pallas_call(
        _mem_kv_kernel,
        grid=(m // tm,),
        in_specs=[pl.BlockSpec((tm, D_MODEL), row),
                  pl.BlockSpec((D_MODEL, 2 * D_MEM), lambda i: (0, 0))],
        out_specs=(pl.BlockSpec((tm, D_MEM), row), pl.BlockSpec((tm, D_MEM), row)),
        out_shape=(jax.ShapeDtypeStruct((m, D_MEM), F32), jax.ShapeDtypeStruct((m, D_MEM), F32)),
        name="mem_kv",
    )(mem2d, w_bf)


def _select_topk(scores, valid, lane):
    lowest = jnp.finfo(F32).min
    s = jnp.where(valid, scores, lowest)
    sel = jnp.zeros(scores.shape, dtype=jnp.bool_)
    for _ in range(MOBA_TOPK):
        m = jnp.max(s, axis=1, keepdims=True)
        idx = jnp.min(jnp.where(s == m, lane, jnp.int32(2**30)), axis=1, keepdims=True)
        pick = lane == idx
        sel = sel | (pick & valid)
        s = jnp.where(pick, lowest, s)
    return sel


def _moba_prompt_kernel(q_ref, k_ref, v_ref, kmean_ref, o_ref, kaug_ref, vaug_ref, m_ref, acc_ref):
    i = pl.program_id(2)
    blk = MOBA_BLOCK
    nb = k_ref.shape[1] // blk
    half = LANES // 2
    lane_b = lax.broadcasted_iota(jnp.int32, (blk, LANES), 1)
    head0_b = lane_b < half

    @pl.when(i == 0)
    def _build_operands():
        def body(j, carry):
            rows = pl.ds(pl.multiple_of(j * blk, blk), blk)
            kb = k_ref[0, rows, :].astype(F32)
            vb = v_ref[0, rows, :].astype(F32)
            hot0 = jnp.where(lane_b == half + j, 1.0, 0.0)
            hot1 = jnp.where(lane_b == j, 1.0, 0.0)
            kaug_ref[0, rows, :] = jnp.where(head0_b, kb, hot0).astype(BF16)
            kaug_ref[1, rows, :] = jnp.where(head0_b, hot1, kb).astype(BF16)
            vaug_ref[0, rows, :] = jnp.where(head0_b, vb, 1.0).astype(BF16)
            vaug_ref[1, rows, :] = jnp.where(head0_b, 1.0, vb).astype(BF16)
            return carry
        lax.fori_loop(0, nb, body, 0)

    q2 = q_ref[0]
    means = kmean_ref[0]
    lane_m = lax.broadcasted_iota(jnp.int32, (nb, LANES), 1)
    means_p = jnp.concatenate([
        jnp.where(lane_m >= half, means, 0.0),
        jnp.zeros((half - nb, LANES), F32),
        jnp.where(lane_m < half, means, 0.0),
        jnp.zeros((half - nb, LANES), F32)], axis=0)
    sc = lax.dot_general(q2, means_p, _NT, precision=lax.Precision.HIGHEST,
                         preferred_element_type=F32)

    scale = HEAD_DIM ** -0.5
    lhs = []
    for e, off in ((0, half), (1, 0)):
        valid = (lane_b >= off) & (lane_b < off + i)
        sel = _select_topk(sc, valid, lane_b)
        bias = jnp.where(sel | (lane_b == off + i), 0.0, NEG_INF)
        mine = head0_b if e == 0 else jnp.logical_not(head0_b)
        lhs.append(jnp.where(mine, q2 * scale, bias).astype(BF16))

    def scores(e, j):
        rows = pl.ds(pl.multiple_of(j * blk, blk), blk)
        s = lax.dot_general(lhs[e], kaug_ref[e, rows, :], _NT, preferred_element_type=F32)
        return s, rows

    r_i = lax.broadcasted_iota(jnp.int32, (blk, blk), 0)
    c_i = lax.broadcasted_iota(jnp.int32, (blk, blk), 1)
    causal = c_i <= r_i
    for e in (0, 1):
        s, rows = scores(e, i)
        s = jnp.where(causal, s, NEG_INF)
        m = jnp.max(s, axis=1, keepdims=True)
        p = jnp.exp(s - m)
        acc_ref[e] = jnp.dot(p.astype(BF16), vaug_ref[e, rows, :], preferred_element_type=F32)
        m_ref[e] = jnp.broadcast_to(m, (blk, LANES))

    def past_block(j, carry):
        for e in (0, 1):
            s, rows = scores(e, j)
            m_old = m_ref[e]
            m_new = jnp.maximum(m_old, jnp.max(s, axis=1, keepdims=True))
            alpha = jnp.exp(m_old - m_new)
            p = jnp.exp(s - jnp.concatenate([m_new, m_new], axis=1))
            pv = jnp.dot(p.astype(BF16), vaug_ref[e, rows, :], preferred_element_type=F32)
            acc_ref[e] = alpha * acc_ref[e] + pv
            m_ref[e] = m_new
        return carry

    lax.fori_loop(0, i, past_block, 0)

    a0 = acc_ref[0]
    a1 = acc_ref[1]
    o0 = a0 / pltpu.roll(a0, half, axis=1)
    o1 = a1 / pltpu.roll(a1, half, axis=1)
    o_ref[0] = jnp.where(head0_b, o0, o1)


def _moba_prompt(q, kb, vb, kmean):
    b, s, _ = q.shape
    nb = s // MOBA_BLOCK
    assert nb <= LANES // 2 and nb % SUBLANES == 0
    return pl.pallas_call(
        _moba_prompt_kernel,
        grid=(b, D_ATTN // LANES, nb),
        in_specs=[pl.BlockSpec((1, MOBA_BLOCK, LANES), lambda bi, hp, i: (bi, i, hp)),
                  pl.BlockSpec((1, s, LANES), lambda bi, hp, i: (bi, 0, hp)),
                  pl.BlockSpec((1, s, LANES), lambda bi, hp, i: (bi, 0, hp)),
                  pl.BlockSpec((1, nb, LANES), lambda bi, hp, i: (bi, 0, hp))],
        out_specs=pl.BlockSpec((1, MOBA_BLOCK, LANES), lambda bi, hp, i: (bi, i, hp)),
        out_shape=jax.ShapeDtypeStruct((b, s, D_ATTN), F32),
        scratch_shapes=[pltpu.VMEM((2, s, LANES), BF16), pltpu.VMEM((2, s, LANES), BF16),
                        pltpu.VMEM((2, MOBA_BLOCK, LANES), F32), pltpu.VMEM((2, MOBA_BLOCK, LANES), F32)],
        compiler_params=pltpu.CompilerParams(
            dimension_semantics=("arbitrary", "arbitrary", "arbitrary"),
            vmem_limit_bytes=40 * 2**20),
        name="moba_prompt",
    )(q, kb, vb, kmean)


def _ssm_disc_kernel(a_re_ref, a_im_ref, log_dt_ref, bt_re_ref, bt_im_ref,
                     abar_re_ref, abar_im_ref, bbt_re_ref, bbt_im_ref):
    a_re = a_re_ref[...]
    a_im = a_im_ref[...]
    dt = jnp.exp(log_dt_ref[...])
    mag = jnp.exp(a_re * dt)
    ang = a_im * dt
    abar_re = mag * jnp.cos(ang)
    abar_im = mag * jnp.sin(ang)
    den = a_re * a_re + a_im * a_im
    f_re = ((abar_re - 1.0) * a_re + abar_im * a_im) / den
    f_im = (abar_im * a_re - (abar_re - 1.0) * a_im) / den
    abar_re_ref[...] = abar_re
    abar_im_ref[...] = abar_im
    bt_re = bt_re_ref[...]
    bt_im = bt_im_ref[...]
    bbt_re_ref[...] = f_re[:, None, :] * bt_re - f_im[:, None, :] * bt_im
    bbt_im_ref[...] = f_re[:, None, :] * bt_im + f_im[:, None, :] * bt_re


def _ssm_operands(a_re, a_im, log_dt, b_re, b_im, c_re, c_im):
    g, p, c = b_re.shape
    abar_re, abar_im, bbt_re, bbt_im = pl.pallas_call(
        _ssm_disc_kernel,
        out_shape=(jax.ShapeDtypeStruct((g, p), F32), jax.ShapeDtypeStruct((g, p), F32),
                   jax.ShapeDtypeStruct((g, c, p), F32), jax.ShapeDtypeStruct((g, c, p), F32)),
        name="ssm_discretize",
    )(a_re.astype(F32), a_im.astype(F32), log_dt.astype(F32).reshape(g, 1),
      b_re.astype(F32).transpose(0, 2, 1), b_im.astype(F32).transpose(0, 2, 1))
    eye = jnp.eye(g, dtype=F32)

    def in_block_diag(bbt):
        return jnp.einsum('gcp,gh->gchp', bbt, eye).reshape(g * c, g * p)

    def out_block_diag(cm):
        return jnp.einsum('gcp,gh->gphc', cm, eye).reshape(g * p, g * c)

    b_mat = jnp.concatenate([in_block_diag(bbt_re), in_block_diag(bbt_im)], axis=1)
    c_mat = jnp.concatenate([out_block_diag(c_re.astype(F32)),
                             -out_block_diag(c_im.astype(F32))], axis=0)
    a_row = jnp.concatenate([abar_re.reshape(1, g * p), abar_im.reshape(1, g * p)], axis=1)
    return a_row, b_mat, c_mat


def _complex_step(a_re, a_im, h_re, h_im, bu_re, bu_im):
    return (a_re * h_re - a_im * h_im + bu_re, a_re * h_im + a_im * h_re + bu_im)


def _ssm_output(h_all, u, c_ref, d_ref, wg_ref, bg_ref):
    y = jnp.dot(h_all.astype(BF16), c_ref[...], preferred_element_type=F32) + d_ref[...] * u
    y = jax.nn.gelu(y)
    z = jnp.dot(y.astype(BF16), wg_ref[...], preferred_element_type=F32) + bg_ref[...]
    return z[:, :D_SSM] * jax.nn.sigmoid(z[:, D_SSM:])


def _ssm_prompt_kernel(u_ref, a_ref, b_ref, c_ref, d_ref, wg_ref, bg_ref, y_ref, hfin_ref,
                       hb_ref, h_ref, init_ref, *, n_seg):
    pss = pl.program_id(0)
    c = pl.program_id(1)
    n_chunks = pl.num_programs(1)
    rows = u_ref.shape[0]
    steps = rows // SSM_CHAINS
    n = SSM_LANES
    a_re = jnp.broadcast_to(a_ref[:, :n], (SSM_CHAINS, n))
    a_im = jnp.broadcast_to(a_ref[:, n:], (SSM_CHAINS, n))

    @pl.when((pss == 0) & (c == 0))
    def _zero_state():
        h_ref[...] = jnp.zeros_like(h_ref)

    @pl.when((pss == 1) & (c == 0))
    def _true_initial_state():
        h_ref[...] = init_ref[...]

    u = u_ref[...]
    hb_ref[...] = jnp.dot(u.astype(BF16), b_ref[...], preferred_element_type=F32)

    def scan(store):
        def step(s, carry):
            h_re, h_im = carry
            r = pl.ds(pl.multiple_of(s * SSM_CHAINS, SSM_CHAINS), SSM_CHAINS)
            h_re, h_im = _complex_step(a_re, a_im, h_re, h_im, hb_ref[r, :n], hb_ref[r, n:])
            if store:
                hb_ref[r, :n] = h_re
                hb_ref[r, n:] = h_im
            return h_re, h_im
        h_re, h_im = lax.fori_loop(0, steps, step, (h_ref[:, :n], h_ref[:, n:]))
        h_ref[:, :n] = h_re
        h_ref[:, n:] = h_im

    @pl.when(pss == 0)
    def _local_pass():
        scan(False)

    @pl.when((pss == 0) & (c == n_chunks - 1))
    def _segment_initial_states():
        seg_len = steps * n_chunks
        p_re, p_im = a_re, a_im
        for _ in range(int(math.log2(seg_len))):
            p_re, p_im = p_re * p_re - p_im * p_im, 2.0 * p_re * p_im
        e_re = h_ref[:, :n]
        e_im = h_ref[:, n:]
        first = (lax.broadcasted_iota(jnp.int32, (SSM_CHAINS, n), 0) % n_seg) == 0
        i_re = jnp.zeros((SSM_CHAINS, n), F32)
        i_im = jnp.zeros((SSM_CHAINS, n), F32)
        for _ in range(n_seg - 1):
            t_re = p_re * i_re - p_im * i_im + e_re
            t_im = p_re * i_im + p_im * i_re + e_im
            i_re = jnp.where(first, 0.0, pltpu.roll(t_re, 1, axis=0))
            i_im = jnp.where(first, 0.0, pltpu.roll(t_im, 1, axis=0))
        init_ref[:, :n] = i_re
        init_ref[:, n:] = i_im

    @pl.when(pss == 1)
    def _output_pass():
        scan(True)
        y_ref[...] = _ssm_output(hb_ref[...], u, c_ref, d_ref, wg_ref, bg_ref).astype(y_ref.dtype)

    @pl.when((pss == 1) & (c == n_chunks - 1))
    def _final_state():
        hfin_ref[...] = h_ref[...]


def _ssm_prompt(u_chain, a_row, b_bf, c_bf, d_row, wg_bf, bg_row, n_seg, steps_per_chunk):
    total = u_chain.shape[0]
    rows = steps_per_chunk * SSM_CHAINS
    n_chunks = total // rows
    seg_len = total // SSM_CHAINS
    assert 2 ** int(math.log2(seg_len)) == seg_len
    const = lambda p, c: (0, 0)
    return pl.pallas_call(
        functools.partial(_ssm_prompt_kernel, n_seg=n_seg),
        grid=(2, n_chunks),
        in_specs=[pl.BlockSpec((rows, D_SSM), lambda p, c: (c, 0)),
                  pl.BlockSpec((1, 2 * SSM_LANES), const),
                  pl.BlockSpec((D_SSM, 2 * SSM_LANES), const),
                  pl.BlockSpec((2 * SSM_LANES, D_SSM), const),
                  pl.BlockSpec((1, D_SSM), const),
                  pl.BlockSpec((D_SSM, 2 * D_SSM), const),
                  pl.BlockSpec((1, 2 * D_SSM), const)],
        out_specs=(pl.BlockSpec((rows, D_SSM), lambda p, c: (p * c, 0)),
                   pl.BlockSpec((SSM_CHAINS, 2 * SSM_LANES), const)),
        out_shape=(jax.ShapeDtypeStruct((total, D_SSM), F32),
                   jax.ShapeDtypeStruct((SSM_CHAINS, 2 * SSM_LANES), F32)),
        scratch_shapes=[pltpu.VMEM((rows, 2 * SSM_LANES), F32),
                        pltpu.VMEM((SSM_CHAINS, 2 * SSM_LANES), F32),
                        pltpu.VMEM((SSM_CHAINS, 2 * SSM_LANES), F32)],
        compiler_params=pltpu.CompilerParams(dimension_semantics=("arbitrary", "arbitrary"),
                                             vmem_limit_bytes=40 * 2**20),
        name="ssm_prompt",
    )(u_chain, a_row, b_bf, c_bf, d_row, wg_bf, bg_row)


def _ssm_sample_kernel(u_ref, h0_ref, a_ref, b_ref, c_ref, d_ref, wg_ref, bg_ref, y_ref, hfin_ref):
    steps, batch, _ = u_ref.shape
    n = SSM_LANES
    a_re = jnp.broadcast_to(a_ref[:, :n], (batch, n))
    a_im = jnp.broadcast_to(a_ref[:, n:], (batch, n))
    h_re = h0_ref[:, :n]
    h_im = h0_ref[:, n:]
    for t in range(steps):
        u = u_ref[t]
        bu = jnp.dot(u, b_ref[...], precision=lax.Precision.HIGHEST, preferred_element_type=F32)
        h_re, h_im = _complex_step(a_re, a_im, h_re, h_im, bu[:, :n], bu[:, n:])
        h_all = jnp.concatenate([h_re, h_im], axis=1)
        y_ref[t] = _ssm_output(h_all, u, c_ref, d_ref, wg_ref, bg_ref)
    hfin_ref[:, :n] = h_re
    hfin_ref[:, n:] = h_im


def _ssm_sample(u_tb, h0, a_row, b_f32, c_bf, d_row, wg_bf, bg_row):
    steps, batch, _ = u_tb.shape
    return pl.pallas_call(
        _ssm_sample_kernel,
        out_shape=(jax.ShapeDtypeStruct((steps, batch, D_SSM), F32),
                   jax.ShapeDtypeStruct((batch, 2 * SSM_LANES), F32)),
        compiler_params=_vmem_limit(40 * 2**20),
        name="ssm_sample",
    )(u_tb, h0, a_row, b_f32, c_bf, d_row, wg_bf, bg_row)


def _mem_attn_prompt_kernel(q_ref, mk_ref, mv_ref, o_ref):
    rows = q_ref.shape[0]
    half = LANES // 2
    scale = (D_MEM // MEM_HEADS) ** -0.5
    head0 = lax.broadcasted_iota(jnp.int32, (rows, LANES), 1) < half
    outs = []
    for hp in range(D_MEM // LANES):
        cols = slice(hp * LANES, (hp + 1) * LANES)
        q2 = q_ref[:, cols].astype(F32)
        k2 = mk_ref[:, cols].astype(BF16)
        v2 = mv_ref[:, cols].astype(BF16)
        res = []
        for e in (0, 1):
            mine = head0 if e == 0 else jnp.logical_not(head0)
            lhs = jnp.where(mine, q2, 0.0).astype(BF16)
            s = lax.dot_general(lhs, k2, _NT, preferred_element_type=F32) * scale
            p = jnp.exp(s - jnp.max(s, axis=1, keepdims=True))
            l = jnp.sum(p, axis=1, keepdims=True)
            res.append(jnp.dot(p.astype(BF16), v2, preferred_element_type=F32) / l)
        outs.append(jnp.where(head0, res[0], res[1]))
    o_ref[...] = jnp.concatenate(outs, axis=1)


def _mem_attn_prompt(qm, mk, mv, batch, seq, tm):
    n_mem = mk.shape[0] // batch
    tiles = seq // tm
    return pl.pallas_call(
        _mem_attn_prompt_kernel,
        grid=(batch, tiles),
        in_specs=[pl.BlockSpec((tm, D_MEM), lambda b, i: (b * tiles + i, 0)),
                  pl.BlockSpec((n_mem, D_MEM), lambda b, i: (b, 0)),
                  pl.BlockSpec((n_mem, D_MEM), lambda b, i: (b, 0))],
        out_specs=pl.BlockSpec((tm, D_MEM), lambda b, i: (b * tiles + i, 0)),
        out_shape=jax.ShapeDtypeStruct((batch * seq, D_MEM), F32),
        name="mem_attn_prompt",
    )(qm, mk, mv)


def _head_rows(x, n_heads, head_dim):
    head_of_lane = lax.broadcasted_iota(jnp.int32, x.shape, 1) // head_dim
    return jnp.concatenate([jnp.where(head_of_lane == h, x, 0.0) for h in range(n_heads)], axis=0)


def _head_diagonal(o_full, t, n_heads, head_dim):
    head_of_lane = lax.broadcasted_iota(jnp.int32, (t, o_full.shape[1]), 1) // head_dim
    out = jnp.zeros((t, o_full.shape[1]), F32)
    for h in range(n_heads):
        out = out + jnp.where(head_of_lane == h, o_full[h * t:(h + 1) * t], 0.0)
    return out


def _mem_attn_sample_kernel(q_ref, mk_ref, mv_ref, o_ref):
    group, t, _ = q_ref.shape
    head_dim = D_MEM // MEM_HEADS
    scale = head_dim ** -0.5
    for g in range(group):
        qbd = _head_rows(q_ref[g].astype(F32) * scale, MEM_HEADS, head_dim).astype(BF16)
        s = lax.dot_general(qbd, mk_ref[g].astype(BF16), _NT, preferred_element_type=F32)
        p = jnp.exp(s - jnp.max(s, axis=1, keepdims=True))
        l = jnp.sum(p, axis=1, keepdims=True)
        o_full = jnp.dot(p.astype(BF16), mv_ref[g].astype(BF16), preferred_element_type=F32) / l
        o_ref[g] = _head_diagonal(o_full, t, MEM_HEADS, head_dim)


def _mem_attn_sample(qm, mk, mv, group):
    db, t, _ = qm.shape
    n_mem = mk.shape[1]
    blk = lambda i: (i, 0, 0)
    return pl.pallas_call(
        _mem_attn_sample_kernel,
        grid=(db // group,),
        in_specs=[pl.BlockSpec((group, t, D_MEM), blk),
                  pl.BlockSpec((group, n_mem, D_MEM), blk),
                  pl.BlockSpec((group, n_mem, D_MEM), blk)],
        out_specs=pl.BlockSpec((group, t, D_MEM), blk),
        out_shape=jax.ShapeDtypeStruct((db, t, D_MEM), F32),
        name="mem_attn_sample",
    )(qm, mk, mv)


def _merge_kernel(x_ref, attn_ref, ssm_ref, mem_ref, gate_ref, w_ref, g_ref, b_ref, y_ref):
    mixed = jnp.concatenate([attn_ref[...], ssm_ref[...], mem_ref[...]], axis=1) * jax.nn.silu(gate_ref[...])
    out = jnp.dot(mixed.astype(BF16), w_ref[...], preferred_element_type=F32)
    z = DEEPNORM_ALPHA * x_ref[...] + out
    mu = jnp.mean(z, axis=1, keepdims=True)
    zc = z - mu
    var = jnp.mean(zc * zc, axis=1, keepdims=True)
    y_ref[...] = zc * lax.rsqrt(var + LN_EPS) * g_ref[...] + b_ref[...]


def _merge(x2d, attn, ssm, mem, gate, w_bf, g_row, b_row, tm, ssm_map):
    m = x2d.shape[0]
    row = lambda i: (i, 0)
    const = lambda i: (0, 0)
    return pl.pallas_call(
        _merge_kernel,
        grid=(m // tm,),
        in_specs=[pl.BlockSpec((tm, D_MODEL), row),
                  pl.BlockSpec((tm, D_ATTN), row),
                  pl.BlockSpec((tm, D_SSM), ssm_map),
                  pl.BlockSpec((tm, D_MEM), row),
                  pl.BlockSpec((tm, D_MIX), row),
                  pl.BlockSpec((D_MIX, D_MODEL), const),
                  pl.BlockSpec((1, D_MODEL), const),
                  pl.BlockSpec((1, D_MODEL), const)],
        out_specs=pl.BlockSpec((tm, D_MODEL), row),
        out_shape=jax.ShapeDtypeStruct((m, D_MODEL), F32),
        compiler_params=_vmem_limit(48 * 2**20),
        name="merge",
    )(x2d, attn, ssm, mem, gate, w_bf, g_row, b_row)


def _moba_sample_kernel(pt_ref, q_ref, kn_ref, vn_ref, *refs, pages_per_step, chunks, page_size):
    del pt_ref
    pps = pages_per_step
    k_refs = refs[:pps]
    v_refs = refs[pps:2 * pps]
    o_ref = refs[2 * pps]
    s_ref, p_ref, ksum_ref, acc_ref, l_ref, pown_ref = refs[2 * pps + 1:]
    r = pl.program_id(0) % (2 * chunks)
    t = q_ref.shape[1]
    rows = N_HEADS_A * t
    n_pages = pps * chunks
    pages_per_block = MOBA_BLOCK // page_size
    n_blocks = n_pages // pages_per_block
    blocks_per_step = pps // pages_per_block
    qbd = _head_rows(q_ref[0] * (HEAD_DIM ** -0.5), N_HEADS_A, HEAD_DIM)
    qbd_bf = qbd.astype(BF16)
    lane = lax.broadcasted_iota(jnp.int32, (rows, page_size), 1)

    def pad_rows(x):
        return jnp.concatenate([x, jnp.zeros((page_size - x.shape[0], x.shape[1]), x.dtype)], axis=0)

    @pl.when(r < chunks)
    def _key_pages():
        sums = []
        for i in range(pps):
            kp = k_refs[i][0]
            s_ref[r * pps + i] = lax.dot_general(qbd_bf, kp.astype(BF16), _NT, preferred_element_type=F32)
            sums.append(jnp.sum(kp, axis=0, keepdims=True))
        block_sums = [sum(sums[m * pages_per_block:(m + 1) * pages_per_block]) for m in range(blocks_per_step)]
        ksum_ref[pl.ds(pl.multiple_of(r * blocks_per_step, blocks_per_step), blocks_per_step), :] = (
            jnp.concatenate(block_sums, axis=0))

    @pl.when(r == chunks)
    def _select_and_softmax():
        means = pad_rows(ksum_ref[...] * (1.0 / MOBA_BLOCK))
        sc = lax.dot_general(qbd, means, _NT, precision=lax.Precision.HIGHEST, preferred_element_type=F32)
        sel = _select_topk(sc, lane < n_blocks, lane)
        lo = lax.dot_general(qbd_bf, pad_rows(kn_ref[0]).astype(BF16), _NT, preferred_element_type=F32)
        t_row = lax.broadcasted_iota(jnp.int32, (rows, page_size), 0) % t
        lo = jnp.where(lane <= t_row, lo, NEG_INF)
        mx = lo
        for j in range(n_blocks):
            bm = s_ref[j * pages_per_block]
            for pg in range(1, pages_per_block):
                bm = jnp.maximum(bm, s_ref[j * pages_per_block + pg])
            mx = jnp.maximum(mx, jnp.where(sel[:, j:j + 1], bm, NEG_INF))
        m = jnp.max(mx, axis=1, keepdims=True)
        po = jnp.exp(lo - m)
        pown_ref[...] = po.astype(BF16)
        lsum = po
        for j in range(n_blocks):
            sj = sel[:, j:j + 1]
            for pg in range(j * pages_per_block, (j + 1) * pages_per_block):
                p = jnp.where(sj, jnp.exp(s_ref[pg] - m), 0.0)
                lsum = lsum + p
                p_ref[pg] = p.astype(BF16)
        l_ref[...] = jnp.broadcast_to(jnp.sum(lsum, axis=1, keepdims=True), l_ref.shape)
        acc_ref[...] = jnp.zeros_like(acc_ref)

    @pl.when(r >= chunks)
    def _value_pages():
        acc = acc_ref[...]
        for i in range(pps):
            acc = acc + jnp.dot(p_ref[(r - chunks) * pps + i], v_refs[i][0].astype(BF16),
                                preferred_element_type=F32)
        acc_ref[...] = acc

    @pl.when(r == 2 * chunks - 1)
    def _finish():
        acc = acc_ref[...] + jnp.dot(pown_ref[...], pad_rows(vn_ref[0]).astype(BF16),
                                     preferred_element_type=F32)
        l = l_ref[...]
        o_full = acc / jnp.concatenate([l] * (D_ATTN // page_size), axis=1)
        o_ref[0] = _head_diagonal(o_full, t, N_HEADS_A, HEAD_DIM)


def _moba_sample(q, k_new, v_new, cache_k, cache_v, page_table, pages_per_step):
    db, t, _ = q.shape
    n_pages = page_table.shape[1]
    page_size = cache_k.shape[1]
    chunks = n_pages // pages_per_step
    steps = 2 * chunks
    assert (n_pages * page_size) % MOBA_BLOCK == 0 and MOBA_BLOCK % page_size == 0
    assert n_pages * page_size // MOBA_BLOCK >= MOBA_TOPK and t <= page_size and page_size == LANES
    rows = N_HEADS_A * t

    def seq_map(n, pt):
        return (n // steps, 0, 0)

    def key_map(i):
        def index(n, pt):
            b = n // steps
            c = jnp.minimum(n % steps, chunks - 1)
            return (pt[b, c * pages_per_step + i], 0, 0)
        return index

    def value_map(i):
        def index(n, pt):
            b = n // steps
            r = n % steps
            in_phase = r >= chunks
            vb = jnp.where(in_phase, b, jnp.maximum(b - 1, 0))
            vc = jnp.where(in_phase, r - chunks, jnp.where(b > 0, chunks - 1, 0))
            return (pt[vb, vc * pages_per_step + i], 0, 0)
        return index

    page_block = (1, page_size, D_ATTN)
    in_specs = ([pl.BlockSpec((1, t, D_ATTN), seq_map)] * 3
                + [pl.BlockSpec(page_block, key_map(i)) for i in range(pages_per_step)]
                + [pl.BlockSpec(page_block, value_map(i)) for i in range(pages_per_step)])
    grid_spec = pltpu.PrefetchScalarGridSpec(
        num_scalar_prefetch=1,
        grid=(db * steps,),
        in_specs=in_specs,
        out_specs=pl.BlockSpec((1, t, D_ATTN), seq_map),
        scratch_shapes=[pltpu.VMEM((n_pages, rows, page_size), F32),
                        pltpu.VMEM((n_pages, rows, page_size), BF16),
                        pltpu.VMEM((n_pages * page_size // MOBA_BLOCK, D_ATTN), F32),
                        pltpu.VMEM((rows, D_ATTN), F32),
                        pltpu.VMEM((rows, page_size), F32),
                        pltpu.VMEM((rows, page_size), BF16)])
    return pl.pallas_call(
        functools.partial(_moba_sample_kernel, pages_per_step=pages_per_step, chunks=chunks,
                          page_size=page_size),
        grid_spec=grid_spec,
        out_shape=jax.ShapeDtypeStruct((db, t, D_ATTN), F32),
        compiler_params=pltpu.CompilerParams(dimension_semantics=("arbitrary",),
                                             vmem_limit_bytes=44 * 2**20),
        name="moba_sample",
    )(page_table, q, k_new, v_new, *([cache_k] * pages_per_step), *([cache_v] * pages_per_step))


def kernel(x_prompt, x_sample, mem_prompt, cache_k, cache_v, state_ssm_re, state_ssm_im,
           cache_mem_k, cache_mem_v, page_table, w_in, w_mem_kv, a_re, a_im, log_dt,
           b_re, b_im, c_re, c_im, d_skip, w_glu, b_glu, w_out, ln_g, ln_b):
    batch, seq, _ = x_prompt.shape
    db, t_new, _ = x_sample.shape
    n_mem = mem_prompt.shape[1]
    n_phys, page_size = cache_k.shape[:2]
    n_seg = SSM_CHAINS // batch
    seg_len = seq // n_seg
    tm = min(512, seg_len)

    w_in_bf = w_in.astype(BF16)
    w_mem_bf = w_mem_kv.astype(BF16)
    w_out_bf = w_out.astype(BF16)
    wg_bf = w_glu.astype(BF16)
    bg_row = b_glu.astype(F32).reshape(1, 2 * D_SSM)
    d_row = d_skip.astype(F32).reshape(1, D_SSM)
    g_row = ln_g.astype(F32).reshape(1, D_MODEL)
    b_row = ln_b.astype(F32).reshape(1, D_MODEL)
    a_row, b_mat, c_mat = _ssm_operands(a_re, a_im, log_dt, b_re, b_im, c_re, c_im)
    b_bf = b_mat.astype(BF16)
    c_bf = c_mat.astype(BF16)

    x2d = x_prompt.reshape(batch * seq, D_MODEL)
    q, k, v, u_chain, qm, gate, kb, vb, kmean = _in_proj_prompt(x2d, w_in_bf, batch, seq, tm)
    attn = _moba_prompt(q.reshape(batch, seq, D_ATTN), kb.reshape(batch, seq, D_ATTN),
                        vb.reshape(batch, seq, D_ATTN), kmean.reshape(batch, seq // MOBA_BLOCK, D_ATTN))
    y_chain, hfin = _ssm_prompt(u_chain.reshape(seg_len * SSM_CHAINS, D_SSM), a_row, b_bf, c_bf, d_row,
                                wg_bf, bg_row, n_seg, steps_per_chunk=min(64, seg_len))
    mk, mv = _mem_kv(mem_prompt.reshape(batch * n_mem, D_MODEL), w_mem_bf)
    mem_o = _mem_attn_prompt(qm, mk, mv, batch, seq, tm)
    tiles_per_seq = seq // tm
    tiles_per_seg = seg_len // tm

    def chain_map(i):
        b = i // tiles_per_seq
        ti = i % tiles_per_seq
        return (ti % tiles_per_seg, b * n_seg + ti // tiles_per_seg)

    y_prompt = _merge(x2d, attn.reshape(batch * seq, D_ATTN), y_chain.reshape(seg_len, SSM_CHAINS * D_SSM),
                      mem_o, gate, w_out_bf, g_row, b_row, tm, chain_map)
    h_last = hfin[n_seg - 1::n_seg]
    ssm_re_p = h_last[:, :SSM_LANES].reshape(batch, SSM_GROUPS, SSM_STATE)
    ssm_im_p = h_last[:, SSM_LANES:].reshape(batch, SSM_GROUPS, SSM_STATE)

    xs2d = x_sample.reshape(db * t_new, D_MODEL)
    tms = min(512, db * t_new)
    q_s, k_s, v_s, u_s, qm_s, gate_s = _in_proj_sample(xs2d, w_in_bf, tms)
    attn_s = _moba_sample(q_s.reshape(db, t_new, D_ATTN), k_s.reshape(db, t_new, D_ATTN),
                          v_s.reshape(db, t_new, D_ATTN), cache_k.reshape(n_phys, page_size, D_ATTN),
                          cache_v.reshape(n_phys, page_size, D_ATTN), page_table,
                          pages_per_step=min(16, page_table.shape[1]))
    h0 = jnp.concatenate([state_ssm_re.astype(F32).reshape(db, SSM_LANES),
                          state_ssm_im.astype(F32).reshape(db, SSM_LANES)], axis=1)
    y_tb, hfin_s = _ssm_sample(u_s.reshape(db, t_new, D_SSM).transpose(1, 0, 2), h0, a_row, b_mat, c_bf,
                               d_row, wg_bf, bg_row)
    ssm_s = y_tb.transpose(1, 0, 2).reshape(db * t_new, D_SSM)
    mem_s = _mem_attn_sample(qm_s.reshape(db, t_new, D_MEM), cache_mem_k.reshape(db, n_mem, D_MEM),
                             cache_mem_v.reshape(db, n_mem, D_MEM), group=min(8, db))
    y_sample = _merge(xs2d, attn_s.reshape(db * t_new, D_ATTN), ssm_s, mem_s.reshape(db * t_new, D_MEM),
                      gate_s, w_out_bf, g_row, b_row, tms, lambda i: (i, 0))

    heads = (N_HEADS_A, HEAD_DIM)
    mem_heads = (MEM_HEADS, D_MEM // MEM_HEADS)
    return (y_prompt.reshape(batch, seq, D_MODEL), y_sample.reshape(db, t_new, D_MODEL),
            k.reshape(batch, seq, *heads), v.reshape(batch, seq, *heads), ssm_re_p, ssm_im_p,
            mk.reshape(batch, n_mem, *mem_heads), mv.reshape(batch, n_mem, *mem_heads),
            k_s.reshape(db, t_new, *heads), v_s.reshape(db, t_new, *heads),
            hfin_s[:, :SSM_LANES].reshape(db, SSM_GROUPS, SSM_STATE),
            hfin_s[:, SSM_LANES:].reshape(db, SSM_GROUPS, SSM_STATE))
```

```python
import functools
import math

import jax
import jax.numpy as jnp
from jax import lax
from jax.experimental import pallas as pl
from jax.experimental.pallas import tpu as pltpu

F32 = jnp.float32
BF16 = jnp.bfloat16

D_MODEL = 1024
D_ATTN = 512
D_SSM = 256
D_MEM = 256
D_MIX = D_ATTN + D_SSM + D_MEM
HEAD_DIM = 64
N_HEADS_A = D_ATTN // HEAD_DIM
MOBA_BLOCK = 256
MOBA_TOPK = 3
SSM_CH = 16
SSM_GROUPS = D_SSM // SSM_CH
SSM_STATE = 64
SSM_LANES = SSM_GROUPS * SSM_STATE
MEM_HEADS = 4
D_IN = 3 * D_ATTN + D_SSM + D_MEM + D_MIX
DEPTH = 1
DEEPNORM_ALPHA = (2.0 * DEPTH) ** 0.25
LN_EPS = 1e-5
NEG_INF = -1e30

LANES = 128
SUBLANES = 8
SSM_CHAINS = SUBLANES

_NT = (((1,), (1,)), ((), ()))


def _vmem_limit(nbytes):
    return pltpu.CompilerParams(vmem_limit_bytes=int(nbytes))


def _proj_kernel(x_ref, w_ref, *out_refs, with_prompt_extras):
    xb = x_ref[...].astype(BF16)

    def seg(lo, hi):
        return jnp.dot(xb, w_ref[:, lo:hi], preferred_element_type=F32)

    c0, c1, c2, c3, c4 = D_ATTN, 2 * D_ATTN, 3 * D_ATTN, 3 * D_ATTN + D_SSM, 3 * D_ATTN + D_SSM + D_MEM
    if with_prompt_extras:
        q_ref, k_ref, v_ref, u_ref, qm_ref, gate_ref, kb_ref, vb_ref, kmean_ref = out_refs
    else:
        q_ref, k_ref, v_ref, u_ref, qm_ref, gate_ref = out_refs
    q_ref[...] = seg(0, c0)
    k = seg(c0, c1)
    k_ref[...] = k
    v = seg(c1, c2)
    v_ref[...] = v
    u_ref[...] = seg(c2, c3)
    qm_ref[...] = seg(c3, c4).astype(qm_ref.dtype)
    gate_ref[...] = seg(c4, D_IN)
    if with_prompt_extras:
        kb_ref[...] = k.astype(BF16)
        vb_ref[...] = v.astype(BF16)
        tm = k.shape[0]
        kmean_ref[0] = jnp.sum(k.reshape(tm // MOBA_BLOCK, MOBA_BLOCK, D_ATTN), axis=1) * (1.0 / MOBA_BLOCK)


def _in_proj_prompt(x2d, w_bf, batch, seq, tm):
    m = x2d.shape[0]
    n_seg = SSM_CHAINS // batch
    seg_len = seq // n_seg
    tiles_per_seq = seq // tm
    tiles_per_seg = seg_len // tm
    row = lambda i: (i, 0)

    def u_map(i):
        b = i // tiles_per_seq
        t = i % tiles_per_seq
        return (t % tiles_per_seg, b * n_seg + t // tiles_per_seg)

    out_shape = (
        jax.ShapeDtypeStruct((m, D_ATTN), F32),
        jax.ShapeDtypeStruct((m, D_ATTN), F32),
        jax.ShapeDtypeStruct((m, D_ATTN), F32),
        jax.ShapeDtypeStruct((seg_len, SSM_CHAINS * D_SSM), F32),
        jax.ShapeDtypeStruct((m, D_MEM), BF16),
        jax.ShapeDtypeStruct((m, D_MIX), F32),
        jax.ShapeDtypeStruct((m, D_ATTN), BF16),
        jax.ShapeDtypeStruct((m, D_ATTN), BF16),
        jax.ShapeDtypeStruct((m // tm, tm // MOBA_BLOCK, D_ATTN), F32),
    )
    out_specs = (
        pl.BlockSpec((tm, D_ATTN), row),
        pl.BlockSpec((tm, D_ATTN), row),
        pl.BlockSpec((tm, D_ATTN), row),
        pl.BlockSpec((tm, D_SSM), u_map),
        pl.BlockSpec((tm, D_MEM), row),
        pl.BlockSpec((tm, D_MIX), row),
        pl.BlockSpec((tm, D_ATTN), row),
        pl.BlockSpec((tm, D_ATTN), row),
        pl.BlockSpec((1, tm // MOBA_BLOCK, D_ATTN), lambda i: (i, 0, 0)),
    )
    return pl.pallas_call(
        functools.partial(_proj_kernel, with_prompt_extras=True),
        grid=(m // tm,),
        in_specs=[pl.BlockSpec((tm, D_MODEL), row),
                  pl.BlockSpec((D_MODEL, D_IN), lambda i: (0, 0), pipeline_mode=pl.Buffered(1))],
        out_specs=out_specs,
        out_shape=out_shape,
        compiler_params=_vmem_limit(52 * 2**20),
        name="in_proj_prompt",
    )(x2d, w_bf)


def _in_proj_sample(x2d, w_bf, tm):
    m = x2d.shape[0]
    row = lambda i: (i, 0)
    out_shape = (
        jax.ShapeDtypeStruct((m, D_ATTN), F32),
        jax.ShapeDtypeStruct((m, D_ATTN), F32),
        jax.ShapeDtypeStruct((m, D_ATTN), F32),
        jax.ShapeDtypeStruct((m, D_SSM), F32),
        jax.ShapeDtypeStruct((m, D_MEM), BF16),
        jax.ShapeDtypeStruct((m, D_MIX), F32),
    )
    widths = (D_ATTN, D_ATTN, D_ATTN, D_SSM, D_MEM, D_MIX)
    return pl.pallas_call(
        functools.partial(_proj_kernel, with_prompt_extras=False),
        grid=(m // tm,),
        in_specs=[pl.BlockSpec((tm, D_MODEL), row),
                  pl.BlockSpec((D_MODEL, D_IN), lambda i: (0, 0), pipeline_mode=pl.Buffered(1))],
        out_specs=tuple(pl.BlockSpec((tm, w), row) for w in widths),
        out_shape=out_shape,
        compiler_params=_vmem_limit(52 * 2**20),
        name="in_proj_sample",
    )(x2d, w_bf)


def _mem_kv_kernel(mem_ref, w_ref, mk_ref, mv_ref):
    kv = jnp.dot(mem_ref[...].astype(BF16), w_ref[...], preferred_element_type=F32)
    mk_ref[...] = kv[:, :D_MEM]
    mv_ref[...] = kv[:, D_MEM:]


def _mem_kv(mem2d, w_bf):
    m = mem2d.shape[0]
    tm = min(m, 512)
    row = lambda i: (i, 0)
    return pl.pallas_call(
        _mem_kv_kernel,
        grid=(m // tm,),
        in_specs=[pl.BlockSpec((tm, D_MODEL), row),
                  pl.BlockSpec((D_MODEL, 2 * D_MEM), lambda i: (0, 0))],
        out_specs=(pl.BlockSpec((tm, D_MEM), row), pl.BlockSpec((tm, D_MEM), row)),
        out_shape=(jax.ShapeDtypeStruct((m, D_MEM), F32), jax.ShapeDtypeStruct((m, D_MEM), F32)),
        name="mem_kv",
    )(mem2d, w_bf)


def _select_topk(scores, valid, index, axis):
    lowest = jnp.finfo(F32).min
    s = jnp.where(valid, scores, lowest)
    sel = jnp.zeros(scores.shape, dtype=jnp.bool_)
    for _ in range(MOBA_TOPK):
        m = jnp.max(s, axis=axis, keepdims=True)
        first = jnp.min(jnp.where(s == m, index, jnp.float32(2**30)), axis=axis, keepdims=True)
        pick = index == first
        sel = sel | (pick & valid)
        s = jnp.where(pick, lowest, s)
    return sel


_MOBA_Q_BLOCKS = 2
_MOBA_ROW_GROUP = 64


def _moba_prompt_kernel(q_ref, k_ref, v_ref, kmean_ref, o_ref,
                        kaug_ref, vaug_ref, lhs_ref, s_ref, p_ref, alpha_ref, m_ref, acc_ref):
    ti = pl.program_id(2)
    blk = MOBA_BLOCK
    tq = q_ref.shape[1]
    qb = tq // blk
    nb = k_ref.shape[1] // blk
    half = LANES // 2
    lane_b = lax.broadcasted_iota(jnp.int32, (blk, LANES), 1)
    head0_b = lane_b < half

    @pl.when(ti == 0)
    def _build_operands():
        def body(j, carry):
            rows = pl.ds(pl.multiple_of(j * blk, blk), blk)
            kb = k_ref[0, rows, :].astype(F32)
            vb = v_ref[0, rows, :].astype(F32)
            hot0 = jnp.where(lane_b == half + j, 1.0, 0.0)
            hot1 = jnp.where(lane_b == j, 1.0, 0.0)
            kaug_ref[0, rows, :] = jnp.where(head0_b, kb, hot0).astype(BF16)
            kaug_ref[1, rows, :] = jnp.where(head0_b, hot1, kb).astype(BF16)
            vaug_ref[0, rows, :] = jnp.where(head0_b, vb, 1.0).astype(BF16)
            vaug_ref[1, rows, :] = jnp.where(head0_b, 1.0, vb).astype(BF16)
            return carry
        lax.fori_loop(0, nb, body, 0)

    q2 = q_ref[0]
    means = kmean_ref[0]
    lane_m = lax.broadcasted_iota(jnp.int32, (nb, LANES), 1)
    pad = [jnp.zeros((half - nb, LANES), F32)] if nb < half else []
    means_p = jnp.concatenate([jnp.where(lane_m >= half, means, 0.0)] + pad
                              + [jnp.where(lane_m < half, means, 0.0)] + pad, axis=0)
    sc_t = lax.dot_general(means_p, q2, _NT, precision=lax.Precision.HIGHEST,
                           preferred_element_type=F32)

    slot = lax.broadcasted_iota(jnp.int32, (nb, tq), 0)
    slot_f = slot.astype(F32)
    n_full = ti * qb + lax.broadcasted_iota(jnp.int32, (nb, tq), 1) // blk
    bias_t = []
    for off in (0, half):
        sel = _select_topk(sc_t[off:off + nb], slot < n_full, slot_f, axis=0)
        bias_t.append(jnp.where(sel | (slot == n_full), 0.0, NEG_INF))
        if nb < half:
            bias_t.append(jnp.zeros((half - nb, tq), F32))
    bias = jnp.concatenate(bias_t, axis=0).T
    head0_q = lax.broadcasted_iota(jnp.int32, (tq, LANES), 1) < half
    qs = q2 * (HEAD_DIM ** -0.5)
    lhs_ref[0] = jnp.where(head0_q, qs, bias).astype(BF16)
    lhs_ref[1] = jnp.where(head0_q, bias, qs).astype(BF16)
    m_ref[...] = jnp.full(m_ref.shape, NEG_INF, F32)
    acc_ref[...] = jnp.zeros_like(acc_ref)

    rg = _MOBA_ROW_GROUP
    c_i = lax.broadcasted_iota(jnp.int32, (rg, blk), 1)
    r_i = lax.broadcasted_iota(jnp.int32, (rg, blk), 0)

    def key_rows(j):
        return pl.ds(j * blk if isinstance(j, int) else pl.multiple_of(j * blk, blk), blk)

    def score_block(j, slot):
        for e in (0, 1):
            s_ref[slot, e] = lax.dot_general(lhs_ref[e], kaug_ref[e, key_rows(j), :], _NT,
                                             preferred_element_type=F32)

    def softmax_pv_block(j, slot, causal):
        for e in (0, 1):
            for g in range(tq // rg):
                rows = pl.ds(g * rg, rg)
                s = s_ref[slot, e, rows, :]
                if causal:
                    s = jnp.where(j * blk + c_i <= ti * tq + g * rg + r_i, s, NEG_INF)
                m_old = m_ref[e, rows, :]
                m_new = jnp.maximum(m_old, jnp.max(s, axis=1, keepdims=True))
                alpha_ref[e, rows, :] = jnp.exp(m_old - m_new)
                p_ref[e, rows, :] = jnp.exp(s - jnp.concatenate([m_new] * (blk // LANES), axis=1)).astype(BF16)
                m_ref[e, rows, :] = m_new
            pv = jnp.dot(p_ref[e], vaug_ref[e, key_rows(j), :], preferred_element_type=F32)
            acc_ref[e] = alpha_ref[e] * acc_ref[e] + pv

    n_past = ti * qb
    score_block(0, 0)

    def past_pair(jj, carry):
        j = 2 * jj
        score_block(j + 1, 1)
        softmax_pv_block(j, 0, causal=False)
        score_block(j + 2, 0)
        softmax_pv_block(j + 1, 1, causal=False)
        return carry

    lax.fori_loop(0, n_past // 2, past_pair, 0)
    for d in range(qb):
        if d + 1 < qb:
            score_block(n_past + d + 1, (d + 1) % 2)
        softmax_pv_block(n_past + d, d % 2, causal=True)

    a0 = acc_ref[0]
    a1 = acc_ref[1]
    o0 = a0 / pltpu.roll(a0, half, axis=1)
    o1 = a1 / pltpu.roll(a1, half, axis=1)
    o_ref[0] = jnp.where(head0_q, o0, o1)


def _moba_prompt(q, kb, vb, kmean):
    b, s, _ = q.shape
    nb = s // MOBA_BLOCK
    tq = _MOBA_Q_BLOCKS * MOBA_BLOCK
    assert nb <= LANES // 2 and nb % SUBLANES == 0 and s % tq == 0 and _MOBA_Q_BLOCKS % 2 == 0
    tile = lambda bi, hp, ti: (bi, ti, hp)
    whole = lambda bi, hp, ti: (bi, 0, hp)
    return pl.pallas_call(
        _moba_prompt_kernel,
        grid=(b, D_ATTN // LANES, s // tq),
        in_specs=[pl.BlockSpec((1, tq, LANES), tile),
                  pl.BlockSpec((1, s, LANES), whole),
                  pl.BlockSpec((1, s, LANES), whole),
                  pl.BlockSpec((1, nb, LANES), whole)],
        out_specs=pl.BlockSpec((1, tq, LANES), tile),
        out_shape=jax.ShapeDtypeStruct((b, s, D_ATTN), F32),
        scratch_shapes=[pltpu.VMEM((2, s, LANES), BF16),
                        pltpu.VMEM((2, s, LANES), BF16),
                        pltpu.VMEM((2, tq, LANES), BF16),
                        pltpu.VMEM((2, 2, tq, MOBA_BLOCK), F32),
                        pltpu.VMEM((2, tq, MOBA_BLOCK), BF16),
                        pltpu.VMEM((2, tq, LANES), F32),
                        pltpu.VMEM((2, tq, LANES), F32),
                        pltpu.VMEM((2, tq, LANES), F32)],
        compiler_params=pltpu.CompilerParams(
            dimension_semantics=("arbitrary", "arbitrary", "arbitrary"),
            vmem_limit_bytes=40 * 2**20),
        name="moba_prompt",
    )(q, kb, vb, kmean)


def _ssm_disc_kernel(a_re_ref, a_im_ref, log_dt_ref, bt_re_ref, bt_im_ref,
                     abar_re_ref, abar_im_ref, bbt_re_ref, bbt_im_ref):
    a_re = a_re_ref[...]
    a_im = a_im_ref[...]
    dt = jnp.exp(log_dt_ref[...])
    mag = jnp.exp(a_re * dt)
    ang = a_im * dt
    abar_re = mag * jnp.cos(ang)
    abar_im = mag * jnp.sin(ang)
    den = a_re * a_re + a_im * a_im
    f_re = ((abar_re - 1.0) * a_re + abar_im * a_im) / den
    f_im = (abar_im * a_re - (abar_re - 1.0) * a_im) / den
    abar_re_ref[...] = abar_re
    abar_im_ref[...] = abar_im
    bt_re = bt_re_ref[...]
    bt_im = bt_im_ref[...]
    bbt_re_ref[...] = f_re[:, None, :] * bt_re - f_im[:, None, :] * bt_im
    bbt_im_ref[...] = f_re[:, None, :] * bt_im + f_im[:, None, :] * bt_re


def _ssm_operands(a_re, a_im, log_dt, b_re, b_im, c_re, c_im):
    g, p, c = b_re.shape
    abar_re, abar_im, bbt_re, bbt_im = pl.pallas_call(
        _ssm_disc_kernel,
        out_shape=(jax.ShapeDtypeStruct((g, p), F32), jax.ShapeDtypeStruct((g, p), F32),
                   jax.ShapeDtypeStruct((g, c, p), F32), jax.ShapeDtypeStruct((g, c, p), F32)),
        name="ssm_discretize",
    )(a_re.astype(F32), a_im.astype(F32), log_dt.astype(F32).reshape(g, 1),
      b_re.astype(F32).transpose(0, 2, 1), b_im.astype(F32).transpose(0, 2, 1))
    eye = jnp.eye(g, dtype=F32)

    def in_block_diag(bbt):
        return jnp.einsum('gcp,gh->gchp', bbt, eye).reshape(g * c, g * p)

    def out_block_diag(cm):
        return jnp.einsum('gcp,gh->gphc', cm, eye).reshape(g * p, g * c)

    b_mat = jnp.concatenate([in_block_diag(bbt_re), in_block_diag(bbt_im)], axis=1)
    c_mat = jnp.concatenate([out_block_diag(c_re.astype(F32)),
                             -out_block_diag(c_im.astype(F32))], axis=0)
    a_row = jnp.concatenate([abar_re.reshape(1, g * p), abar_im.reshape(1, g * p)], axis=1)
    return a_row, b_mat, c_mat


def _complex_step(a_re, a_im, h_re, h_im, bu_re, bu_im):
    return (a_re * h_re - a_im * h_im + bu_re, a_re * h_im + a_im * h_re + bu_im)


def _ssm_output(h_all, u, c_ref, d_ref, wg_ref, bg_ref):
    y = jnp.dot(h_all.astype(BF16), c_ref[...], preferred_element_type=F32) + d_ref[...] * u
    y = jax.nn.gelu(y)
    z = jnp.dot(y.astype(BF16), wg_ref[...], preferred_element_type=F32) + bg_ref[...]
    return z[:, :D_SSM] * jax.nn.sigmoid(z[:, D_SSM:])


def _ssm_prompt_kernel(u_ref, a_ref, b_ref, c_ref, d_ref, wg_ref, bg_ref, y_ref, hfin_ref,
                       hb_ref, h_ref, init_ref, *, n_seg):
    pss = pl.program_id(0)
    c = pl.program_id(1)
    n_chunks = pl.num_programs(1)
    rows = u_ref.shape[0]
    steps = rows // SSM_CHAINS
    n = SSM_LANES
    a_re = jnp.broadcast_to(a_ref[:, :n], (SSM_CHAINS, n))
    a_im = jnp.broadcast_to(a_ref[:, n:], (SSM_CHAINS, n))

    @pl.when((pss == 0) & (c == 0))
    def _zero_state():
        h_ref[...] = jnp.zeros_like(h_ref)

    @pl.when((pss == 1) & (c == 0))
    def _true_initial_state():
        h_ref[...] = init_ref[...]

    u = u_ref[...]
    hb_ref[...] = jnp.dot(u.astype(BF16), b_ref[...], preferred_element_type=F32)

    def scan(store):
        def step(s, carry):
            h_re, h_im = carry
            r = pl.ds(pl.multiple_of(s * SSM_CHAINS, SSM_CHAINS), SSM_CHAINS)
            h_re, h_im = _complex_step(a_re, a_im, h_re, h_im, hb_ref[r, :n], hb_ref[r, n:])
            if store:
                hb_ref[r, :n] = h_re
                hb_ref[r, n:] = h_im
            return h_re, h_im
        h_re, h_im = lax.fori_loop(0, steps, step, (h_ref[:, :n], h_ref[:, n:]))
        h_ref[:, :n] = h_re
        h_ref[:, n:] = h_im

    @pl.when(pss == 0)
    def _local_pass():
        scan(False)

    @pl.when((pss == 0) & (c == n_chunks - 1))
    def _segment_initial_states():
        seg_len = steps * n_chunks
        p_re, p_im = a_re, a_im
        for _ in range(int(math.log2(seg_len))):
            p_re, p_im = p_re * p_re - p_im * p_im, 2.0 * p_re * p_im
        e_re = h_ref[:, :n]
        e_im = h_ref[:, n:]
        first = (lax.broadcasted_iota(jnp.int32, (SSM_CHAINS, n), 0) % n_seg) == 0
        i_re = jnp.zeros((SSM_CHAINS, n), F32)
        i_im = jnp.zeros((SSM_CHAINS, n), F32)
        for _ in range(n_seg - 1):
            t_re = p_re * i_re - p_im * i_im + e_re
            t_im = p_re * i_im + p_im * i_re + e_im
            i_re = jnp.where(first, 0.0, pltpu.roll(t_re, 1, axis=0))
            i_im = jnp.where(first, 0.0, pltpu.roll(t_im, 1, axis=0))
        init_ref[:, :n] = i_re
        init_ref[:, n:] = i_im

    @pl.when(pss == 1)
    def _output_pass():
        scan(True)
        y_ref[...] = _ssm_output(hb_ref[...], u, c_ref, d_ref, wg_ref, bg_ref).astype(y_ref.dtype)

    @pl.when((pss == 1) & (c == n_chunks - 1))
    def _final_state():
        hfin_ref[...] = h_ref[...]


def _ssm_prompt(u_chain, a_row, b_bf, c_bf, d_row, wg_bf, bg_row, n_seg, steps_per_chunk):
    total = u_chain.shape[0]
    rows = steps_per_chunk * SSM_CHAINS
    n_chunks = total // rows
    seg_len = total // SSM_CHAINS
    assert 2 ** int(math.log2(seg_len)) == seg_len
    const = lambda p, c: (0, 0)
    return pl.pallas_call(
        functools.partial(_ssm_prompt_kernel, n_seg=n_seg),
        grid=(2, n_chunks),
        in_specs=[pl.BlockSpec((rows, D_SSM), lambda p, c: (c, 0)),
                  pl.BlockSpec((1, 2 * SSM_LANES), const),
                  pl.BlockSpec((D_SSM, 2 * SSM_LANES), const),
                  pl.BlockSpec((2 * SSM_LANES, D_SSM), const),
                  pl.BlockSpec((1, D_SSM), const),
                  pl.BlockSpec((D_SSM, 2 * D_SSM), const),
                  pl.BlockSpec((1, 2 * D_SSM), const)],
        out_specs=(pl.BlockSpec((rows, D_SSM), lambda p, c: (p * c, 0)),
                   pl.BlockSpec((SSM_CHAINS, 2 * SSM_LANES), const)),
        out_shape=(jax.ShapeDtypeStruct((total, D_SSM), F32),
                   jax.ShapeDtypeStruct((SSM_CHAINS, 2 * SSM_LANES), F32)),
        scratch_shapes=[pltpu.VMEM((rows, 2 * SSM_LANES), F32),
                        pltpu.VMEM((SSM_CHAINS, 2 * SSM_LANES), F32),
                        pltpu.VMEM((SSM_CHAINS, 2 * SSM_LANES), F32)],
        compiler_params=pltpu.CompilerParams(dimension_semantics=("arbitrary", "arbitrary"),
                                             vmem_limit_bytes=40 * 2**20),
        name="ssm_prompt",
    )(u_chain, a_row, b_bf, c_bf, d_row, wg_bf, bg_row)


def _ssm_sample_kernel(u_ref, h0_ref, a_ref, b_ref, c_ref, d_ref, wg_ref, bg_ref, y_ref, hfin_ref):
    steps, batch, _ = u_ref.shape
    n = SSM_LANES
    a_re = jnp.broadcast_to(a_ref[:, :n], (batch, n))
    a_im = jnp.broadcast_to(a_ref[:, n:], (batch, n))
    h_re = h0_ref[:, :n]
    h_im = h0_ref[:, n:]
    for t in range(steps):
        u = u_ref[t]
        bu = jnp.dot(u, b_ref[...], precision=lax.Precision.HIGHEST, preferred_element_type=F32)
        h_re, h_im = _complex_step(a_re, a_im, h_re, h_im, bu[:, :n], bu[:, n:])
        h_all = jnp.concatenate([h_re, h_im], axis=1)
        y_ref[t] = _ssm_output(h_all, u, c_ref, d_ref, wg_ref, bg_ref)
    hfin_ref[:, :n] = h_re
    hfin_ref[:, n:] = h_im


def _ssm_sample(u_tb, h0, a_row, b_f32, c_bf, d_row, wg_bf, bg_row):
    steps, batch, _ = u_tb.shape
    return pl.pallas_call(
        _ssm_sample_kernel,
        out_shape=(jax.ShapeDtypeStruct((steps, batch, D_SSM), F32),
                   jax.ShapeDtypeStruct((batch, 2 * SSM_LANES), F32)),
        compiler_params=_vmem_limit(40 * 2**20),
        name="ssm_sample",
    )(u_tb, h0, a_row, b_f32, c_bf, d_row, wg_bf, bg_row)


def _mem_attn_prompt_kernel(q_ref, mk_ref, mv_ref, o_ref):
    rows = q_ref.shape[0]
    half = LANES // 2
    scale = (D_MEM // MEM_HEADS) ** -0.5
    head0 = lax.broadcasted_iota(jnp.int32, (rows, LANES), 1) < half
    outs = []
    for hp in range(D_MEM // LANES):
        cols = slice(hp * LANES, (hp + 1) * LANES)
        q2 = q_ref[:, cols].astype(F32)
        k2 = mk_ref[:, cols].astype(BF16)
        v2 = mv_ref[:, cols].astype(BF16)
        res = []
        for e in (0, 1):
            mine = head0 if e == 0 else jnp.logical_not(head0)
            lhs = jnp.where(mine, q2, 0.0).astype(BF16)
            s = lax.dot_general(lhs, k2, _NT, preferred_element_type=F32) * scale
            p = jnp.exp(s - jnp.max(s, axis=1, keepdims=True))
            l = jnp.sum(p, axis=1, keepdims=True)
            res.append(jnp.dot(p.astype(BF16), v2, preferred_element_type=F32) / l)
        outs.append(jnp.where(head0, res[0], res[1]))
    o_ref[...] = jnp.concatenate(outs, axis=1)


def _mem_attn_prompt(qm, mk, mv, batch, seq, tm):
    n_mem = mk.shape[0] // batch
    tiles = seq // tm
    return pl.pallas_call(
        _mem_attn_prompt_kernel,
        grid=(batch, tiles),
        in_specs=[pl.BlockSpec((tm, D_MEM), lambda b, i: (b * tiles + i, 0)),
                  pl.BlockSpec((n_mem, D_MEM), lambda b, i: (b, 0)),
                  pl.BlockSpec((n_mem, D_MEM), lambda b, i: (b, 0))],
        out_specs=pl.BlockSpec((tm, D_MEM), lambda b, i: (b * tiles + i, 0)),
        out_shape=jax.ShapeDtypeStruct((batch * seq, D_MEM), F32),
        name="mem_attn_prompt",
    )(qm, mk, mv)


def _head_rows(x, n_heads, head_dim):
    head_of_lane = lax.broadcasted_iota(jnp.int32, x.shape, 1) // head_dim
    return jnp.concatenate([jnp.where(head_of_lane == h, x, 0.0) for h in range(n_heads)], axis=0)


def _head_diagonal(o_full, t, n_heads, head_dim):
    head_of_lane = lax.broadcasted_iota(jnp.int32, (t, o_full.shape[1]), 1) // head_dim
    out = jnp.zeros((t, o_full.shape[1]), F32)
    for h in range(n_heads):
        out = out + jnp.where(head_of_lane == h, o_full[h * t:(h + 1) * t], 0.0)
    return out


def _mem_attn_sample_kernel(q_ref, mkt_ref, mvt_ref, o_ref):
    group, t, _ = q_ref.shape
    head_dim = D_MEM // MEM_HEADS
    scale = head_dim ** -0.5
    for g in range(group):
        qbd = _head_rows(q_ref[g].astype(F32) * scale, MEM_HEADS, head_dim).astype(BF16)
        s = jnp.dot(qbd, mkt_ref[g].astype(BF16), preferred_element_type=F32)
        p = jnp.exp(s - jnp.max(s, axis=1, keepdims=True))
        l = jnp.sum(p, axis=1, keepdims=True)
        o_full = lax.dot_general(p.astype(BF16), mvt_ref[g].astype(BF16), _NT, preferred_element_type=F32) / l
        o_ref[g] = _head_diagonal(o_full, t, MEM_HEADS, head_dim)


def _mem_attn_sample(qm, mkt, mvt, group):
    db, t, _ = qm.shape
    n_mem = mkt.shape[2]
    blk = lambda i: (i, 0, 0)
    return pl.pallas_call(
        _mem_attn_sample_kernel,
        grid=(db // group,),
        in_specs=[pl.BlockSpec((group, t, D_MEM), blk),
                  pl.BlockSpec((group, D_MEM, n_mem), blk),
                  pl.BlockSpec((group, D_MEM, n_mem), blk)],
        out_specs=pl.BlockSpec((group, t, D_MEM), blk),
        out_shape=jax.ShapeDtypeStruct((db, t, D_MEM), F32),
        name="mem_attn_sample",
    )(qm, mkt, mvt)


def _merge_kernel(x_ref, attn_ref, ssm_ref, mem_ref, gate_ref, w_ref, g_ref, b_ref, y_ref):
    mixed = jnp.concatenate([attn_ref[...], ssm_ref[...], mem_ref[...]], axis=1) * jax.nn.silu(gate_ref[...])
    out = jnp.dot(mixed.astype(BF16), w_ref[...], preferred_element_type=F32)
    z = DEEPNORM_ALPHA * x_ref[...] + out
    mu = jnp.mean(z, axis=1, keepdims=True)
    zc = z - mu
    var = jnp.mean(zc * zc, axis=1, keepdims=True)
    y_ref[...] = zc * lax.rsqrt(var + LN_EPS) * g_ref[...] + b_ref[...]


def _merge(x2d, attn, ssm, mem, gate, w_bf, g_row, b_row, tm, ssm_map):
    m = x2d.shape[0]
    row = lambda i: (i, 0)
    const = lambda i: (0, 0)
    return pl.pallas_call(
        _merge_kernel,
        grid=(m // tm,),
        in_specs=[pl.BlockSpec((tm, D_MODEL), row),
                  pl.BlockSpec((tm, D_ATTN), row),
                  pl.BlockSpec((tm, D_SSM), ssm_map),
                  pl.BlockSpec((tm, D_MEM), row),
                  pl.BlockSpec((tm, D_MIX), row),
                  pl.BlockSpec((D_MIX, D_MODEL), const),
                  pl.BlockSpec((1, D_MODEL), const),
                  pl.BlockSpec((1, D_MODEL), const)],
        out_specs=pl.BlockSpec((tm, D_MODEL), row),
        out_shape=jax.ShapeDtypeStruct((m, D_MODEL), F32),
        compiler_params=_vmem_limit(48 * 2**20),
        name="merge",
    )(x2d, attn, ssm, mem, gate, w_bf, g_row, b_row)


def _moba_sample_kernel(pt_ref, q_ref, kn_ref, vn_ref, *refs, pages_per_step, chunks, page_size):
    del pt_ref
    pps = pages_per_step
    k_refs = refs[:pps]
    v_refs = refs[pps:2 * pps]
    o_ref = refs[2 * pps]
    s_ref, p_ref, acc_ref, l_ref, pown_ref = refs[2 * pps + 1:]
    r = pl.program_id(0) % (2 * chunks)
    t = q_ref.shape[1]
    rows = N_HEADS_A * t
    n_pages = pps * chunks
    pages_per_block = MOBA_BLOCK // page_size
    n_blocks = n_pages // pages_per_block
    qbd_bf = _head_rows(q_ref[0] * (HEAD_DIM ** -0.5), N_HEADS_A, HEAD_DIM).astype(BF16)
    lane = lax.broadcasted_iota(jnp.int32, (rows, page_size), 1)

    def pad_rows(x):
        return jnp.concatenate([x, jnp.zeros((page_size - x.shape[0], x.shape[1]), x.dtype)], axis=0)

    @pl.when(r < chunks)
    def _key_pages():
        for i in range(pps):
            s_ref[r * pps + i] = jnp.dot(qbd_bf, k_refs[i][0].astype(BF16), preferred_element_type=F32)

    @pl.when(r == chunks)
    def _select_and_softmax():
        sc = jnp.zeros((rows, page_size), F32)
        for j in range(n_blocks):
            bs = s_ref[j * pages_per_block]
            for pg in range(1, pages_per_block):
                bs = bs + s_ref[j * pages_per_block + pg]
            sc = jnp.where(lane == j, jnp.sum(bs, axis=1, keepdims=True), sc)
        sel = _select_topk(sc, lane < n_blocks, lane.astype(F32), axis=1)
        lo = lax.dot_general(qbd_bf, pad_rows(kn_ref[0]).astype(BF16), _NT, preferred_element_type=F32)
        t_row = lax.broadcasted_iota(jnp.int32, (rows, page_size), 0) % t
        lo = jnp.where(lane <= t_row, lo, NEG_INF)
        mx = lo
        for j in range(n_blocks):
            bm = s_ref[j * pages_per_block]
            for pg in range(1, pages_per_block):
                bm = jnp.maximum(bm, s_ref[j * pages_per_block + pg])
            mx = jnp.maximum(mx, jnp.where(sel[:, j:j + 1], bm, NEG_INF))
        m = jnp.max(mx, axis=1, keepdims=True)
        po = jnp.exp(lo - m)
        pown_ref[...] = po.astype(BF16)
        lsum = po
        for j in range(n_blocks):
            sj = sel[:, j:j + 1]
            for pg in range(j * pages_per_block, (j + 1) * pages_per_block):
                p = jnp.where(sj, jnp.exp(s_ref[pg] - m), 0.0)
                lsum = lsum + p
                p_ref[pg] = p.astype(BF16)
        l_ref[...] = jnp.broadcast_to(jnp.sum(lsum, axis=1, keepdims=True), l_ref.shape)
        acc_ref[...] = jnp.zeros_like(acc_ref)

    @pl.when(r >= chunks)
    def _value_pages():
        acc = acc_ref[...]
        for i in range(pps):
            acc = acc + lax.dot_general(p_ref[(r - chunks) * pps + i], v_refs[i][0].astype(BF16), _NT,
                                        preferred_element_type=F32)
        acc_ref[...] = acc

    @pl.when(r == 2 * chunks - 1)
    def _finish():
        acc = acc_ref[...] + jnp.dot(pown_ref[...], pad_rows(vn_ref[0]).astype(BF16),
                                     preferred_element_type=F32)
        l = l_ref[...]
        o_full = acc / jnp.concatenate([l] * (D_ATTN // page_size), axis=1)
        o_ref[0] = _head_diagonal(o_full, t, N_HEADS_A, HEAD_DIM)


def _moba_sample(q, k_new, v_new, cache_k, cache_v, page_table, pages_per_step):
    db, t, _ = q.shape
    n_pages = page_table.shape[1]
    page_size = cache_k.shape[2]
    chunks = n_pages // pages_per_step
    steps = 2 * chunks
    assert (n_pages * page_size) % MOBA_BLOCK == 0 and MOBA_BLOCK % page_size == 0
    assert n_pages * page_size // MOBA_BLOCK >= MOBA_TOPK and t <= page_size and page_size == LANES
    rows = N_HEADS_A * t

    def seq_map(n, pt):
        return (n // steps, 0, 0)

    def key_map(i):
        def index(n, pt):
            b = n // steps
            c = jnp.minimum(n % steps, chunks - 1)
            return (pt[b, c * pages_per_step + i], 0, 0)
        return index

    def value_map(i):
        def index(n, pt):
            b = n // steps
            r = n % steps
            in_phase = r >= chunks
            vb = jnp.where(in_phase, b, jnp.maximum(b - 1, 0))
            vc = jnp.where(in_phase, r - chunks, jnp.where(b > 0, chunks - 1, 0))
            return (pt[vb, vc * pages_per_step + i], 0, 0)
        return index

    page_block = (1, D_ATTN, page_size)
    in_specs = ([pl.BlockSpec((1, t, D_ATTN), seq_map)] * 3
                + [pl.BlockSpec(page_block, key_map(i)) for i in range(pages_per_step)]
                + [pl.BlockSpec(page_block, value_map(i)) for i in range(pages_per_step)])
    grid_spec = pltpu.PrefetchScalarGridSpec(
        num_scalar_prefetch=1,
        grid=(db * steps,),
        in_specs=in_specs,
        out_specs=pl.BlockSpec((1, t, D_ATTN), seq_map),
        scratch_shapes=[pltpu.VMEM((n_pages, rows, page_size), F32),
                        pltpu.VMEM((n_pages, rows, page_size), BF16),
                        pltpu.VMEM((rows, D_ATTN), F32),
                        pltpu.VMEM((rows, page_size), F32),
                        pltpu.VMEM((rows, page_size), BF16)])
    return pl.pallas_call(
        functools.partial(_moba_sample_kernel, pages_per_step=pages_per_step, chunks=chunks,
                          page_size=page_size),
        grid_spec=grid_spec,
        out_shape=jax.ShapeDtypeStruct((db, t, D_ATTN), F32),
        compiler_params=pltpu.CompilerParams(dimension_semantics=("arbitrary",),
                                             vmem_limit_bytes=44 * 2**20),
        name="moba_sample",
    )(page_table, q, k_new, v_new, *([cache_k] * pages_per_step), *([cache_v] * pages_per_step))


def kernel(x_prompt, x_sample, mem_prompt, cache_k, cache_v, state_ssm_re, state_ssm_im,
           cache_mem_k, cache_mem_v, page_table, w_in, w_mem_kv, a_re, a_im, log_dt,
           b_re, b_im, c_re, c_im, d_skip, w_glu, b_glu, w_out, ln_g, ln_b):
    batch, seq, _ = x_prompt.shape
    db, t_new, _ = x_sample.shape
    n_mem = mem_prompt.shape[1]
    n_phys, page_size = cache_k.shape[:2]
    n_seg = SSM_CHAINS // batch
    seg_len = seq // n_seg
    tm = min(512, seg_len)

    w_in_bf = w_in.astype(BF16)
    w_mem_bf = w_mem_kv.astype(BF16)
    w_out_bf = w_out.astype(BF16)
    wg_bf = w_glu.astype(BF16)
    bg_row = b_glu.astype(F32).reshape(1, 2 * D_SSM)
    d_row = d_skip.astype(F32).reshape(1, D_SSM)
    g_row = ln_g.astype(F32).reshape(1, D_MODEL)
    b_row = ln_b.astype(F32).reshape(1, D_MODEL)
    a_row, b_mat, c_mat = _ssm_operands(a_re, a_im, log_dt, b_re, b_im, c_re, c_im)
    b_bf = b_mat.astype(BF16)
    c_bf = c_mat.astype(BF16)

    x2d = x_prompt.reshape(batch * seq, D_MODEL)
    q, k, v, u_chain, qm, gate, kb, vb, kmean = _in_proj_prompt(x2d, w_in_bf, batch, seq, tm)
    attn = _moba_prompt(q.reshape(batch, seq, D_ATTN), kb.reshape(batch, seq, D_ATTN),
                        vb.reshape(batch, seq, D_ATTN), kmean.reshape(batch, seq // MOBA_BLOCK, D_ATTN))
    y_chain, hfin = _ssm_prompt(u_chain.reshape(seg_len * SSM_CHAINS, D_SSM), a_row, b_bf, c_bf, d_row,
                                wg_bf, bg_row, n_seg, steps_per_chunk=min(64, seg_len))
    mk, mv = _mem_kv(mem_prompt.reshape(batch * n_mem, D_MODEL), w_mem_bf)
    mem_o = _mem_attn_prompt(qm, mk, mv, batch, seq, tm)
    tiles_per_seq = seq // tm
    tiles_per_seg = seg_len // tm

    def chain_map(i):
        b = i // tiles_per_seq
        ti = i % tiles_per_seq
        return (ti % tiles_per_seg, b * n_seg + ti // tiles_per_seg)

    y_prompt = _merge(x2d, attn.reshape(batch * seq, D_ATTN), y_chain.reshape(seg_len, SSM_CHAINS * D_SSM),
                      mem_o, gate, w_out_bf, g_row, b_row, tm, chain_map)
    h_last = hfin[n_seg - 1::n_seg]
    ssm_re_p = h_last[:, :SSM_LANES].reshape(batch, SSM_GROUPS, SSM_STATE)
    ssm_im_p = h_last[:, SSM_LANES:].reshape(batch, SSM_GROUPS, SSM_STATE)

    xs2d = x_sample.reshape(db * t_new, D_MODEL)
    tms = min(512, db * t_new)
    q_s, k_s, v_s, u_s, qm_s, gate_s = _in_proj_sample(xs2d, w_in_bf, tms)
    attn_s = _moba_sample(q_s.reshape(db, t_new, D_ATTN), k_s.reshape(db, t_new, D_ATTN),
                          v_s.reshape(db, t_new, D_ATTN),
                          cache_k.transpose(0, 2, 3, 1).reshape(n_phys, D_ATTN, page_size),
                          cache_v.transpose(0, 2, 3, 1).reshape(n_phys, D_ATTN, page_size), page_table,
                          pages_per_step=min(16, page_table.shape[1]))
    h0 = jnp.concatenate([state_ssm_re.astype(F32).reshape(db, SSM_LANES),
                          state_ssm_im.astype(F32).reshape(db, SSM_LANES)], axis=1)
    y_tb, hfin_s = _ssm_sample(u_s.reshape(db, t_new, D_SSM).transpose(1, 0, 2), h0, a_row, b_mat, c_bf,
                               d_row, wg_bf, bg_row)
    ssm_s = y_tb.transpose(1, 0, 2).reshape(db * t_new, D_SSM)
    mem_s = _mem_attn_sample(qm_s.reshape(db, t_new, D_MEM),
                             cache_mem_k.transpose(0, 2, 3, 1).reshape(db, D_MEM, n_mem),
                             cache_mem_v.transpose(0, 2, 3, 1).reshape(db, D_MEM, n_mem), group=min(8, db))
    y_sample = _merge(xs2d, attn_s.reshape(db * t_new, D_ATTN), ssm_s, mem_s.reshape(db * t_new, D_MEM),
                      gate_s, w_out_bf, g_row, b_row, tms, lambda i: (i, 0))

    heads = (N_HEADS_A, HEAD_DIM)
    mem_heads = (MEM_HEADS, D_MEM // MEM_HEADS)
    return (y_prompt.reshape(batch, seq, D_MODEL), y_sample.reshape(db, t_new, D_MODEL),
            k.reshape(batch, seq, *heads), v.reshape(batch, seq, *heads), ssm_re_p, ssm_im_p,
            mk.reshape(batch, n_mem, *mem_heads), mv.reshape(batch, n_mem, *mem_heads),
            k_s.reshape(db, t_new, *heads), v_s.reshape(db, t_new, *heads),
            hfin_s[:, :SSM_LANES].reshape(db, SSM_GROUPS, SSM_STATE),
            hfin_s[:, SSM_LANES:].reshape(db, SSM_GROUPS, SSM_STATE))
```

```python
import functools
import math

import jax
import jax.numpy as jnp
from jax import lax
from jax.experimental import pallas as pl
from jax.experimental.pallas import tpu as pltpu

F32 = jnp.float32
BF16 = jnp.bfloat16

D_MODEL = 1024
D_ATTN = 512
D_SSM = 256
D_MEM = 256
D_MIX = D_ATTN + D_SSM + D_MEM
HEAD_DIM = 64
N_HEADS_A = D_ATTN // HEAD_DIM
MOBA_BLOCK = 256
MOBA_TOPK = 3
SSM_CH = 16
SSM_GROUPS = D_SSM // SSM_CH
SSM_STATE = 64
SSM_LANES = SSM_GROUPS * SSM_STATE
MEM_HEADS = 4
D_IN = 3 * D_ATTN + D_SSM + D_MEM + D_MIX
DEPTH = 1
DEEPNORM_ALPHA = (2.0 * DEPTH) ** 0.25
LN_EPS = 1e-5
NEG_INF = -1e30

LANES = 128
SUBLANES = 8
SSM_CHAINS = SUBLANES

_NT = (((1,), (1,)), ((), ()))


def _vmem_limit(nbytes):
    return pltpu.CompilerParams(vmem_limit_bytes=int(nbytes))


def _proj_kernel(x_ref, w_ref, *out_refs, with_prompt_extras):
    xb = x_ref[...].astype(BF16)

    def seg(lo, hi):
        return jnp.dot(xb, w_ref[:, lo:hi], preferred_element_type=F32)

    c0, c1, c2, c3, c4 = D_ATTN, 2 * D_ATTN, 3 * D_ATTN, 3 * D_ATTN + D_SSM, 3 * D_ATTN + D_SSM + D_MEM
    if with_prompt_extras:
        q_ref, k_ref, v_ref, u_ref, qm_ref, gate_ref, kb_ref, vb_ref, kmean_ref = out_refs
    else:
        q_ref, k_ref, v_ref, u_ref, qm_ref, gate_ref = out_refs
    q_ref[...] = seg(0, c0)
    k = seg(c0, c1)
    k_ref[...] = k
    v = seg(c1, c2)
    v_ref[...] = v
    u_ref[...] = seg(c2, c3)
    qm_ref[...] = seg(c3, c4).astype(qm_ref.dtype)
    gate_ref[...] = seg(c4, D_IN)
    if with_prompt_extras:
        kb_ref[...] = k.astype(BF16)
        vb_ref[...] = v.astype(BF16)
        tm = k.shape[0]
        kmean_ref[0] = jnp.sum(k.reshape(tm // MOBA_BLOCK, MOBA_BLOCK, D_ATTN), axis=1) * (1.0 / MOBA_BLOCK)


def _in_proj_prompt(x2d, w_bf, batch, seq, tm):
    m = x2d.shape[0]
    n_seg = SSM_CHAINS // batch
    seg_len = seq // n_seg
    tiles_per_seq = seq // tm
    tiles_per_seg = seg_len // tm
    row = lambda i: (i, 0)

    def u_map(i):
        b = i // tiles_per_seq
        t = i % tiles_per_seq
        return (t % tiles_per_seg, b * n_seg + t // tiles_per_seg)

    out_shape = (
        jax.ShapeDtypeStruct((m, D_ATTN), F32),
        jax.ShapeDtypeStruct((m, D_ATTN), F32),
        jax.ShapeDtypeStruct((m, D_ATTN), F32),
        jax.ShapeDtypeStruct((seg_len, SSM_CHAINS * D_SSM), F32),
        jax.ShapeDtypeStruct((m, D_MEM), BF16),
        jax.ShapeDtypeStruct((m, D_MIX), F32),
        jax.ShapeDtypeStruct((m, D_ATTN), BF16),
        jax.ShapeDtypeStruct((m, D_ATTN), BF16),
        jax.ShapeDtypeStruct((m // tm, tm // MOBA_BLOCK, D_ATTN), F32),
    )
    out_specs = (
        pl.BlockSpec((tm, D_ATTN), row),
        pl.BlockSpec((tm, D_ATTN), row),
        pl.BlockSpec((tm, D_ATTN), row),
        pl.BlockSpec((tm, D_SSM), u_map),
        pl.BlockSpec((tm, D_MEM), row),
        pl.BlockSpec((tm, D_MIX), row),
        pl.BlockSpec((tm, D_ATTN), row),
        pl.BlockSpec((tm, D_ATTN), row),
        pl.BlockSpec((1, tm // MOBA_BLOCK, D_ATTN), lambda i: (i, 0, 0)),
    )
    return pl.pallas_call(
        functools.partial(_proj_kernel, with_prompt_extras=True),
        grid=(m // tm,),
        in_specs=[pl.BlockSpec((tm, D_MODEL), row),
                  pl.BlockSpec((D_MODEL, D_IN), lambda i: (0, 0), pipeline_mode=pl.Buffered(1))],
        out_specs=out_specs,
        out_shape=out_shape,
        compiler_params=_vmem_limit(52 * 2**20),
        name="in_proj_prompt",
    )(x2d, w_bf)


def _in_proj_sample(x2d, w_bf, tm):
    m = x2d.shape[0]
    row = lambda i: (i, 0)
    out_shape = (
        jax.ShapeDtypeStruct((m, D_ATTN), F32),
        jax.ShapeDtypeStruct((m, D_ATTN), F32),
        jax.ShapeDtypeStruct((m, D_ATTN), F32),
        jax.ShapeDtypeStruct((m, D_SSM), F32),
        jax.ShapeDtypeStruct((m, D_MEM), BF16),
        jax.ShapeDtypeStruct((m, D_MIX), F32),
    )
    widths = (D_ATTN, D_ATTN, D_ATTN, D_SSM, D_MEM, D_MIX)
    return pl.pallas_call(
        functools.partial(_proj_kernel, with_prompt_extras=False),
        grid=(m // tm,),
        in_specs=[pl.BlockSpec((tm, D_MODEL), row),
                  pl.BlockSpec((D_MODEL, D_IN), lambda i: (0, 0), pipeline_mode=pl.Buffered(1))],
        out_specs=tuple(pl.BlockSpec((tm, w), row) for w in widths),
        out_shape=out_shape,
        compiler_params=_vmem_limit(52 * 2**20),
        name="in_proj_sample",
    )(x2d, w_bf)


def _mem_kv_kernel(mem_ref, w_ref, mk_ref, mv_ref):
    kv = jnp.dot(mem_ref[...].astype(BF16), w_ref[...], preferred_element_type=F32)
    mk_ref[...] = kv[:, :D_MEM]
    mv_ref[...] = kv[:, D_MEM:]


def _mem_kv(mem2d, w_bf):
    m = mem2d.shape[0]
    tm = min(m, 512)
    row = lambda i: (i, 0)
    return pl.pallas_call(
        _mem_kv_kernel,
        grid=(m // tm,),
        in_specs=[pl.BlockSpec((tm, D_MODEL), row),
                  pl.BlockSpec((D_MODEL, 2 * D_MEM), lambda i: (0, 0))],
        out_specs=(pl.BlockSpec((tm, D_MEM), row), pl.BlockSpec((tm, D_MEM), row)),
        out_shape=(jax.ShapeDtypeStruct((m, D_MEM), F32), jax.ShapeDtypeStruct((m, D_MEM), F32)),
        name="mem_kv",
    )(mem2d, w_bf)


def _select_topk(scores, valid, index, axis):
    lowest = jnp.finfo(F32).min
    s = jnp.where(valid, scores, lowest)
    sel = jnp.zeros(scores.shape, dtype=jnp.bool_)
    for _ in range(MOBA_TOPK):
        m = jnp.max(s, axis=axis, keepdims=True)
        first = jnp.min(jnp.where(s == m, index, jnp.float32(2**30)), axis=axis, keepdims=True)
        pick = index == first
        sel = sel | (pick & valid)
        s = jnp.where(pick, lowest, s)
    return sel


_MOBA_Q_BLOCKS = 2
_MOBA_ROW_GROUP = 128


def _moba_prompt_kernel(q_ref, k_ref, v_ref, kmean_ref, o_ref,
                        kaug_ref, vaug_ref, lhs_ref, s_ref, p_ref, alpha_ref, m_ref, acc_ref):
    ti = pl.program_id(2)
    blk = MOBA_BLOCK
    tq = q_ref.shape[1]
    qb = tq // blk
    nb = k_ref.shape[1] // blk
    half = LANES // 2
    lane_b = lax.broadcasted_iota(jnp.int32, (blk, LANES), 1)
    head0_b = lane_b < half

    @pl.when(ti == 0)
    def _build_operands():
        def body(j, carry):
            rows = pl.ds(pl.multiple_of(j * blk, blk), blk)
            kb = k_ref[0, rows, :].astype(F32)
            vb = v_ref[0, rows, :].astype(F32)
            hot0 = jnp.where(lane_b == half + j, 1.0, 0.0)
            hot1 = jnp.where(lane_b == j, 1.0, 0.0)
            kaug_ref[0, rows, :] = jnp.where(head0_b, kb, hot0).astype(BF16)
            kaug_ref[1, rows, :] = jnp.where(head0_b, hot1, kb).astype(BF16)
            vaug_ref[0, rows, :] = jnp.where(head0_b, vb, 1.0).astype(BF16)
            vaug_ref[1, rows, :] = jnp.where(head0_b, 1.0, vb).astype(BF16)
            return carry
        lax.fori_loop(0, nb, body, 0)

    q2 = q_ref[0]
    means = kmean_ref[0]
    lane_m = lax.broadcasted_iota(jnp.int32, (nb, LANES), 1)
    pad = [jnp.zeros((half - nb, LANES), F32)] if nb < half else []
    means_p = jnp.concatenate([jnp.where(lane_m >= half, means, 0.0)] + pad
                              + [jnp.where(lane_m < half, means, 0.0)] + pad, axis=0)
    sc_t = lax.dot_general(means_p, q2, _NT, precision=lax.Precision.HIGHEST,
                           preferred_element_type=F32)

    slot = lax.broadcasted_iota(jnp.int32, (nb, tq), 0)
    slot_f = slot.astype(F32)
    n_full = ti * qb + lax.broadcasted_iota(jnp.int32, (nb, tq), 1) // blk
    bias_t = []
    for off in (0, half):
        sel = _select_topk(sc_t[off:off + nb], slot < n_full, slot_f, axis=0)
        bias_t.append(jnp.where(sel | (slot == n_full), 0.0, NEG_INF))
        if nb < half:
            bias_t.append(jnp.zeros((half - nb, tq), F32))
    bias = jnp.concatenate(bias_t, axis=0).T
    head0_q = lax.broadcasted_iota(jnp.int32, (tq, LANES), 1) < half
    qs = q2 * (HEAD_DIM ** -0.5)
    lhs_ref[0] = jnp.where(head0_q, qs, bias).astype(BF16)
    lhs_ref[1] = jnp.where(head0_q, bias, qs).astype(BF16)
    m_ref[...] = jnp.full(m_ref.shape, NEG_INF, F32)
    acc_ref[...] = jnp.zeros_like(acc_ref)

    rg = _MOBA_ROW_GROUP
    c_i = lax.broadcasted_iota(jnp.int32, (rg, blk), 1)
    r_i = lax.broadcasted_iota(jnp.int32, (rg, blk), 0)

    def key_rows(j):
        return pl.ds(j * blk if isinstance(j, int) else pl.multiple_of(j * blk, blk), blk)

    def score_block(j, slot):
        for e in (0, 1):
            s_ref[slot, e] = lax.dot_general(lhs_ref[e], kaug_ref[e, key_rows(j), :], _NT,
                                             preferred_element_type=F32)

    def softmax_pv_block(j, slot, causal):
        for e in (0, 1):
            for g in range(tq // rg):
                rows = pl.ds(g * rg, rg)
                s = s_ref[slot, e, rows, :]
                if causal:
                    s = jnp.where(j * blk + c_i <= ti * tq + g * rg + r_i, s, NEG_INF)
                m_old = m_ref[e, rows, :]
                m_new = jnp.maximum(m_old, jnp.max(s, axis=1, keepdims=True))
                alpha_ref[e, rows, :] = jnp.exp(m_old - m_new)
                p_ref[e, rows, :] = jnp.exp(s - jnp.concatenate([m_new] * (blk // LANES), axis=1)).astype(BF16)
                m_ref[e, rows, :] = m_new
            pv = jnp.dot(p_ref[e], vaug_ref[e, key_rows(j), :], preferred_element_type=F32)
            acc_ref[e] = alpha_ref[e] * acc_ref[e] + pv

    n_past = ti * qb
    score_block(0, 0)

    def past_pair(jj, carry):
        j = 2 * jj
        score_block(j + 1, 1)
        softmax_pv_block(j, 0, causal=False)
        score_block(j + 2, 0)
        softmax_pv_block(j + 1, 1, causal=False)
        return carry

    lax.fori_loop(0, n_past // 2, past_pair, 0)
    for d in range(qb):
        if d + 1 < qb:
            score_block(n_past + d + 1, (d + 1) % 2)
        softmax_pv_block(n_past + d, d % 2, causal=True)

    a0 = acc_ref[0]
    a1 = acc_ref[1]
    o0 = a0 / pltpu.roll(a0, half, axis=1)
    o1 = a1 / pltpu.roll(a1, half, axis=1)
    o_ref[0] = jnp.where(head0_q, o0, o1)


def _moba_prompt(q, kb, vb, kmean):
    b, s, _ = q.shape
    nb = s // MOBA_BLOCK
    tq = _MOBA_Q_BLOCKS * MOBA_BLOCK
    assert nb <= LANES // 2 and nb % SUBLANES == 0 and s % tq == 0 and _MOBA_Q_BLOCKS % 2 == 0
    tile = lambda bi, hp, ti: (bi, ti, hp)
    whole = lambda bi, hp, ti: (bi, 0, hp)
    return pl.pallas_call(
        _moba_prompt_kernel,
        grid=(b, D_ATTN // LANES, s // tq),
        in_specs=[pl.BlockSpec((1, tq, LANES), tile),
                  pl.BlockSpec((1, s, LANES), whole),
                  pl.BlockSpec((1, s, LANES), whole),
                  pl.BlockSpec((1, nb, LANES), whole)],
        out_specs=pl.BlockSpec((1, tq, LANES), tile),
        out_shape=jax.ShapeDtypeStruct((b, s, D_ATTN), F32),
        scratch_shapes=[pltpu.VMEM((2, s, LANES), BF16),
                        pltpu.VMEM((2, s, LANES), BF16),
                        pltpu.VMEM((2, tq, LANES), BF16),
                        pltpu.VMEM((2, 2, tq, MOBA_BLOCK), F32),
                        pltpu.VMEM((2, tq, MOBA_BLOCK), BF16),
                        pltpu.VMEM((2, tq, LANES), F32),
                        pltpu.VMEM((2, tq, LANES), F32),
                        pltpu.VMEM((2, tq, LANES), F32)],
        compiler_params=pltpu.CompilerParams(
            dimension_semantics=("arbitrary", "arbitrary", "arbitrary"),
            vmem_limit_bytes=40 * 2**20),
        name="moba_prompt",
    )(q, kb, vb, kmean)


def _ssm_disc_kernel(a_re_ref, a_im_ref, log_dt_ref, bt_re_ref, bt_im_ref,
                     abar_re_ref, abar_im_ref, bbt_re_ref, bbt_im_ref):
    a_re = a_re_ref[...]
    a_im = a_im_ref[...]
    dt = jnp.exp(log_dt_ref[...])
    mag = jnp.exp(a_re * dt)
    ang = a_im * dt
    abar_re = mag * jnp.cos(ang)
    abar_im = mag * jnp.sin(ang)
    den = a_re * a_re + a_im * a_im
    f_re = ((abar_re - 1.0) * a_re + abar_im * a_im) / den
    f_im = (abar_im * a_re - (abar_re - 1.0) * a_im) / den
    abar_re_ref[...] = abar_re
    abar_im_ref[...] = abar_im
    bt_re = bt_re_ref[...]
    bt_im = bt_im_ref[...]
    bbt_re_ref[...] = f_re[:, None, :] * bt_re - f_im[:, None, :] * bt_im
    bbt_im_ref[...] = f_re[:, None, :] * bt_im + f_im[:, None, :] * bt_re


def _ssm_operands(a_re, a_im, log_dt, b_re, b_im, c_re, c_im):
    g, p, c = b_re.shape
    abar_re, abar_im, bbt_re, bbt_im = pl.pallas_call(
        _ssm_disc_kernel,
        out_shape=(jax.ShapeDtypeStruct((g, p), F32), jax.ShapeDtypeStruct((g, p), F32),
                   jax.ShapeDtypeStruct((g, c, p), F32), jax.ShapeDtypeStruct((g, c, p), F32)),
        name="ssm_discretize",
    )(a_re.astype(F32), a_im.astype(F32), log_dt.astype(F32).reshape(g, 1),
      b_re.astype(F32).transpose(0, 2, 1), b_im.astype(F32).transpose(0, 2, 1))
    eye = jnp.eye(g, dtype=F32)

    def in_block_diag(bbt):
        return jnp.einsum('gcp,gh->gchp', bbt, eye).reshape(g * c, g * p)

    def out_block_diag(cm):
        return jnp.einsum('gcp,gh->gphc', cm, eye).reshape(g * p, g * c)

    b_mat = jnp.concatenate([in_block_diag(bbt_re), in_block_diag(bbt_im)], axis=1)
    c_mat = jnp.concatenate([out_block_diag(c_re.astype(F32)),
                             -out_block_diag(c_im.astype(F32))], axis=0)
    a_row = jnp.concatenate([abar_re.reshape(1, g * p), abar_im.reshape(1, g * p)], axis=1)
    return a_row, b_mat, c_mat


def _complex_step(a_re, a_im, h_re, h_im, bu_re, bu_im):
    return (a_re * h_re - a_im * h_im + bu_re, a_re * h_im + a_im * h_re + bu_im)


def _ssm_output(h_all, u, c_ref, d_ref, wg_ref, bg_ref):
    y = jnp.dot(h_all.astype(BF16), c_ref[...], preferred_element_type=F32) + d_ref[...] * u
    y = jax.nn.gelu(y)
    z = jnp.dot(y.astype(BF16), wg_ref[...], preferred_element_type=F32) + bg_ref[...]
    return z[:, :D_SSM] * jax.nn.sigmoid(z[:, D_SSM:])


def _ssm_prompt_kernel(u_ref, a_ref, b_ref, c_ref, d_ref, wg_ref, bg_ref, y_ref, hfin_ref,
                       hb_ref, h_ref, init_ref, *, n_seg):
    pss = pl.program_id(0)
    c = pl.program_id(1)
    n_chunks = pl.num_programs(1)
    rows = u_ref.shape[0]
    steps = rows // SSM_CHAINS
    n = SSM_LANES
    a_re = jnp.broadcast_to(a_ref[:, :n], (SSM_CHAINS, n))
    a_im = jnp.broadcast_to(a_ref[:, n:], (SSM_CHAINS, n))

    @pl.when((pss == 0) & (c == 0))
    def _zero_state():
        h_ref[...] = jnp.zeros_like(h_ref)

    @pl.when((pss == 1) & (c == 0))
    def _true_initial_state():
        h_ref[...] = init_ref[...]

    u = u_ref[...]
    hb_ref[...] = jnp.dot(u.astype(BF16), b_ref[...], preferred_element_type=F32)

    def scan(store):
        def step(s, carry):
            h_re, h_im = carry
            r = pl.ds(pl.multiple_of(s * SSM_CHAINS, SSM_CHAINS), SSM_CHAINS)
            h_re, h_im = _complex_step(a_re, a_im, h_re, h_im, hb_ref[r, :n], hb_ref[r, n:])
            if store:
                hb_ref[r, :n] = h_re
                hb_ref[r, n:] = h_im
            return h_re, h_im
        h_re, h_im = lax.fori_loop(0, steps, step, (h_ref[:, :n], h_ref[:, n:]))
        h_ref[:, :n] = h_re
        h_ref[:, n:] = h_im

    @pl.when(pss == 0)
    def _local_pass():
        scan(False)

    @pl.when((pss == 0) & (c == n_chunks - 1))
    def _segment_initial_states():
        seg_len = steps * n_chunks
        p_re, p_im = a_re, a_im
        for _ in range(int(math.log2(seg_len))):
            p_re, p_im = p_re * p_re - p_im * p_im, 2.0 * p_re * p_im
        e_re = h_ref[:, :n]
        e_im = h_ref[:, n:]
        first = (lax.broadcasted_iota(jnp.int32, (SSM_CHAINS, n), 0) % n_seg) == 0
        i_re = jnp.zeros((SSM_CHAINS, n), F32)
        i_im = jnp.zeros((SSM_CHAINS, n), F32)
        for _ in range(n_seg - 1):
            t_re = p_re * i_re - p_im * i_im + e_re
            t_im = p_re * i_im + p_im * i_re + e_im
            i_re = jnp.where(first, 0.0, pltpu.roll(t_re, 1, axis=0))
            i_im = jnp.where(first, 0.0, pltpu.roll(t_im, 1, axis=0))
        init_ref[:, :n] = i_re
        init_ref[:, n:] = i_im

    @pl.when(pss == 1)
    def _output_pass():
        scan(True)
        y_ref[...] = _ssm_output(hb_ref[...], u, c_ref, d_ref, wg_ref, bg_ref).astype(y_ref.dtype)

    @pl.when((pss == 1) & (c == n_chunks - 1))
    def _final_state():
        hfin_ref[...] = h_ref[...]


def _ssm_prompt(u_chain, a_row, b_bf, c_bf, d_row, wg_bf, bg_row, n_seg, steps_per_chunk):
    total = u_chain.shape[0]
    rows = steps_per_chunk * SSM_CHAINS
    n_chunks = total // rows
    seg_len = total // SSM_CHAINS
    assert 2 ** int(math.log2(seg_len)) == seg_len
    const = lambda p, c: (0, 0)
    return pl.pallas_call(
        functools.partial(_ssm_prompt_kernel, n_seg=n_seg),
        grid=(2, n_chunks),
        in_specs=[pl.BlockSpec((rows, D_SSM), lambda p, c: (c, 0)),
                  pl.BlockSpec((1, 2 * SSM_LANES), const),
                  pl.BlockSpec((D_SSM, 2 * SSM_LANES), const),
                  pl.BlockSpec((2 * SSM_LANES, D_SSM), const),
                  pl.BlockSpec((1, D_SSM), const),
                  pl.BlockSpec((D_SSM, 2 * D_SSM), const),
                  pl.BlockSpec((1, 2 * D_SSM), const)],
        out_specs=(pl.BlockSpec((rows, D_SSM), lambda p, c: (p * c, 0)),
                   pl.BlockSpec((SSM_CHAINS, 2 * SSM_LANES), const)),
        out_shape=(jax.ShapeDtypeStruct((total, D_SSM), F32),
                   jax.ShapeDtypeStruct((SSM_CHAINS, 2 * SSM_LANES), F32)),
        scratch_shapes=[pltpu.VMEM((rows, 2 * SSM_LANES), F32),
                        pltpu.VMEM((SSM_CHAINS, 2 * SSM_LANES), F32),
                        pltpu.VMEM((SSM_CHAINS, 2 * SSM_LANES), F32)],
        compiler_params=pltpu.CompilerParams(dimension_semantics=("arbitrary", "arbitrary"),
                                             vmem_limit_bytes=40 * 2**20),
        name="ssm_prompt",
    )(u_chain, a_row, b_bf, c_bf, d_row, wg_bf, bg_row)


def _ssm_sample_kernel(u_ref, h0_ref, a_ref, b_ref, c_ref, d_ref, wg_ref, bg_ref, y_ref, hfin_ref):
    steps, batch, _ = u_ref.shape
    n = SSM_LANES
    a_re = jnp.broadcast_to(a_ref[:, :n], (batch, n))
    a_im = jnp.broadcast_to(a_ref[:, n:], (batch, n))
    h_re = h0_ref[:, :n]
    h_im = h0_ref[:, n:]
    for t in range(steps):
        u = u_ref[t]
        bu = jnp.dot(u, b_ref[...], precision=lax.Precision.HIGHEST, preferred_element_type=F32)
        h_re, h_im = _complex_step(a_re, a_im, h_re, h_im, bu[:, :n], bu[:, n:])
        h_all = jnp.concatenate([h_re, h_im], axis=1)
        y_ref[t] = _ssm_output(h_all, u, c_ref, d_ref, wg_ref, bg_ref)
    hfin_ref[:, :n] = h_re
    hfin_ref[:, n:] = h_im


def _ssm_sample(u_tb, h0, a_row, b_f32, c_bf, d_row, wg_bf, bg_row):
    steps, batch, _ = u_tb.shape
    return pl.pallas_call(
        _ssm_sample_kernel,
        out_shape=(jax.ShapeDtypeStruct((steps, batch, D_SSM), F32),
                   jax.ShapeDtypeStruct((batch, 2 * SSM_LANES), F32)),
        compiler_params=_vmem_limit(40 * 2**20),
        name="ssm_sample",
    )(u_tb, h0, a_row, b_f32, c_bf, d_row, wg_bf, bg_row)


def _mem_attn_prompt_kernel(q_ref, mk_ref, mv_ref, o_ref):
    rows = q_ref.shape[0]
    half = LANES // 2
    scale = (D_MEM // MEM_HEADS) ** -0.5
    head0 = lax.broadcasted_iota(jnp.int32, (rows, LANES), 1) < half
    outs = []
    for hp in range(D_MEM // LANES):
        cols = slice(hp * LANES, (hp + 1) * LANES)
        q2 = q_ref[:, cols].astype(F32)
        k2 = mk_ref[:, cols].astype(BF16)
        v2 = mv_ref[:, cols].astype(BF16)
        res = []
        for e in (0, 1):
            mine = head0 if e == 0 else jnp.logical_not(head0)
            lhs = jnp.where(mine, q2, 0.0).astype(BF16)
            s = lax.dot_general(lhs, k2, _NT, preferred_element_type=F32) * scale
            p = jnp.exp(s - jnp.max(s, axis=1, keepdims=True))
            l = jnp.sum(p, axis=1, keepdims=True)
            res.append(jnp.dot(p.astype(BF16), v2, preferred_element_type=F32) / l)
        outs.append(jnp.where(head0, res[0], res[1]))
    o_ref[...] = jnp.concatenate(outs, axis=1)


def _mem_attn_prompt(qm, mk, mv, batch, seq, tm):
    n_mem = mk.shape[0] // batch
    tiles = seq // tm
    return pl.pallas_call(
        _mem_attn_prompt_kernel,
        grid=(batch, tiles),
        in_specs=[pl.BlockSpec((tm, D_MEM), lambda b, i: (b * tiles + i, 0)),
                  pl.BlockSpec((n_mem, D_MEM), lambda b, i: (b, 0)),
                  pl.BlockSpec((n_mem, D_MEM), lambda b, i: (b, 0))],
        out_specs=pl.BlockSpec((tm, D_MEM), lambda b, i: (b * tiles + i, 0)),
        out_shape=jax.ShapeDtypeStruct((batch * seq, D_MEM), F32),
        name="mem_attn_prompt",
    )(qm, mk, mv)


def _head_rows(x, n_heads, head_dim):
    head_of_lane = lax.broadcasted_iota(jnp.int32, x.shape, 1) // head_dim
    return jnp.concatenate([jnp.where(head_of_lane == h, x, 0.0) for h in range(n_heads)], axis=0)


def _head_diagonal(o_full, t, n_heads, head_dim):
    head_of_lane = lax.broadcasted_iota(jnp.int32, (t, o_full.shape[1]), 1) // head_dim
    out = jnp.zeros((t, o_full.shape[1]), F32)
    for h in range(n_heads):
        out = out + jnp.where(head_of_lane == h, o_full[h * t:(h + 1) * t], 0.0)
    return out


def _mem_attn_sample_kernel(q_ref, mkt_ref, mvt_ref, o_ref):
    group, t, _ = q_ref.shape
    head_dim = D_MEM // MEM_HEADS
    scale = head_dim ** -0.5
    for g in range(group):
        qbd = _head_rows(q_ref[g].astype(F32) * scale, MEM_HEADS, head_dim).astype(BF16)
        s = jnp.dot(qbd, mkt_ref[g].astype(BF16), preferred_element_type=F32)
        p = jnp.exp(s - jnp.max(s, axis=1, keepdims=True))
        l = jnp.sum(p, axis=1, keepdims=True)
        o_full = lax.dot_general(p.astype(BF16), mvt_ref[g].astype(BF16), _NT, preferred_element_type=F32) / l
        o_ref[g] = _head_diagonal(o_full, t, MEM_HEADS, head_dim)


def _mem_attn_sample(qm, mkt, mvt, group):
    db, t, _ = qm.shape
    n_mem = mkt.shape[2]
    blk = lambda i: (i, 0, 0)
    return pl.pallas_call(
        _mem_attn_sample_kernel,
        grid=(db // group,),
        in_specs=[pl.BlockSpec((group, t, D_MEM), blk),
                  pl.BlockSpec((group, D_MEM, n_mem), blk),
                  pl.BlockSpec((group, D_MEM, n_mem), blk)],
        out_specs=pl.BlockSpec((group, t, D_MEM), blk),
        out_shape=jax.ShapeDtypeStruct((db, t, D_MEM), F32),
        name="mem_attn_sample",
    )(qm, mkt, mvt)


def _merge_kernel(x_ref, attn_ref, ssm_ref, mem_ref, gate_ref, w_ref, g_ref, b_ref, y_ref):
    mixed = jnp.concatenate([attn_ref[...], ssm_ref[...], mem_ref[...]], axis=1) * jax.nn.silu(gate_ref[...])
    out = jnp.dot(mixed.astype(BF16), w_ref[...], preferred_element_type=F32)
    z = DEEPNORM_ALPHA * x_ref[...] + out
    mu = jnp.mean(z, axis=1, keepdims=True)
    zc = z - mu
    var = jnp.mean(zc * zc, axis=1, keepdims=True)
    y_ref[...] = zc * lax.rsqrt(var + LN_EPS) * g_ref[...] + b_ref[...]


def _merge(x2d, attn, ssm, mem, gate, w_bf, g_row, b_row, tm, ssm_map):
    m = x2d.shape[0]
    row = lambda i: (i, 0)
    const = lambda i: (0, 0)
    return pl.pallas_call(
        _merge_kernel,
        grid=(m // tm,),
        in_specs=[pl.BlockSpec((tm, D_MODEL), row),
                  pl.BlockSpec((tm, D_ATTN), row),
                  pl.BlockSpec((tm, D_SSM), ssm_map),
                  pl.BlockSpec((tm, D_MEM), row),
                  pl.BlockSpec((tm, D_MIX), row),
                  pl.BlockSpec((D_MIX, D_MODEL), const),
                  pl.BlockSpec((1, D_MODEL), const),
                  pl.BlockSpec((1, D_MODEL), const)],
        out_specs=pl.BlockSpec((tm, D_MODEL), row),
        out_shape=jax.ShapeDtypeStruct((m, D_MODEL), F32),
        compiler_params=_vmem_limit(48 * 2**20),
        name="merge",
    )(x2d, attn, ssm, mem, gate, w_bf, g_row, b_row)


def _moba_sample_kernel(pt_ref, q_ref, kn_ref, vn_ref, ck_ref, cv_ref, o_ref,
                        buf_ref, sem_ref, s_ref, p_ref, *, pages_per_chunk, chunks, page_size):
    ppc = pages_per_chunk
    b = pl.program_id(0)
    n_seq = pl.num_programs(0)
    items = 2 * chunks
    t = q_ref.shape[1]
    rows = N_HEADS_A * t
    n_pages = ppc * chunks
    pages_per_block = MOBA_BLOCK // page_size
    n_blocks = n_pages // pages_per_block
    qbd_bf = _head_rows(q_ref[0] * (HEAD_DIM ** -0.5), N_HEADS_A, HEAD_DIM).astype(BF16)
    lane = lax.broadcasted_iota(jnp.int32, (rows, page_size), 1)

    def pad_rows(x):
        return jnp.concatenate([x, jnp.zeros((page_size - x.shape[0], x.shape[1]), x.dtype)], axis=0)

    def chunk_copies(seq, item, slot):
        src_ref, chunk = (ck_ref, item) if item < chunks else (cv_ref, item - chunks)
        return [pltpu.make_async_copy(src_ref.at[pt_ref[seq, chunk * ppc + i]], buf_ref.at[slot, i],
                                      sem_ref.at[slot]) for i in range(ppc)]

    @pl.when(b == 0)
    def _first_chunk():
        for cp in chunk_copies(b, 0, 0):
            cp.start()

    def key_chunk(item, slot):
        for i in range(ppc):
            s_ref[item * ppc + i] = jnp.dot(qbd_bf, buf_ref[slot, i].astype(BF16), preferred_element_type=F32)

    def select_and_softmax():
        sc = jnp.zeros((rows, page_size), F32)
        for j in range(n_blocks):
            bs = s_ref[j * pages_per_block]
            for pg in range(1, pages_per_block):
                bs = bs + s_ref[j * pages_per_block + pg]
            sc = jnp.where(lane == j, jnp.sum(bs, axis=1, keepdims=True), sc)
        sel = _select_topk(sc, lane < n_blocks, lane.astype(F32), axis=1)
        lo = lax.dot_general(qbd_bf, pad_rows(kn_ref[0]).astype(BF16), _NT, preferred_element_type=F32)
        t_row = lax.broadcasted_iota(jnp.int32, (rows, page_size), 0) % t
        lo = jnp.where(lane <= t_row, lo, NEG_INF)
        mx = lo
        for j in range(n_blocks):
            bm = s_ref[j * pages_per_block]
            for pg in range(1, pages_per_block):
                bm = jnp.maximum(bm, s_ref[j * pages_per_block + pg])
            mx = jnp.maximum(mx, jnp.where(sel[:, j:j + 1], bm, NEG_INF))
        m = jnp.max(mx, axis=1, keepdims=True)
        po = jnp.exp(lo - m)
        lsum = po
        for j in range(n_blocks):
            sj = sel[:, j:j + 1]
            for pg in range(j * pages_per_block, (j + 1) * pages_per_block):
                p = jnp.where(sj, jnp.exp(s_ref[pg] - m), 0.0)
                lsum = lsum + p
                p_ref[pg] = p.astype(BF16)
        return po.astype(BF16), jnp.sum(lsum, axis=1, keepdims=True)

    def value_chunk(item, slot, acc):
        for i in range(ppc):
            acc = acc + lax.dot_general(p_ref[(item - chunks) * ppc + i], buf_ref[slot, i].astype(BF16), _NT,
                                        preferred_element_type=F32)
        return acc

    acc = jnp.zeros((rows, D_ATTN), F32)
    for item in range(items):
        slot = item % 2
        if item + 1 < items:
            for cp in chunk_copies(b, item + 1, 1 - slot):
                cp.start()
        else:
            @pl.when(b + 1 < n_seq)
            def _next_sequence():
                for cp in chunk_copies(b + 1, 0, 1 - slot):
                    cp.start()
        for cp in chunk_copies(b, item, slot):
            cp.wait()
        if item < chunks:
            key_chunk(item, slot)
        else:
            if item == chunks:
                p_own, l = select_and_softmax()
            acc = value_chunk(item, slot, acc)

    acc = acc + jnp.dot(p_own, pad_rows(vn_ref[0]).astype(BF16), preferred_element_type=F32)
    o_ref[0] = _head_diagonal(acc / l, t, N_HEADS_A, HEAD_DIM)


def _moba_sample(q, k_new, v_new, cache_k, cache_v, page_table, pages_per_chunk):
    db, t, _ = q.shape
    n_pages = page_table.shape[1]
    page_size = cache_k.shape[2]
    chunks = n_pages // pages_per_chunk
    assert (n_pages * page_size) % MOBA_BLOCK == 0 and MOBA_BLOCK % page_size == 0
    assert n_pages * page_size // MOBA_BLOCK >= MOBA_TOPK and t <= page_size and page_size == LANES
    assert n_pages % pages_per_chunk == 0 and (2 * chunks) % 2 == 0
    rows = N_HEADS_A * t
    seq_map = lambda b, pt: (b, 0, 0)
    grid_spec = pltpu.PrefetchScalarGridSpec(
        num_scalar_prefetch=1,
        grid=(db,),
        in_specs=[pl.BlockSpec((1, t, D_ATTN), seq_map)] * 3
                 + [pl.BlockSpec(memory_space=pl.ANY)] * 2,
        out_specs=pl.BlockSpec((1, t, D_ATTN), seq_map),
        scratch_shapes=[pltpu.VMEM((2, pages_per_chunk, D_ATTN, page_size), F32),
                        pltpu.SemaphoreType.DMA((2,)),
                        pltpu.VMEM((n_pages, rows, page_size), F32),
                        pltpu.VMEM((n_pages, rows, page_size), BF16)])
    return pl.pallas_call(
        functools.partial(_moba_sample_kernel, pages_per_chunk=pages_per_chunk, chunks=chunks,
                          page_size=page_size),
        grid_spec=grid_spec,
        out_shape=jax.ShapeDtypeStruct((db, t, D_ATTN), F32),
        compiler_params=pltpu.CompilerParams(dimension_semantics=("arbitrary",),
                                             vmem_limit_bytes=44 * 2**20),
        name="moba_sample",
    )(page_table, q, k_new, v_new, cache_k, cache_v)


def kernel(x_prompt, x_sample, mem_prompt, cache_k, cache_v, state_ssm_re, state_ssm_im,
           cache_mem_k, cache_mem_v, page_table, w_in, w_mem_kv, a_re, a_im, log_dt,
           b_re, b_im, c_re, c_im, d_skip, w_glu, b_glu, w_out, ln_g, ln_b):
    batch, seq, _ = x_prompt.shape
    db, t_new, _ = x_sample.shape
    n_mem = mem_prompt.shape[1]
    n_phys, page_size = cache_k.shape[:2]
    n_seg = SSM_CHAINS // batch
    seg_len = seq // n_seg
    tm = min(512, seg_len)

    w_in_bf = w_in.astype(BF16)
    w_mem_bf = w_mem_kv.astype(BF16)
    w_out_bf = w_out.astype(BF16)
    wg_bf = w_glu.astype(BF16)
    bg_row = b_glu.astype(F32).reshape(1, 2 * D_SSM)
    d_row = d_skip.astype(F32).reshape(1, D_SSM)
    g_row = ln_g.astype(F32).reshape(1, D_MODEL)
    b_row = ln_b.astype(F32).reshape(1, D_MODEL)
    a_row, b_mat, c_mat = _ssm_operands(a_re, a_im, log_dt, b_re, b_im, c_re, c_im)
    b_bf = b_mat.astype(BF16)
    c_bf = c_mat.astype(BF16)

    x2d = x_prompt.reshape(batch * seq, D_MODEL)
    q, k, v, u_chain, qm, gate, kb, vb, kmean = _in_proj_prompt(x2d, w_in_bf, batch, seq, tm)
    attn = _moba_prompt(q.reshape(batch, seq, D_ATTN), kb.reshape(batch, seq, D_ATTN),
                        vb.reshape(batch, seq, D_ATTN), kmean.reshape(batch, seq // MOBA_BLOCK, D_ATTN))
    y_chain, hfin = _ssm_prompt(u_chain.reshape(seg_len * SSM_CHAINS, D_SSM), a_row, b_bf, c_bf, d_row,
                                wg_bf, bg_row, n_seg, steps_per_chunk=min(64, seg_len))
    mk, mv = _mem_kv(mem_prompt.reshape(batch * n_mem, D_MODEL), w_mem_bf)
    mem_o = _mem_attn_prompt(qm, mk, mv, batch, seq, tm)
    tiles_per_seq = seq // tm
    tiles_per_seg = seg_len // tm

    def chain_map(i):
        b = i // tiles_per_seq
        ti = i % tiles_per_seq
        return (ti % tiles_per_seg, b * n_seg + ti // tiles_per_seg)

    y_prompt = _merge(x2d, attn.reshape(batch * seq, D_ATTN), y_chain.reshape(seg_len, SSM_CHAINS * D_SSM),
                      mem_o, gate, w_out_bf, g_row, b_row, tm, chain_map)
    h_last = hfin[n_seg - 1::n_seg]
    ssm_re_p = h_last[:, :SSM_LANES].reshape(batch, SSM_GROUPS, SSM_STATE)
    ssm_im_p = h_last[:, SSM_LANES:].reshape(batch, SSM_GROUPS, SSM_STATE)

    xs2d = x_sample.reshape(db * t_new, D_MODEL)
    tms = min(512, db * t_new)
    q_s, k_s, v_s, u_s, qm_s, gate_s = _in_proj_sample(xs2d, w_in_bf, tms)
    attn_s = _moba_sample(q_s.reshape(db, t_new, D_ATTN), k_s.reshape(db, t_new, D_ATTN),
                          v_s.reshape(db, t_new, D_ATTN),
                          cache_k.transpose(0, 2, 3, 1).reshape(n_phys, D_ATTN, page_size),
                          cache_v.transpose(0, 2, 3, 1).reshape(n_phys, D_ATTN, page_size), page_table,
                          pages_per_chunk=min(16, page_table.shape[1] // 2))
    h0 = jnp.concatenate([state_ssm_re.astype(F32).reshape(db, SSM_LANES),
                          state_ssm_im.astype(F32).reshape(db, SSM_LANES)], axis=1)
    y_tb, hfin_s = _ssm_sample(u_s.reshape(db, t_new, D_SSM).transpose(1, 0, 2), h0, a_row, b_mat, c_bf,
                               d_row, wg_bf, bg_row)
    ssm_s = y_tb.transpose(1, 0, 2).reshape(db * t_new, D_SSM)
    mem_s = _mem_attn_sample(qm_s.reshape(db, t_new, D_MEM),
                             cache_mem_k.transpose(0, 2, 3, 1).reshape(db, D_MEM, n_mem),
                             cache_mem_v.transpose(0, 2, 3, 1).reshape(db, D_MEM, n_mem), group=min(8, db))
    y_sample = _merge(xs2d, attn_s.reshape(db * t_new, D_ATTN), ssm_s, mem_s.reshape(db * t_new, D_MEM),
                      gate_s, w_out_bf, g_row, b_row, tms, lambda i: (i, 0))

    heads = (N_HEADS_A, HEAD_DIM)
    mem_heads = (MEM_HEADS, D_MEM // MEM_HEADS)
    return (y_prompt.reshape(batch, seq, D_MODEL), y_sample.reshape(db, t_new, D_MODEL),
            k.reshape(batch, seq, *heads), v.reshape(batch, seq, *heads), ssm_re_p, ssm_im_p,
            mk.reshape(batch, n_mem, *mem_heads), mv.reshape(batch, n_mem, *mem_heads),
            k_s.reshape(db, t_new, *heads), v_s.reshape(db, t_new, *heads),
            hfin_s[:, :SSM_LANES].reshape(db, SSM_GROUPS, SSM_STATE),
            hfin_s[:, SSM_LANES:].reshape(db, SSM_GROUPS, SSM_STATE))
```

```python
import functools
import math

import jax
import jax.numpy as jnp
from jax import lax
from jax.experimental import pallas as pl
from jax.experimental.pallas import tpu as pltpu

F32 = jnp.float32
BF16 = jnp.bfloat16

D_MODEL = 1024
D_ATTN = 512
D_SSM = 256
D_MEM = 256
D_MIX = D_ATTN + D_SSM + D_MEM
HEAD_DIM = 64
N_HEADS_A = D_ATTN // HEAD_DIM
MOBA_BLOCK = 256
MOBA_TOPK = 3
SSM_CH = 16
SSM_GROUPS = D_SSM // SSM_CH
SSM_STATE = 64
SSM_LANES = SSM_GROUPS * SSM_STATE
MEM_HEADS = 4
D_IN = 3 * D_ATTN + D_SSM + D_MEM + D_MIX
DEPTH = 1
DEEPNORM_ALPHA = (2.0 * DEPTH) ** 0.25
LN_EPS = 1e-5
NEG_INF = -1e30

LANES = 128
SUBLANES = 8
SSM_CHAINS = SUBLANES

_NT = (((1,), (1,)), ((), ()))


def _vmem_limit(nbytes):
    return pltpu.CompilerParams(vmem_limit_bytes=int(nbytes))


def _proj_kernel(x_ref, w_ref, *out_refs, with_prompt_extras):
    xb = x_ref[...].astype(BF16)

    def seg(lo, hi):
        return jnp.dot(xb, w_ref[:, lo:hi], preferred_element_type=F32)

    c0, c1, c2, c3, c4 = D_ATTN, 2 * D_ATTN, 3 * D_ATTN, 3 * D_ATTN + D_SSM, 3 * D_ATTN + D_SSM + D_MEM
    if with_prompt_extras:
        q_ref, k_ref, v_ref, u_ref, qm_ref, gate_ref, kb_ref, vb_ref, kmean_ref = out_refs
    else:
        q_ref, k_ref, v_ref, u_ref, qm_ref, gate_ref = out_refs
    q_ref[...] = seg(0, c0)
    k = seg(c0, c1)
    k_ref[...] = k
    v = seg(c1, c2)
    v_ref[...] = v
    u_ref[...] = seg(c2, c3)
    qm_ref[...] = seg(c3, c4).astype(qm_ref.dtype)
    gate_ref[...] = seg(c4, D_IN)
    if with_prompt_extras:
        kb_ref[...] = k.astype(BF16)
        vb_ref[...] = v.astype(BF16)
        tm = k.shape[0]
        kmean_ref[0] = jnp.sum(k.reshape(tm // MOBA_BLOCK, MOBA_BLOCK, D_ATTN), axis=1) * (1.0 / MOBA_BLOCK)


def _in_proj_prompt(x2d, w_bf, batch, seq, tm):
    m = x2d.shape[0]
    n_seg = SSM_CHAINS // batch
    seg_len = seq // n_seg
    tiles_per_seq = seq // tm
    tiles_per_seg = seg_len // tm
    row = lambda i: (i, 0)

    def u_map(i):
        b = i // tiles_per_seq
        t = i % tiles_per_seq
        return (t % tiles_per_seg, b * n_seg + t // tiles_per_seg)

    out_shape = (
        jax.ShapeDtypeStruct((m, D_ATTN), F32),
        jax.ShapeDtypeStruct((m, D_ATTN), F32),
        jax.ShapeDtypeStruct((m, D_ATTN), F32),
        jax.ShapeDtypeStruct((seg_len, SSM_CHAINS * D_SSM), F32),
        jax.ShapeDtypeStruct((m, D_MEM), BF16),
        jax.ShapeDtypeStruct((m, D_MIX), F32),
        jax.ShapeDtypeStruct((m, D_ATTN), BF16),
        jax.ShapeDtypeStruct((m, D_ATTN), BF16),
        jax.ShapeDtypeStruct((m // tm, tm // MOBA_BLOCK, D_ATTN), F32),
    )
    out_specs = (
        pl.BlockSpec((tm, D_ATTN), row),
        pl.BlockSpec((tm, D_ATTN), row),
        pl.BlockSpec((tm, D_ATTN), row),
        pl.BlockSpec((tm, D_SSM), u_map),
        pl.BlockSpec((tm, D_MEM), row),
        pl.BlockSpec((tm, D_MIX), row),
        pl.BlockSpec((tm, D_ATTN), row),
        pl.BlockSpec((tm, D_ATTN), row),
        pl.BlockSpec((1, tm // MOBA_BLOCK, D_ATTN), lambda i: (i, 0, 0)),
    )
    return pl.pallas_call(
        functools.partial(_proj_kernel, with_prompt_extras=True),
        grid=(m // tm,),
        in_specs=[pl.BlockSpec((tm, D_MODEL), row),
                  pl.BlockSpec((D_MODEL, D_IN), lambda i: (0, 0), pipeline_mode=pl.Buffered(1))],
        out_specs=out_specs,
        out_shape=out_shape,
        compiler_params=_vmem_limit(52 * 2**20),
        name="in_proj_prompt",
    )(x2d, w_bf)


def _in_proj_sample(x2d, w_bf, tm):
    m = x2d.shape[0]
    row = lambda i: (i, 0)
    out_shape = (
        jax.ShapeDtypeStruct((m, D_ATTN), F32),
        jax.ShapeDtypeStruct((m, D_ATTN), F32),
        jax.ShapeDtypeStruct((m, D_ATTN), F32),
        jax.ShapeDtypeStruct((m, D_SSM), F32),
        jax.ShapeDtypeStruct((m, D_MEM), BF16),
        jax.ShapeDtypeStruct((m, D_MIX), F32),
    )
    widths = (D_ATTN, D_ATTN, D_ATTN, D_SSM, D_MEM, D_MIX)
    return pl.pallas_call(
        functools.partial(_proj_kernel, with_prompt_extras=False),
        grid=(m // tm,),
        in_specs=[pl.BlockSpec((tm, D_MODEL), row),
                  pl.BlockSpec((D_MODEL, D_IN), lambda i: (0, 0), pipeline_mode=pl.Buffered(1))],
        out_specs=tuple(pl.BlockSpec((tm, w), row) for w in widths),
        out_shape=out_shape,
        compiler_params=_vmem_limit(52 * 2**20),
        name="in_proj_sample",
    )(x2d, w_bf)


def _mem_kv_kernel(mem_ref, w_ref, mk_ref, mv_ref):
    kv = jnp.dot(mem_ref[...].astype(BF16), w_ref[...], preferred_element_type=F32)
    mk_ref[...] = kv[:, :D_MEM]
    mv_ref[...] = kv[:, D_MEM:]


def _mem_kv(mem2d, w_bf):
    m = mem2d.shape[0]
    tm = min(m, 512)
    row = lambda i: (i, 0)
    return pl.pallas_call(
        _mem_kv_kernel,
        grid=(m // tm,),
        in_specs=[pl.BlockSpec((tm, D_MODEL), row),
                  pl.BlockSpec((D_MODEL, 2 * D_MEM), lambda i: (0, 0))],
        out_specs=(pl.BlockSpec((tm, D_MEM), row), pl.BlockSpec((tm, D_MEM), row)),
        out_shape=(jax.ShapeDtypeStruct((m, D_MEM), F32), jax.ShapeDtypeStruct((m, D_MEM), F32)),
        name="mem_kv",
    )(mem2d, w_bf)


def _select_topk(scores, valid, index, axis):
    lowest = jnp.finfo(F32).min
    s = jnp.where(valid, scores, lowest)
    sel = jnp.zeros(scores.shape, dtype=jnp.bool_)
    for _ in range(MOBA_TOPK):
        m = jnp.max(s, axis=axis, keepdims=True)
        first = jnp.min(jnp.where(s == m, index, jnp.float32(2**30)), axis=axis, keepdims=True)
        pick = index == first
        sel = sel | (pick & valid)
        s = jnp.where(pick, lowest, s)
    return sel


_MOBA_Q_BLOCKS = 2
_MOBA_ROW_GROUP = 128


def _moba_prompt_parts(ti, q_ref, k_ref, v_ref, kmean_ref, o_ref,
                       kaug_ref, vaug_ref, lhs_ref, s_ref, p_ref, alpha_ref, m_ref, acc_ref):
    blk = MOBA_BLOCK
    tq = q_ref.shape[1]
    qb = tq // blk
    nb = k_ref.shape[1] // blk
    half = LANES // 2

    def build_operands():
        lane_b = lax.broadcasted_iota(jnp.int32, (blk, LANES), 1)
        head0_b = lane_b < half

        def body(j, carry):
            rows = pl.ds(pl.multiple_of(j * blk, blk), blk)
            kb = k_ref[0, rows, :].astype(F32)
            vb = v_ref[0, rows, :].astype(F32)
            hot0 = jnp.where(lane_b == half + j, 1.0, 0.0)
            hot1 = jnp.where(lane_b == j, 1.0, 0.0)
            kaug_ref[0, rows, :] = jnp.where(head0_b, kb, hot0).astype(BF16)
            kaug_ref[1, rows, :] = jnp.where(head0_b, hot1, kb).astype(BF16)
            vaug_ref[0, rows, :] = jnp.where(head0_b, vb, 1.0).astype(BF16)
            vaug_ref[1, rows, :] = jnp.where(head0_b, 1.0, vb).astype(BF16)
            return carry
        lax.fori_loop(0, nb, body, 0)

    def select_blocks():
        q2 = q_ref[0]
        means = kmean_ref[0]
        lane_m = lax.broadcasted_iota(jnp.int32, (nb, LANES), 1)
        pad = [jnp.zeros((half - nb, LANES), F32)] if nb < half else []
        means_p = jnp.concatenate([jnp.where(lane_m >= half, means, 0.0)] + pad
                                  + [jnp.where(lane_m < half, means, 0.0)] + pad, axis=0)
        sc_t = lax.dot_general(means_p, q2, _NT, precision=lax.Precision.HIGHEST,
                               preferred_element_type=F32)
        slot = lax.broadcasted_iota(jnp.int32, (nb, tq), 0)
        slot_f = slot.astype(F32)
        n_full = ti * qb + lax.broadcasted_iota(jnp.int32, (nb, tq), 1) // blk
        bias_t = []
        for off in (0, half):
            sel = _select_topk(sc_t[off:off + nb], slot < n_full, slot_f, axis=0)
            bias_t.append(jnp.where(sel | (slot == n_full), 0.0, NEG_INF))
            if nb < half:
                bias_t.append(jnp.zeros((half - nb, tq), F32))
        bias = jnp.concatenate(bias_t, axis=0).T
        head0_q = lax.broadcasted_iota(jnp.int32, (tq, LANES), 1) < half
        qs = q2 * (HEAD_DIM ** -0.5)
        lhs_ref[0] = jnp.where(head0_q, qs, bias).astype(BF16)
        lhs_ref[1] = jnp.where(head0_q, bias, qs).astype(BF16)
        m_ref[...] = jnp.full(m_ref.shape, NEG_INF, F32)
        acc_ref[...] = jnp.zeros_like(acc_ref)

    rg = _MOBA_ROW_GROUP
    c_i = lax.broadcasted_iota(jnp.int32, (rg, blk), 1)
    r_i = lax.broadcasted_iota(jnp.int32, (rg, blk), 0)

    def key_rows(j):
        return pl.ds(j * blk if isinstance(j, int) else pl.multiple_of(j * blk, blk), blk)

    def score_block(j, slot):
        for e in (0, 1):
            s_ref[slot, e] = lax.dot_general(lhs_ref[e], kaug_ref[e, key_rows(j), :], _NT,
                                             preferred_element_type=F32)

    def softmax_pv_block(j, slot, causal):
        for e in (0, 1):
            for g in range(tq // rg):
                rows = pl.ds(g * rg, rg)
                s = s_ref[slot, e, rows, :]
                if causal:
                    s = jnp.where(j * blk + c_i <= ti * tq + g * rg + r_i, s, NEG_INF)
                m_old = m_ref[e, rows, :]
                m_new = jnp.maximum(m_old, jnp.max(s, axis=1, keepdims=True))
                alpha_ref[e, rows, :] = jnp.exp(m_old - m_new)
                p_ref[e, rows, :] = jnp.exp(s - jnp.concatenate([m_new] * (blk // LANES), axis=1)).astype(BF16)
                m_ref[e, rows, :] = m_new
            pv = jnp.dot(p_ref[e], vaug_ref[e, key_rows(j), :], preferred_element_type=F32)
            acc_ref[e] = alpha_ref[e] * acc_ref[e] + pv

    n_past = ti * qb

    def setup():
        pl.when(ti == 0)(build_operands)
        select_blocks()
        score_block(0, 0)

    def past_pair(jj, carry):
        j = 2 * jj
        score_block(j + 1, 1)
        softmax_pv_block(j, 0, causal=False)
        score_block(j + 2, 0)
        softmax_pv_block(j + 1, 1, causal=False)
        return carry

    def past_pairs(lo, hi):
        lax.fori_loop(lo, hi, past_pair, 0)

    def finish():
        for d in range(qb):
            if d + 1 < qb:
                score_block(n_past + d + 1, (d + 1) % 2)
            softmax_pv_block(n_past + d, d % 2, causal=True)
        a0 = acc_ref[0]
        a1 = acc_ref[1]
        o0 = a0 / pltpu.roll(a0, half, axis=1)
        o1 = a1 / pltpu.roll(a1, half, axis=1)
        head0_q = lax.broadcasted_iota(jnp.int32, (tq, LANES), 1) < half
        o_ref[0] = jnp.where(head0_q, o0, o1)

    return setup, past_pairs, finish


def _moba_prompt_scratch(s, tq):
    return [pltpu.VMEM((2, s, LANES), BF16),
            pltpu.VMEM((2, s, LANES), BF16),
            pltpu.VMEM((2, tq, LANES), BF16),
            pltpu.VMEM((2, 2, tq, MOBA_BLOCK), F32),
            pltpu.VMEM((2, tq, MOBA_BLOCK), BF16),
            pltpu.VMEM((2, tq, LANES), F32),
            pltpu.VMEM((2, tq, LANES), F32),
            pltpu.VMEM((2, tq, LANES), F32)]


def _ssm_disc_kernel(a_re_ref, a_im_ref, log_dt_ref, bt_re_ref, bt_im_ref,
                     abar_re_ref, abar_im_ref, bbt_re_ref, bbt_im_ref):
    a_re = a_re_ref[...]
    a_im = a_im_ref[...]
    dt = jnp.exp(log_dt_ref[...])
    mag = jnp.exp(a_re * dt)
    ang = a_im * dt
    abar_re = mag * jnp.cos(ang)
    abar_im = mag * jnp.sin(ang)
    den = a_re * a_re + a_im * a_im
    f_re = ((abar_re - 1.0) * a_re + abar_im * a_im) / den
    f_im = (abar_im * a_re - (abar_re - 1.0) * a_im) / den
    abar_re_ref[...] = abar_re
    abar_im_ref[...] = abar_im
    bt_re = bt_re_ref[...]
    bt_im = bt_im_ref[...]
    bbt_re_ref[...] = f_re[:, None, :] * bt_re - f_im[:, None, :] * bt_im
    bbt_im_ref[...] = f_re[:, None, :] * bt_im + f_im[:, None, :] * bt_re


def _ssm_operands(a_re, a_im, log_dt, b_re, b_im, c_re, c_im):
    g, p, c = b_re.shape
    abar_re, abar_im, bbt_re, bbt_im = pl.pallas_call(
        _ssm_disc_kernel,
        out_shape=(jax.ShapeDtypeStruct((g, p), F32), jax.ShapeDtypeStruct((g, p), F32),
                   jax.ShapeDtypeStruct((g, c, p), F32), jax.ShapeDtypeStruct((g, c, p), F32)),
        name="ssm_discretize",
    )(a_re.astype(F32), a_im.astype(F32), log_dt.astype(F32).reshape(g, 1),
      b_re.astype(F32).transpose(0, 2, 1), b_im.astype(F32).transpose(0, 2, 1))
    eye = jnp.eye(g, dtype=F32)

    def in_block_diag(bbt):
        return jnp.einsum('gcp,gh->gchp', bbt, eye).reshape(g * c, g * p)

    def out_block_diag(cm):
        return jnp.einsum('gcp,gh->gphc', cm, eye).reshape(g * p, g * c)

    b_mat = jnp.concatenate([in_block_diag(bbt_re), in_block_diag(bbt_im)], axis=1)
    c_mat = jnp.concatenate([out_block_diag(c_re.astype(F32)),
                             -out_block_diag(c_im.astype(F32))], axis=0)
    a_row = jnp.concatenate([abar_re.reshape(1, g * p), abar_im.reshape(1, g * p)], axis=1)
    return a_row, b_mat, c_mat


def _complex_step(a_re, a_im, h_re, h_im, bu_re, bu_im):
    return (a_re * h_re - a_im * h_im + bu_re, a_re * h_im + a_im * h_re + bu_im)


def _ssm_output(h_all, u, c_ref, d_ref, wg_ref, bg_ref):
    y = jnp.dot(h_all.astype(BF16), c_ref[...], preferred_element_type=F32) + d_ref[...] * u
    y = jax.nn.gelu(y)
    z = jnp.dot(y.astype(BF16), wg_ref[...], preferred_element_type=F32) + bg_ref[...]
    return z[:, :D_SSM] * jax.nn.sigmoid(z[:, D_SSM:])


def _ssm_prompt_kernel(u_ref, a_ref, b_ref, c_ref, d_ref, wg_ref, bg_ref, y_ref, hfin_ref,
                       hb_ref, h_ref, init_ref, *, n_seg):
    pss = pl.program_id(0)
    c = pl.program_id(1)
    n_chunks = pl.num_programs(1)
    rows = u_ref.shape[0]
    steps = rows // SSM_CHAINS
    n = SSM_LANES
    a_re = jnp.broadcast_to(a_ref[:, :n], (SSM_CHAINS, n))
    a_im = jnp.broadcast_to(a_ref[:, n:], (SSM_CHAINS, n))

    @pl.when((pss == 0) & (c == 0))
    def _zero_state():
        h_ref[...] = jnp.zeros_like(h_ref)

    @pl.when((pss == 1) & (c == 0))
    def _true_initial_state():
        h_ref[...] = init_ref[...]

    u = u_ref[...]
    hb_ref[...] = jnp.dot(u.astype(BF16), b_ref[...], preferred_element_type=F32)

    def scan(store):
        def step(s, carry):
            h_re, h_im = carry
            r = pl.ds(pl.multiple_of(s * SSM_CHAINS, SSM_CHAINS), SSM_CHAINS)
            h_re, h_im = _complex_step(a_re, a_im, h_re, h_im, hb_ref[r, :n], hb_ref[r, n:])
            if store:
                hb_ref[r, :n] = h_re
                hb_ref[r, n:] = h_im
            return h_re, h_im
        h_re, h_im = lax.fori_loop(0, steps, step, (h_ref[:, :n], h_ref[:, n:]))
        h_ref[:, :n] = h_re
        h_ref[:, n:] = h_im

    @pl.when(pss == 0)
    def _local_pass():
        scan(False)

    @pl.when((pss == 0) & (c == n_chunks - 1))
    def _segment_initial_states():
        seg_len = steps * n_chunks
        p_re, p_im = a_re, a_im
        for _ in range(int(math.log2(seg_len))):
            p_re, p_im = p_re * p_re - p_im * p_im, 2.0 * p_re * p_im
        e_re = h_ref[:, :n]
        e_im = h_ref[:, n:]
        first = (lax.broadcasted_iota(jnp.int32, (SSM_CHAINS, n), 0) % n_seg) == 0
        i_re = jnp.zeros((SSM_CHAINS, n), F32)
        i_im = jnp.zeros((SSM_CHAINS, n), F32)
        for _ in range(n_seg - 1):
            t_re = p_re * i_re - p_im * i_im + e_re
            t_im = p_re * i_im + p_im * i_re + e_im
            i_re = jnp.where(first, 0.0, pltpu.roll(t_re, 1, axis=0))
            i_im = jnp.where(first, 0.0, pltpu.roll(t_im, 1, axis=0))
        init_ref[:, :n] = i_re
        init_ref[:, n:] = i_im

    @pl.when(pss == 1)
    def _output_pass():
        scan(True)
        y_ref[...] = _ssm_output(hb_ref[...], u, c_ref, d_ref, wg_ref, bg_ref).astype(y_ref.dtype)

    @pl.when((pss == 1) & (c == n_chunks - 1))
    def _final_state():
        hfin_ref[...] = h_ref[...]


def _ssm_prompt(u_chain, a_row, b_bf, c_bf, d_row, wg_bf, bg_row, n_seg, steps_per_chunk):
    total = u_chain.shape[0]
    rows = steps_per_chunk * SSM_CHAINS
    n_chunks = total // rows
    seg_len = total // SSM_CHAINS
    assert 2 ** int(math.log2(seg_len)) == seg_len
    const = lambda p, c: (0, 0)
    return pl.pallas_call(
        functools.partial(_ssm_prompt_kernel, n_seg=n_seg),
        grid=(2, n_chunks),
        in_specs=[pl.BlockSpec((rows, D_SSM), lambda p, c: (c, 0)),
                  pl.BlockSpec((1, 2 * SSM_LANES), const),
                  pl.BlockSpec((D_SSM, 2 * SSM_LANES), const),
                  pl.BlockSpec((2 * SSM_LANES, D_SSM), const),
                  pl.BlockSpec((1, D_SSM), const),
                  pl.BlockSpec((D_SSM, 2 * D_SSM), const),
                  pl.BlockSpec((1, 2 * D_SSM), const)],
        out_specs=(pl.BlockSpec((rows, D_SSM), lambda p, c: (p * c, 0)),
                   pl.BlockSpec((SSM_CHAINS, 2 * SSM_LANES), const)),
        out_shape=(jax.ShapeDtypeStruct((total, D_SSM), F32),
                   jax.ShapeDtypeStruct((SSM_CHAINS, 2 * SSM_LANES), F32)),
        scratch_shapes=[pltpu.VMEM((rows, 2 * SSM_LANES), F32),
                        pltpu.VMEM((SSM_CHAINS, 2 * SSM_LANES), F32),
                        pltpu.VMEM((SSM_CHAINS, 2 * SSM_LANES), F32)],
        compiler_params=pltpu.CompilerParams(dimension_semantics=("arbitrary", "arbitrary"),
                                             vmem_limit_bytes=40 * 2**20),
        name="ssm_prompt",
    )(u_chain, a_row, b_bf, c_bf, d_row, wg_bf, bg_row)


def _ssm_sample_kernel(u_ref, h0_ref, a_ref, b_ref, c_ref, d_ref, wg_ref, bg_ref, y_ref, hfin_ref):
    steps, batch, _ = u_ref.shape
    n = SSM_LANES
    a_re = jnp.broadcast_to(a_ref[:, :n], (batch, n))
    a_im = jnp.broadcast_to(a_ref[:, n:], (batch, n))
    h_re = h0_ref[:, :n]
    h_im = h0_ref[:, n:]
    for t in range(steps):
        u = u_ref[t]
        bu = jnp.dot(u, b_ref[...], precision=lax.Precision.HIGHEST, preferred_element_type=F32)
        h_re, h_im = _complex_step(a_re, a_im, h_re, h_im, bu[:, :n], bu[:, n:])
        h_all = jnp.concatenate([h_re, h_im], axis=1)
        y_ref[t] = _ssm_output(h_all, u, c_ref, d_ref, wg_ref, bg_ref)
    hfin_ref[:, :n] = h_re
    hfin_ref[:, n:] = h_im


def _ssm_sample(u_tb, h0, a_row, b_f32, c_bf, d_row, wg_bf, bg_row):
    steps, batch, _ = u_tb.shape
    return pl.pallas_call(
        _ssm_sample_kernel,
        out_shape=(jax.ShapeDtypeStruct((steps, batch, D_SSM), F32),
                   jax.ShapeDtypeStruct((batch, 2 * SSM_LANES), F32)),
        compiler_params=_vmem_limit(40 * 2**20),
        name="ssm_sample",
    )(u_tb, h0, a_row, b_f32, c_bf, d_row, wg_bf, bg_row)


def _mem_attn_prompt_kernel(q_ref, mk_ref, mv_ref, o_ref):
    rows = q_ref.shape[0]
    half = LANES // 2
    scale = (D_MEM // MEM_HEADS) ** -0.5
    head0 = lax.broadcasted_iota(jnp.int32, (rows, LANES), 1) < half
    outs = []
    for hp in range(D_MEM // LANES):
        cols = slice(hp * LANES, (hp + 1) * LANES)
        q2 = q_ref[:, cols].astype(F32)
        k2 = mk_ref[:, cols].astype(BF16)
        v2 = mv_ref[:, cols].astype(BF16)
        res = []
        for e in (0, 1):
            mine = head0 if e == 0 else jnp.logical_not(head0)
            lhs = jnp.where(mine, q2, 0.0).astype(BF16)
            s = lax.dot_general(lhs, k2, _NT, preferred_element_type=F32) * scale
            p = jnp.exp(s - jnp.max(s, axis=1, keepdims=True))
            l = jnp.sum(p, axis=1, keepdims=True)
            res.append(jnp.dot(p.astype(BF16), v2, preferred_element_type=F32) / l)
        outs.append(jnp.where(head0, res[0], res[1]))
    o_ref[...] = jnp.concatenate(outs, axis=1)


def _mem_attn_prompt(qm, mk, mv, batch, seq, tm):
    n_mem = mk.shape[0] // batch
    tiles = seq // tm
    return pl.pallas_call(
        _mem_attn_prompt_kernel,
        grid=(batch, tiles),
        in_specs=[pl.BlockSpec((tm, D_MEM), lambda b, i: (b * tiles + i, 0)),
                  pl.BlockSpec((n_mem, D_MEM), lambda b, i: (b, 0)),
                  pl.BlockSpec((n_mem, D_MEM), lambda b, i: (b, 0))],
        out_specs=pl.BlockSpec((tm, D_MEM), lambda b, i: (b * tiles + i, 0)),
        out_shape=jax.ShapeDtypeStruct((batch * seq, D_MEM), F32),
        name="mem_attn_prompt",
    )(qm, mk, mv)


def _head_rows(x, n_heads, head_dim):
    head_of_lane = lax.broadcasted_iota(jnp.int32, x.shape, 1) // head_dim
    return jnp.concatenate([jnp.where(head_of_lane == h, x, 0.0) for h in range(n_heads)], axis=0)


def _head_diagonal(o_full, t, n_heads, head_dim):
    head_of_lane = lax.broadcasted_iota(jnp.int32, (t, o_full.shape[1]), 1) // head_dim
    out = jnp.zeros((t, o_full.shape[1]), F32)
    for h in range(n_heads):
        out = out + jnp.where(head_of_lane == h, o_full[h * t:(h + 1) * t], 0.0)
    return out


def _mem_attn_sample_kernel(q_ref, mkt_ref, mvt_ref, o_ref):
    group, t, _ = q_ref.shape
    head_dim = D_MEM // MEM_HEADS
    scale = head_dim ** -0.5
    for g in range(group):
        qbd = _head_rows(q_ref[g].astype(F32) * scale, MEM_HEADS, head_dim).astype(BF16)
        s = jnp.dot(qbd, mkt_ref[g].astype(BF16), preferred_element_type=F32)
        p = jnp.exp(s - jnp.max(s, axis=1, keepdims=True))
        l = jnp.sum(p, axis=1, keepdims=True)
        o_full = lax.dot_general(p.astype(BF16), mvt_ref[g].astype(BF16), _NT, preferred_element_type=F32) / l
        o_ref[g] = _head_diagonal(o_full, t, MEM_HEADS, head_dim)


def _mem_attn_sample(qm, mkt, mvt, group):
    db, t, _ = qm.shape
    n_mem = mkt.shape[2]
    blk = lambda i: (i, 0, 0)
    return pl.pallas_call(
        _mem_attn_sample_kernel,
        grid=(db // group,),
        in_specs=[pl.BlockSpec((group, t, D_MEM), blk),
                  pl.BlockSpec((group, D_MEM, n_mem), blk),
                  pl.BlockSpec((group, D_MEM, n_mem), blk)],
        out_specs=pl.BlockSpec((group, t, D_MEM), blk),
        out_shape=jax.ShapeDtypeStruct((db, t, D_MEM), F32),
        name="mem_attn_sample",
    )(qm, mkt, mvt)


def _merge_kernel(x_ref, attn_ref, ssm_ref, mem_ref, gate_ref, w_ref, g_ref, b_ref, y_ref):
    mixed = jnp.concatenate([attn_ref[...], ssm_ref[...], mem_ref[...]], axis=1) * jax.nn.silu(gate_ref[...])
    out = jnp.dot(mixed.astype(BF16), w_ref[...], preferred_element_type=F32)
    z = DEEPNORM_ALPHA * x_ref[...] + out
    mu = jnp.mean(z, axis=1, keepdims=True)
    zc = z - mu
    var = jnp.mean(zc * zc, axis=1, keepdims=True)
    y_ref[...] = zc * lax.rsqrt(var + LN_EPS) * g_ref[...] + b_ref[...]


def _merge(x2d, attn, ssm, mem, gate, w_bf, g_row, b_row, tm, ssm_map):
    m = x2d.shape[0]
    row = lambda i: (i, 0)
    const = lambda i: (0, 0)
    return pl.pallas_call(
        _merge_kernel,
        grid=(m // tm,),
        in_specs=[pl.BlockSpec((tm, D_MODEL), row),
                  pl.BlockSpec((tm, D_ATTN), row),
                  pl.BlockSpec((tm, D_SSM), ssm_map),
                  pl.BlockSpec((tm, D_MEM), row),
                  pl.BlockSpec((tm, D_MIX), row),
                  pl.BlockSpec((D_MIX, D_MODEL), const),
                  pl.BlockSpec((1, D_MODEL), const),
                  pl.BlockSpec((1, D_MODEL), const)],
        out_specs=pl.BlockSpec((tm, D_MODEL), row),
        out_shape=jax.ShapeDtypeStruct((m, D_MODEL), F32),
        compiler_params=_vmem_limit(48 * 2**20),
        name="merge",
    )(x2d, attn, ssm, mem, gate, w_bf, g_row, b_row)


def _moba_sample_parts(b, n_seq, pt_ref, q_ref, kn_ref, vn_ref, ck_ref, cv_ref, o_ref,
                       buf_ref, sem_ref, s_ref, p_ref, qbd_ref, pown_ref, l_ref, acc_ref,
                       *, pages_per_chunk, chunks, page_size):
    ppc = pages_per_chunk
    items = 2 * chunks
    t = q_ref.shape[1]
    rows = N_HEADS_A * t
    n_pages = ppc * chunks
    pages_per_block = MOBA_BLOCK // page_size
    n_blocks = n_pages // pages_per_block

    def pad_rows(x):
        return jnp.concatenate([x, jnp.zeros((page_size - x.shape[0], x.shape[1]), x.dtype)], axis=0)

    def chunk_copies(seq, r, slot):
        src_ref, chunk = (ck_ref, r) if r < chunks else (cv_ref, r - chunks)
        return [pltpu.make_async_copy(src_ref.at[pt_ref[seq, chunk * ppc + i]], buf_ref.at[slot, i],
                                      sem_ref.at[slot]) for i in range(ppc)]

    def prologue():
        @pl.when(b == 0)
        def _first_chunk():
            for cp in chunk_copies(b, 0, 0):
                cp.start()
        qbd_ref[...] = _head_rows(q_ref[0] * (HEAD_DIM ** -0.5), N_HEADS_A, HEAD_DIM).astype(BF16)

    def key_chunk(r, slot):
        qbd_bf = qbd_ref[...]
        for i in range(ppc):
            s_ref[r * ppc + i] = jnp.dot(qbd_bf, buf_ref[slot, i].astype(BF16), preferred_element_type=F32)

    def select_and_softmax():
        qbd_bf = qbd_ref[...]
        lane = lax.broadcasted_iota(jnp.int32, (rows, page_size), 1)
        sc = jnp.zeros((rows, page_size), F32)
        for j in range(n_blocks):
            bs = s_ref[j * pages_per_block]
            for pg in range(1, pages_per_block):
                bs = bs + s_ref[j * pages_per_block + pg]
            sc = jnp.where(lane == j, jnp.sum(bs, axis=1, keepdims=True), sc)
        sel = _select_topk(sc, lane < n_blocks, lane.astype(F32), axis=1)
        lo = lax.dot_general(qbd_bf, pad_rows(kn_ref[0]).astype(BF16), _NT, preferred_element_type=F32)
        t_row = lax.broadcasted_iota(jnp.int32, (rows, page_size), 0) % t
        lo = jnp.where(lane <= t_row, lo, NEG_INF)
        mx = lo
        for j in range(n_blocks):
            bm = s_ref[j * pages_per_block]
            for pg in range(1, pages_per_block):
                bm = jnp.maximum(bm, s_ref[j * pages_per_block + pg])
            mx = jnp.maximum(mx, jnp.where(sel[:, j:j + 1], bm, NEG_INF))
        m = jnp.max(mx, axis=1, keepdims=True)
        po = jnp.exp(lo - m)
        lsum = po
        for j in range(n_blocks):
            sj = sel[:, j:j + 1]
            for pg in range(j * pages_per_block, (j + 1) * pages_per_block):
                p = jnp.where(sj, jnp.exp(s_ref[pg] - m), 0.0)
                lsum = lsum + p
                p_ref[pg] = p.astype(BF16)
        pown_ref[...] = po.astype(BF16)
        l_ref[...] = jnp.broadcast_to(jnp.sum(lsum, axis=1, keepdims=True), l_ref.shape)
        acc_ref[...] = jnp.zeros_like(acc_ref)

    def value_chunk(r, slot):
        acc = acc_ref[...]
        for i in range(ppc):
            acc = acc + lax.dot_general(p_ref[(r - chunks) * ppc + i], buf_ref[slot, i].astype(BF16), _NT,
                                        preferred_element_type=F32)
        acc_ref[...] = acc

    def item(r):
        slot = r % 2
        if r + 1 < items:
            for cp in chunk_copies(b, r + 1, 1 - slot):
                cp.start()
        else:
            @pl.when(b + 1 < n_seq)
            def _next_sequence():
                for cp in chunk_copies(b + 1, 0, 1 - slot):
                    cp.start()
        for cp in chunk_copies(b, r, slot):
            cp.wait()
        if r < chunks:
            key_chunk(r, slot)
        else:
            if r == chunks:
                select_and_softmax()
            value_chunk(r, slot)

    def finish():
        acc = acc_ref[...] + jnp.dot(pown_ref[...], pad_rows(vn_ref[0]).astype(BF16), preferred_element_type=F32)
        l = jnp.concatenate([l_ref[...]] * (D_ATTN // page_size), axis=1)
        o_ref[0] = _head_diagonal(acc / l, t, N_HEADS_A, HEAD_DIM)

    return prologue, item, finish


def _moba_sample_scratch(t, n_pages, pages_per_chunk, page_size):
    rows = N_HEADS_A * t
    return [pltpu.VMEM((2, pages_per_chunk, D_ATTN, page_size), F32),
            pltpu.SemaphoreType.DMA((2,)),
            pltpu.VMEM((n_pages, rows, page_size), F32),
            pltpu.VMEM((n_pages, rows, page_size), BF16),
            pltpu.VMEM((rows, D_ATTN), BF16),
            pltpu.VMEM((rows, page_size), BF16),
            pltpu.VMEM((rows, page_size), F32),
            pltpu.VMEM((rows, D_ATTN), F32)]


_N_PROMPT_SCRATCH = 8


def _moba_kernel(pt_ref, q_ref, k_ref, v_ref, kmean_ref, qs_ref, kn_ref, vn_ref, ck_ref, cv_ref,
                 o_ref, os_ref, *scratch, n_tiles, tiles_per_head, n_seq, pages_per_chunk, chunks, page_size):
    n = pl.program_id(0)
    both = n_tiles == n_seq
    has_tile = None if both else n < n_tiles
    has_seq = None if both else n < n_seq

    def when(cond):
        return (lambda f: f()) if cond is None else pl.when(cond)

    ti = lax.rem(n if both else jnp.minimum(n, n_tiles - 1), tiles_per_head)
    p_setup, p_pairs, p_finish = _moba_prompt_parts(
        ti, q_ref, k_ref, v_ref, kmean_ref, o_ref, *scratch[:_N_PROMPT_SCRATCH])
    s_prologue, s_item, s_finish = _moba_sample_parts(
        n, n_seq, pt_ref, qs_ref, kn_ref, vn_ref, ck_ref, cv_ref, os_ref, *scratch[_N_PROMPT_SCRATCH:],
        pages_per_chunk=pages_per_chunk, chunks=chunks, page_size=page_size)

    when(has_tile)(p_setup)
    when(has_seq)(s_prologue)
    pairs = ti * (q_ref.shape[1] // MOBA_BLOCK) // 2
    if not both:
        pairs = jnp.where(has_tile, pairs, 0)
    items = 2 * chunks
    for r in range(items):
        when(has_seq)(functools.partial(s_item, r))
        p_pairs(r * pairs // items, (r + 1) * pairs // items)
    when(has_tile)(p_finish)
    when(has_seq)(s_finish)


def _moba(q, kb, vb, kmean, q_s, k_new, v_new, cache_k, cache_v, page_table, pages_per_chunk):
    b, s, _ = q.shape
    nb = s // MOBA_BLOCK
    tq = _MOBA_Q_BLOCKS * MOBA_BLOCK
    assert nb <= LANES // 2 and nb % SUBLANES == 0 and s % tq == 0 and _MOBA_Q_BLOCKS % 2 == 0
    head_pairs = D_ATTN // LANES
    tiles_per_head = s // tq
    n_tiles = b * head_pairs * tiles_per_head

    db, t, _ = q_s.shape
    n_pages = page_table.shape[1]
    page_size = cache_k.shape[2]
    chunks = n_pages // pages_per_chunk
    assert (n_pages * page_size) % MOBA_BLOCK == 0 and MOBA_BLOCK % page_size == 0
    assert n_pages * page_size // MOBA_BLOCK >= MOBA_TOPK and t <= page_size and page_size == LANES
    assert n_pages % pages_per_chunk == 0

    def tile_of(n):
        n = jnp.minimum(n, n_tiles - 1)
        return n // (head_pairs * tiles_per_head), (n // tiles_per_head) % head_pairs, n % tiles_per_head

    def tile_map(n, pt):
        bi, hp, ti = tile_of(n)
        return (bi, ti, hp)

    def head_map(n, pt):
        bi, hp, _ = tile_of(n)
        return (bi, 0, hp)

    def seq_map(n, pt):
        return (jnp.minimum(n, db - 1), 0, 0)

    grid_spec = pltpu.PrefetchScalarGridSpec(
        num_scalar_prefetch=1,
        grid=(max(n_tiles, db),),
        in_specs=[pl.BlockSpec((1, tq, LANES), tile_map),
                  pl.BlockSpec((1, s, LANES), head_map),
                  pl.BlockSpec((1, s, LANES), head_map),
                  pl.BlockSpec((1, nb, LANES), head_map)]
                 + [pl.BlockSpec((1, t, D_ATTN), seq_map)] * 3
                 + [pl.BlockSpec(memory_space=pl.ANY)] * 2,
        out_specs=(pl.BlockSpec((1, tq, LANES), tile_map), pl.BlockSpec((1, t, D_ATTN), seq_map)),
        scratch_shapes=_moba_prompt_scratch(s, tq) + _moba_sample_scratch(t, n_pages, pages_per_chunk, page_size))
    return pl.pallas_call(
        functools.partial(_moba_kernel, n_tiles=n_tiles, tiles_per_head=tiles_per_head, n_seq=db,
                          pages_per_chunk=pages_per_chunk, chunks=chunks, page_size=page_size),
        grid_spec=grid_spec,
        out_shape=(jax.ShapeDtypeStruct((b, s, D_ATTN), F32), jax.ShapeDtypeStruct((db, t, D_ATTN), F32)),
        compiler_params=pltpu.CompilerParams(dimension_semantics=("arbitrary",),
                                             vmem_limit_bytes=52 * 2**20),
        name="moba",
    )(page_table, q, kb, vb, kmean, q_s, k_new, v_new, cache_k, cache_v)


def kernel(x_prompt, x_sample, mem_prompt, cache_k, cache_v, state_ssm_re, state_ssm_im,
           cache_mem_k, cache_mem_v, page_table, w_in, w_mem_kv, a_re, a_im, log_dt,
           b_re, b_im, c_re, c_im, d_skip, w_glu, b_glu, w_out, ln_g, ln_b):
    batch, seq, _ = x_prompt.shape
    db, t_new, _ = x_sample.shape
    n_mem = mem_prompt.shape[1]
    n_phys, page_size = cache_k.shape[:2]
    n_seg = SSM_CHAINS // batch
    seg_len = seq // n_seg
    tm = min(512, seg_len)

    w_in_bf = w_in.astype(BF16)
    w_mem_bf = w_mem_kv.astype(BF16)
    w_out_bf = w_out.astype(BF16)
    wg_bf = w_glu.astype(BF16)
    bg_row = b_glu.astype(F32).reshape(1, 2 * D_SSM)
    d_row = d_skip.astype(F32).reshape(1, D_SSM)
    g_row = ln_g.astype(F32).reshape(1, D_MODEL)
    b_row = ln_b.astype(F32).reshape(1, D_MODEL)
    a_row, b_mat, c_mat = _ssm_operands(a_re, a_im, log_dt, b_re, b_im, c_re, c_im)
    b_bf = b_mat.astype(BF16)
    c_bf = c_mat.astype(BF16)

    x2d = x_prompt.reshape(batch * seq, D_MODEL)
    q, k, v, u_chain, qm, gate, kb, vb, kmean = _in_proj_prompt(x2d, w_in_bf, batch, seq, tm)
    xs2d = x_sample.reshape(db * t_new, D_MODEL)
    tms = min(512, db * t_new)
    q_s, k_s, v_s, u_s, qm_s, gate_s = _in_proj_sample(xs2d, w_in_bf, tms)
    attn, attn_s = _moba(q.reshape(batch, seq, D_ATTN), kb.reshape(batch, seq, D_ATTN),
                         vb.reshape(batch, seq, D_ATTN), kmean.reshape(batch, seq // MOBA_BLOCK, D_ATTN),
                         q_s.reshape(db, t_new, D_ATTN), k_s.reshape(db, t_new, D_ATTN),
                         v_s.reshape(db, t_new, D_ATTN),
                         cache_k.transpose(0, 2, 3, 1).reshape(n_phys, D_ATTN, page_size),
                         cache_v.transpose(0, 2, 3, 1).reshape(n_phys, D_ATTN, page_size), page_table,
                         pages_per_chunk=min(16, page_table.shape[1] // 2))

    y_chain, hfin = _ssm_prompt(u_chain.reshape(seg_len * SSM_CHAINS, D_SSM), a_row, b_bf, c_bf, d_row,
                                wg_bf, bg_row, n_seg, steps_per_chunk=min(64, seg_len))
    mk, mv = _mem_kv(mem_prompt.reshape(batch * n_mem, D_MODEL), w_mem_bf)
    mem_o = _mem_attn_prompt(qm, mk, mv, batch, seq, tm)
    tiles_per_seq = seq // tm
    tiles_per_seg = seg_len // tm

    def chain_map(i):
        b = i // tiles_per_seq
        ti = i % tiles_per_seq
        return (ti % tiles_per_seg, b * n_seg + ti // tiles_per_seg)

    y_prompt = _merge(x2d, attn.reshape(batch * seq, D_ATTN), y_chain.reshape(seg_len, SSM_CHAINS * D_SSM),
                      mem_o, gate, w_out_bf, g_row, b_row, tm, chain_map)
    h_last = hfin[n_seg - 1::n_seg]
    ssm_re_p = h_last[:, :SSM_LANES].reshape(batch, SSM_GROUPS, SSM_STATE)
    ssm_im_p = h_last[:, SSM_LANES:].reshape(batch, SSM_GROUPS, SSM_STATE)

    h0 =jnp.concatenate([state_ssm_re.astype(F32).reshape(db, SSM_LANES),
                          state_ssm_im.astype(F32).reshape(db, SSM_LANES)], axis=1)
    y_tb, hfin_s = _ssm_sample(u_s.reshape(db, t_new, D_SSM).transpose(1, 0, 2), h0, a_row, b_mat, c_bf,
                               d_row, wg_bf, bg_row)
    ssm_s = y_tb.transpose(1, 0, 2).reshape(db * t_new, D_SSM)
    mem_s = _mem_attn_sample(qm_s.reshape(db, t_new, D_MEM),
                             cache_mem_k.transpose(0, 2, 3, 1).reshape(db, D_MEM, n_mem),
                             cache_mem_v.transpose(0, 2, 3, 1).reshape(db, D_MEM, n_mem), group=min(8, db))
    y_sample = _merge(xs2d, attn_s.reshape(db * t_new, D_ATTN), ssm_s, mem_s.reshape(db * t_new, D_MEM),
                      gate_s, w_out_bf, g_row, b_row, tms, lambda i: (i, 0))

    heads = (N_HEADS_A, HEAD_DIM)
    mem_heads = (MEM_HEADS, D_MEM // MEM_HEADS)
    return (y_prompt.reshape(batch, seq, D_MODEL), y_sample.reshape(db, t_new, D_MODEL),
            k.reshape(batch, seq, *heads), v.reshape(batch, seq, *heads), ssm_re_p, ssm_im_p,
            mk.reshape(batch, n_mem, *mem_heads), mv.reshape(batch, n_mem, *mem_heads),
            k_s.reshape(db, t_new, *heads), v_s.reshape(db, t_new, *heads),
            hfin_s[:, :SSM_LANES].reshape(db, SSM_GROUPS, SSM_STATE),
            hfin_s[:, SSM_LANES:].reshape(db, SSM_GROUPS, SSM_STATE))
```

```python
import functools
import math

import jax
import jax.numpy as jnp
from jax import lax
from jax.experimental import pallas as pl
from jax.experimental.pallas import tpu as pltpu

F32 = jnp.float32
BF16 = jnp.bfloat16

D_MODEL = 1024
D_ATTN = 512
D_SSM = 256
D_MEM = 256
D_MIX = D_ATTN + D_SSM + D_MEM
HEAD_DIM = 64
N_HEADS_A = D_ATTN // HEAD_DIM
MOBA_BLOCK = 256
MOBA_TOPK = 3
SSM_CH = 16
SSM_GROUPS = D_SSM // SSM_CH
SSM_STATE = 64
SSM_LANES = SSM_GROUPS * SSM_STATE
MEM_HEADS = 4
D_IN = 3 * D_ATTN + D_SSM + D_MEM + D_MIX
DEPTH = 1
DEEPNORM_ALPHA = (2.0 * DEPTH) ** 0.25
LN_EPS = 1e-5
NEG_INF = -1e30

LANES = 128
SUBLANES = 8
SSM_CHAINS = SUBLANES

_NT = (((1,), (1,)), ((), ()))


def _vmem_limit(nbytes):
    return pltpu.CompilerParams(vmem_limit_bytes=int(nbytes))


def _proj_kernel(x_ref, w_ref, *out_refs, with_prompt_extras):
    xb = x_ref[...].astype(BF16)

    def seg(lo, hi):
        return jnp.dot(xb, w_ref[:, lo:hi], preferred_element_type=F32)

    c0, c1, c2, c3, c4 = D_ATTN, 2 * D_ATTN, 3 * D_ATTN, 3 * D_ATTN + D_SSM, 3 * D_ATTN + D_SSM + D_MEM
    if with_prompt_extras:
        q_ref, k_ref, v_ref, u_ref, qm_ref, gate_ref, kb_ref, vb_ref, kmean_ref = out_refs
    else:
        q_ref, k_ref, v_ref, u_ref, qm_ref, gate_ref = out_refs
    q_ref[...] = seg(0, c0)
    k = seg(c0, c1)
    v = seg(c1, c2)
    u_ref[...] = seg(c2, c3)
    qm_ref[...] = seg(c3, c4).astype(qm_ref.dtype)
    gate_ref[...] = seg(c4, D_IN).astype(gate_ref.dtype)
    if not with_prompt_extras:
        k_ref[...] = k
        v_ref[...] = v
    else:
        k_ref[0] = k.T
        v_ref[0] = v.T
        kb_ref[...] = k.astype(BF16)
        vb_ref[...] = v.astype(BF16)
        tm = k.shape[0]
        kmean_ref[0] = jnp.sum(k.reshape(tm // MOBA_BLOCK, MOBA_BLOCK, D_ATTN), axis=1) * (1.0 / MOBA_BLOCK)


def _in_proj_prompt(x2d, w_bf, batch, seq, tm):
    m = x2d.shape[0]
    n_seg = SSM_CHAINS // batch
    seg_len = seq // n_seg
    tiles_per_seq = seq // tm
    tiles_per_seg = seg_len // tm
    row = lambda i: (i, 0)

    def u_map(i):
        b = i // tiles_per_seq
        t = i % tiles_per_seq
        return (t % tiles_per_seg, b * n_seg + t // tiles_per_seg)

    out_shape = (
        jax.ShapeDtypeStruct((m, D_ATTN), F32),
        jax.ShapeDtypeStruct((batch, D_ATTN, seq), F32),
        jax.ShapeDtypeStruct((batch, D_ATTN, seq), F32),
        jax.ShapeDtypeStruct((seg_len, SSM_CHAINS * D_SSM), F32),
        jax.ShapeDtypeStruct((m, D_MEM), BF16),
        jax.ShapeDtypeStruct((m, D_MIX), BF16),
        jax.ShapeDtypeStruct((m, D_ATTN), BF16),
        jax.ShapeDtypeStruct((m, D_ATTN), BF16),
        jax.ShapeDtypeStruct((m // tm, tm // MOBA_BLOCK, D_ATTN), F32),
    )
    kv_t = lambda i: (i // tiles_per_seq, 0, i % tiles_per_seq)
    out_specs = (
        pl.BlockSpec((tm, D_ATTN), row),
        pl.BlockSpec((1, D_ATTN, tm), kv_t),
        pl.BlockSpec((1, D_ATTN, tm), kv_t),
        pl.BlockSpec((tm, D_SSM), u_map),
        pl.BlockSpec((tm, D_MEM), row),
        pl.BlockSpec((tm, D_MIX), row),
        pl.BlockSpec((tm, D_ATTN), row),
        pl.BlockSpec((tm, D_ATTN), row),
        pl.BlockSpec((1, tm // MOBA_BLOCK, D_ATTN), lambda i: (i, 0, 0)),
    )
    return pl.pallas_call(
        functools.partial(_proj_kernel, with_prompt_extras=True),
        grid=(m // tm,),
        in_specs=[pl.BlockSpec((tm, D_MODEL), row),
                  pl.BlockSpec((D_MODEL, D_IN), lambda i: (0, 0), pipeline_mode=pl.Buffered(1))],
        out_specs=out_specs,
        out_shape=out_shape,
        compiler_params=_vmem_limit(52 * 2**20),
        name="in_proj_prompt",
    )(x2d, w_bf)


def _in_proj_sample(x2d, w_bf, tm):
    m = x2d.shape[0]
    row = lambda i: (i, 0)
    out_shape = (
        jax.ShapeDtypeStruct((m, D_ATTN), F32),
        jax.ShapeDtypeStruct((m, D_ATTN), F32),
        jax.ShapeDtypeStruct((m, D_ATTN), F32),
        jax.ShapeDtypeStruct((m, D_SSM), F32),
        jax.ShapeDtypeStruct((m, D_MEM), BF16),
        jax.ShapeDtypeStruct((m, D_MIX), BF16),
    )
    widths = (D_ATTN, D_ATTN, D_ATTN, D_SSM, D_MEM, D_MIX)
    return pl.pallas_call(
        functools.partial(_proj_kernel, with_prompt_extras=False),
        grid=(m // tm,),
        in_specs=[pl.BlockSpec((tm, D_MODEL), row),
                  pl.BlockSpec((D_MODEL, D_IN), lambda i: (0, 0), pipeline_mode=pl.Buffered(1))],
        out_specs=tuple(pl.BlockSpec((tm, w), row) for w in widths),
        out_shape=out_shape,
        compiler_params=_vmem_limit(52 * 2**20),
        name="in_proj_sample",
    )(x2d, w_bf)


def _mem_kv_kernel(mem_ref, w_ref, mk_ref, mv_ref):
    kv = jnp.dot(mem_ref[...].astype(BF16), w_ref[...], preferred_element_type=F32)
    mk_ref[...] = kv[:, :D_MEM]
    mv_ref[...] = kv[:, D_MEM:]


def _mem_kv(mem2d, w_bf):
    m = mem2d.shape[0]
    tm = min(m, 512)
    row = lambda i: (i, 0)
    return pl.pallas_call(
        _mem_kv_kernel,
        grid=(m // tm,),
        in_specs=[pl.BlockSpec((tm, D_MODEL), row),
                  pl.BlockSpec((D_MODEL, 2 * D_MEM), lambda i: (0, 0))],
        out_specs=(pl.BlockSpec((tm, D_MEM), row), pl.BlockSpec((tm, D_MEM), row)),
        out_shape=(jax.ShapeDtypeStruct((m, D_MEM), F32), jax.ShapeDtypeStruct((m, D_MEM), F32)),
        name="mem_kv",
    )(mem2d, w_bf)


def _select_topk(scores, valid, index, axis):
    lowest = jnp.finfo(F32).min
    s = jnp.where(valid, scores, lowest)
    sel = jnp.zeros(scores.shape, dtype=jnp.bool_)
    for _ in range(MOBA_TOPK):
        m = jnp.max(s, axis=axis, keepdims=True)
        first = jnp.min(jnp.where(s == m, index, jnp.float32(2**30)), axis=axis, keepdims=True)
        pick = index == first
        sel = sel | (pick & valid)
        s = jnp.where(pick, lowest, s)
    return sel


_MOBA_Q_BLOCKS = 2
_MOBA_ROW_GROUP = 128


def _moba_prompt_parts(ti, q_ref, k_ref, v_ref, kmean_ref, o_ref,
                       kaug_ref, vaug_ref, lhs_ref, s_ref, p_ref, alpha_ref, m_ref, acc_ref):
    blk = MOBA_BLOCK
    tq = q_ref.shape[1]
    qb = tq // blk
    nb = k_ref.shape[1] // blk
    half = LANES // 2

    def build_operands():
        lane_b = lax.broadcasted_iota(jnp.int32, (blk, LANES), 1)
        head0_b = lane_b < half

        def body(j, carry):
            rows = pl.ds(pl.multiple_of(j * blk, blk), blk)
            kb = k_ref[0, rows, :].astype(F32)
            vb = v_ref[0, rows, :].astype(F32)
            hot0 = jnp.where(lane_b == half + j, 1.0, 0.0)
            hot1 = jnp.where(lane_b == j, 1.0, 0.0)
            kaug_ref[0, rows, :] = jnp.where(head0_b, kb, hot0).astype(BF16)
            kaug_ref[1, rows, :] = jnp.where(head0_b, hot1, kb).astype(BF16)
            vaug_ref[0, rows, :] = jnp.where(head0_b, vb, 1.0).astype(BF16)
            vaug_ref[1, rows, :] = jnp.where(head0_b, 1.0, vb).astype(BF16)
            return carry
        lax.fori_loop(0, nb, body, 0)

    def select_blocks():
        q2 = q_ref[0]
        means = kmean_ref[0]
        lane_m = lax.broadcasted_iota(jnp.int32, (nb, LANES), 1)
        pad = [jnp.zeros((half - nb, LANES), F32)] if nb < half else []
        means_p = jnp.concatenate([jnp.where(lane_m >= half, means, 0.0)] + pad
                                  + [jnp.where(lane_m < half, means, 0.0)] + pad, axis=0)
        sc_t = lax.dot_general(means_p, q2, _NT, precision=lax.Precision.HIGHEST,
                               preferred_element_type=F32)
        slot = lax.broadcasted_iota(jnp.int32, (nb, tq), 0)
        slot_f = slot.astype(F32)
        n_full = ti * qb + lax.broadcasted_iota(jnp.int32, (nb, tq), 1) // blk
        bias_t = []
        for off in (0, half):
            sel = _select_topk(sc_t[off:off + nb], slot < n_full, slot_f, axis=0)
            bias_t.append(jnp.where(sel | (slot == n_full), 0.0, NEG_INF))
            if nb < half:
                bias_t.append(jnp.zeros((half - nb, tq), F32))
        bias = jnp.concatenate(bias_t, axis=0).T
        head0_q = lax.broadcasted_iota(jnp.int32, (tq, LANES), 1) < half
        qs = q2 * (HEAD_DIM ** -0.5)
        lhs_ref[0] = jnp.where(head0_q, qs, bias).astype(BF16)
        lhs_ref[1] = jnp.where(head0_q, bias, qs).astype(BF16)
        m_ref[...] = jnp.full(m_ref.shape, NEG_INF, F32)
        acc_ref[...] = jnp.zeros_like(acc_ref)

    rg = _MOBA_ROW_GROUP
    c_i = lax.broadcasted_iota(jnp.int32, (rg, blk), 1)
    r_i = lax.broadcasted_iota(jnp.int32, (rg, blk), 0)

    def key_rows(j):
        return pl.ds(j * blk if isinstance(j, int) else pl.multiple_of(j * blk, blk), blk)

    def score_block(j, slot):
        for e in (0, 1):
            s_ref[slot, e] = lax.dot_general(lhs_ref[e], kaug_ref[e, key_rows(j), :], _NT,
                                             preferred_element_type=F32)

    def softmax_pv_block(j, slot, causal):
        for e in (0, 1):
            for g in range(tq // rg):
                rows = pl.ds(g * rg, rg)
                s = s_ref[slot, e, rows, :]
                if causal:
                    s = jnp.where(j * blk + c_i <= ti * tq + g * rg + r_i, s, NEG_INF)
                m_old = m_ref[e, rows, :]
                m_new = jnp.maximum(m_old, jnp.max(s, axis=1, keepdims=True))
                alpha_ref[e, rows, :] = jnp.exp(m_old - m_new)
                p_ref[e, rows, :] = jnp.exp(s - jnp.concatenate([m_new] * (blk // LANES), axis=1)).astype(BF16)
                m_ref[e, rows, :] = m_new
            pv = jnp.dot(p_ref[e], vaug_ref[e, key_rows(j), :], preferred_element_type=F32)
            acc_ref[e] = alpha_ref[e] * acc_ref[e] + pv

    n_past = ti * qb

    def setup():
        pl.when(ti == 0)(build_operands)
        select_blocks()
        score_block(0, 0)

    def past_pair(jj, carry):
        j = 2 * jj
        score_block(j + 1, 1)
        softmax_pv_block(j, 0, causal=False)
        score_block(j + 2, 0)
        softmax_pv_block(j + 1, 1, causal=False)
        return carry

    def past_pairs(lo, hi):
        lax.fori_loop(lo, hi, past_pair, 0)

    def finish():
        for d in range(qb):
            if d + 1 < qb:
                score_block(n_past + d + 1, (d + 1) % 2)
            softmax_pv_block(n_past + d, d % 2, causal=True)
        a0 = acc_ref[0]
        a1 = acc_ref[1]
        o0 = a0 / pltpu.roll(a0, half, axis=1)
        o1 = a1 / pltpu.roll(a1, half, axis=1)
        head0_q = lax.broadcasted_iota(jnp.int32, (tq, LANES), 1) < half
        o_ref[0] = jnp.where(head0_q, o0, o1).astype(o_ref.dtype)

    return setup, past_pairs, finish


def _moba_prompt_scratch(s, tq):
    return [pltpu.VMEM((2, s, LANES), BF16),
            pltpu.VMEM((2, s, LANES), BF16),
            pltpu.VMEM((2, tq, LANES), BF16),
            pltpu.VMEM((2, 2, tq, MOBA_BLOCK), F32),
            pltpu.VMEM((2, tq, MOBA_BLOCK), BF16),
            pltpu.VMEM((2, tq, LANES), F32),
            pltpu.VMEM((2, tq, LANES), F32),
            pltpu.VMEM((2, tq, LANES), F32)]


def _ssm_disc_kernel(a_re_ref, a_im_ref, log_dt_ref, bt_re_ref, bt_im_ref,
                     abar_re_ref, abar_im_ref, bbt_re_ref, bbt_im_ref):
    a_re = a_re_ref[...]
    a_im = a_im_ref[...]
    dt = jnp.exp(log_dt_ref[...])
    mag = jnp.exp(a_re * dt)
    ang = a_im * dt
    abar_re = mag * jnp.cos(ang)
    abar_im = mag * jnp.sin(ang)
    den = a_re * a_re + a_im * a_im
    f_re = ((abar_re - 1.0) * a_re + abar_im * a_im) / den
    f_im = (abar_im * a_re - (abar_re - 1.0) * a_im) / den
    abar_re_ref[...] = abar_re
    abar_im_ref[...] = abar_im
    bt_re = bt_re_ref[...]
    bt_im = bt_im_ref[...]
    bbt_re_ref[...] = f_re[:, None, :] * bt_re - f_im[:, None, :] * bt_im
    bbt_im_ref[...] = f_re[:, None, :] * bt_im + f_im[:, None, :] * bt_re


def _ssm_operands(a_re, a_im, log_dt, b_re, b_im, c_re, c_im):
    g, p, c = b_re.shape
    abar_re, abar_im, bbt_re, bbt_im = pl.pallas_call(
        _ssm_disc_kernel,
        out_shape=(jax.ShapeDtypeStruct((g, p), F32), jax.ShapeDtypeStruct((g, p), F32),
                   jax.ShapeDtypeStruct((g, c, p), F32), jax.ShapeDtypeStruct((g, c, p), F32)),
        name="ssm_discretize",
    )(a_re.astype(F32), a_im.astype(F32), log_dt.astype(F32).reshape(g, 1),
      b_re.astype(F32).transpose(0, 2, 1), b_im.astype(F32).transpose(0, 2, 1))
    eye = jnp.eye(g, dtype=F32)

    def in_block_diag(bbt):
        return jnp.einsum('gcp,gh->gchp', bbt, eye).reshape(g * c, g * p)

    def out_block_diag(cm):
        return jnp.einsum('gcp,gh->gphc', cm, eye).reshape(g * p, g * c)

    b_mat = jnp.concatenate([in_block_diag(bbt_re), in_block_diag(bbt_im)], axis=1)
    c_mat = jnp.concatenate([out_block_diag(c_re.astype(F32)),
                             -out_block_diag(c_im.astype(F32))], axis=0)
    a_row = jnp.concatenate([abar_re.reshape(1, g * p), abar_im.reshape(1, g * p)], axis=1)
    return a_row, b_mat, c_mat


def _complex_step(a_re, a_im, h_re, h_im, bu_re, bu_im):
    return (a_re * h_re - a_im * h_im + bu_re, a_re * h_im + a_im * h_re + bu_im)


def _ssm_output(h_all, u, c_ref, d_ref, wg_ref, bg_ref):
    y = jnp.dot(h_all.astype(BF16), c_ref[...], preferred_element_type=F32) + d_ref[...] * u
    y = jax.nn.gelu(y)
    z = jnp.dot(y.astype(BF16), wg_ref[...], preferred_element_type=F32) + bg_ref[...]
    return z[:, :D_SSM] * jax.nn.sigmoid(z[:, D_SSM:])


def _ssm_prompt_kernel(u_ref, a_ref, b_ref, c_ref, d_ref, wg_ref, bg_ref, y_ref, hfin_ref,
                       hb_ref, h_ref, init_ref, *, n_seg):
    pss = pl.program_id(0)
    c = pl.program_id(1)
    n_chunks = pl.num_programs(1)
    rows = u_ref.shape[0]
    steps = rows // SSM_CHAINS
    n = SSM_LANES
    a_re = jnp.broadcast_to(a_ref[:, :n], (SSM_CHAINS, n))
    a_im = jnp.broadcast_to(a_ref[:, n:], (SSM_CHAINS, n))

    @pl.when((pss == 0) & (c == 0))
    def _zero_state():
        h_ref[...] = jnp.zeros_like(h_ref)

    @pl.when((pss == 1) & (c == 0))
    def _true_initial_state():
        h_ref[...] = init_ref[...]

    u = u_ref[...]
    hb_ref[...] = jnp.dot(u.astype(BF16), b_ref[...], preferred_element_type=F32)

    def scan(store):
        def step(s, carry):
            h_re, h_im = carry
            r = pl.ds(pl.multiple_of(s * SSM_CHAINS, SSM_CHAINS), SSM_CHAINS)
            h_re, h_im = _complex_step(a_re, a_im, h_re, h_im, hb_ref[r, :n], hb_ref[r, n:])
            if store:
                hb_ref[r, :n] = h_re
                hb_ref[r, n:] = h_im
            return h_re, h_im
        h_re, h_im = lax.fori_loop(0, steps, step, (h_ref[:, :n], h_ref[:, n:]))
        h_ref[:, :n] = h_re
        h_ref[:, n:] = h_im

    @pl.when(pss == 0)
    def _local_pass():
        scan(False)

    @pl.when((pss == 0) & (c == n_chunks - 1))
    def _segment_initial_states():
        seg_len = steps * n_chunks
        p_re, p_im = a_re, a_im
        for _ in range(int(math.log2(seg_len))):
            p_re, p_im = p_re * p_re - p_im * p_im, 2.0 * p_re * p_im
        e_re = h_ref[:, :n]
        e_im = h_ref[:, n:]
        first = (lax.broadcasted_iota(jnp.int32, (SSM_CHAINS, n), 0) % n_seg) == 0
        i_re = jnp.zeros((SSM_CHAINS, n), F32)
        i_im = jnp.zeros((SSM_CHAINS, n), F32)
        for _ in range(n_seg - 1):
            t_re = p_re * i_re - p_im * i_im + e_re
            t_im = p_re * i_im + p_im * i_re + e_im
            i_re = jnp.where(first, 0.0, pltpu.roll(t_re, 1, axis=0))
            i_im = jnp.where(first, 0.0, pltpu.roll(t_im, 1, axis=0))
        init_ref[:, :n] = i_re
        init_ref[:, n:] = i_im

    @pl.when(pss == 1)
    def _output_pass():
        scan(True)
        y_ref[...] = _ssm_output(hb_ref[...], u, c_ref, d_ref, wg_ref, bg_ref).astype(y_ref.dtype)

    @pl.when((pss == 1) & (c == n_chunks - 1))
    def _final_state():
        hfin_ref[...] = h_ref[...]


def _ssm_prompt(u_chain, a_row, b_bf, c_bf, d_row, wg_bf, bg_row, n_seg, steps_per_chunk):
    total = u_chain.shape[0]
    rows = steps_per_chunk * SSM_CHAINS
    n_chunks = total // rows
    seg_len = total // SSM_CHAINS
    assert 2 ** int(math.log2(seg_len)) == seg_len
    const = lambda p, c: (0, 0)
    return pl.pallas_call(
        functools.partial(_ssm_prompt_kernel, n_seg=n_seg),
        grid=(2, n_chunks),
        in_specs=[pl.BlockSpec((rows, D_SSM), lambda p, c: (c, 0)),
                  pl.BlockSpec((1, 2 * SSM_LANES), const),
                  pl.BlockSpec((D_SSM, 2 * SSM_LANES), const),
                  pl.BlockSpec((2 * SSM_LANES, D_SSM), const),
                  pl.BlockSpec((1, D_SSM), const),
                  pl.BlockSpec((D_SSM, 2 * D_SSM), const),
                  pl.BlockSpec((1, 2 * D_SSM), const)],
        out_specs=(pl.BlockSpec((rows, D_SSM), lambda p, c: (p * c, 0)),
                   pl.BlockSpec((SSM_CHAINS, 2 * SSM_LANES), const)),
        out_shape=(jax.ShapeDtypeStruct((total, D_SSM), BF16),
                   jax.ShapeDtypeStruct((SSM_CHAINS, 2 * SSM_LANES), F32)),
        scratch_shapes=[pltpu.VMEM((rows, 2 * SSM_LANES), F32),
                        pltpu.VMEM((SSM_CHAINS, 2 * SSM_LANES), F32),
                        pltpu.VMEM((SSM_CHAINS, 2 * SSM_LANES), F32)],
        compiler_params=pltpu.CompilerParams(dimension_semantics=("arbitrary", "arbitrary"),
                                             vmem_limit_bytes=40 * 2**20),
        name="ssm_prompt",
    )(u_chain, a_row, b_bf, c_bf, d_row, wg_bf, bg_row)


def _ssm_sample_kernel(u_ref, h0_ref, a_ref, b_ref, c_ref, d_ref, wg_ref, bg_ref, y_ref, hfin_ref):
    steps, batch, _ = u_ref.shape
    n = SSM_LANES
    a_re = jnp.broadcast_to(a_ref[:, :n], (batch, n))
    a_im = jnp.broadcast_to(a_ref[:, n:], (batch, n))
    h_re = h0_ref[:, :n]
    h_im = h0_ref[:, n:]
    for t in range(steps):
        u = u_ref[t]
        bu = jnp.dot(u, b_ref[...], precision=lax.Precision.HIGHEST, preferred_element_type=F32)
        h_re, h_im = _complex_step(a_re, a_im, h_re, h_im, bu[:, :n], bu[:, n:])
        h_all = jnp.concatenate([h_re, h_im], axis=1)
        y_ref[t] = _ssm_output(h_all, u, c_ref, d_ref, wg_ref, bg_ref)
    hfin_ref[:, :n] = h_re
    hfin_ref[:, n:] = h_im


def _ssm_sample(u_tb, h0, a_row, b_f32, c_bf, d_row, wg_bf, bg_row):
    steps, batch, _ = u_tb.shape
    return pl.pallas_call(
        _ssm_sample_kernel,
        out_shape=(jax.ShapeDtypeStruct((steps, batch, D_SSM), F32),
                   jax.ShapeDtypeStruct((batch, 2 * SSM_LANES), F32)),
        compiler_params=_vmem_limit(40 * 2**20),
        name="ssm_sample",
    )(u_tb, h0, a_row, b_f32, c_bf, d_row, wg_bf, bg_row)


def _mem_attn_prompt_kernel(q_ref, mk_ref, mv_ref, o_ref):
    rows = q_ref.shape[0]
    half = LANES // 2
    scale = (D_MEM // MEM_HEADS) ** -0.5
    head0 = lax.broadcasted_iota(jnp.int32, (rows, LANES), 1) < half
    outs = []
    for hp in range(D_MEM // LANES):
        cols = slice(hp * LANES, (hp + 1) * LANES)
        q2 = q_ref[:, cols].astype(F32)
        k2 = mk_ref[:, cols].astype(BF16)
        v2 = mv_ref[:, cols].astype(BF16)
        res = []
        for e in (0, 1):
            mine = head0 if e == 0 else jnp.logical_not(head0)
            lhs = jnp.where(mine, q2, 0.0).astype(BF16)
            s = lax.dot_general(lhs, k2, _NT, preferred_element_type=F32) * scale
            p = jnp.exp(s - jnp.max(s, axis=1, keepdims=True))
            l = jnp.sum(p, axis=1, keepdims=True)
            res.append(jnp.dot(p.astype(BF16), v2, preferred_element_type=F32) / l)
        outs.append(jnp.where(head0, res[0], res[1]))
    o_ref[...] = jnp.concatenate(outs, axis=1).astype(o_ref.dtype)


def _mem_attn_prompt(qm, mk, mv, batch, seq, tm):
    n_mem = mk.shape[0] // batch
    tiles = seq // tm
    return pl.pallas_call(
        _mem_attn_prompt_kernel,
        grid=(batch, tiles),
        in_specs=[pl.BlockSpec((tm, D_MEM), lambda b, i: (b * tiles + i, 0)),
                  pl.BlockSpec((n_mem, D_MEM), lambda b, i: (b, 0)),
                  pl.BlockSpec((n_mem, D_MEM), lambda b, i: (b, 0))],
        out_specs=pl.BlockSpec((tm, D_MEM), lambda b, i: (b * tiles + i, 0)),
        out_shape=jax.ShapeDtypeStruct((batch * seq, D_MEM), BF16),
        name="mem_attn_prompt",
    )(qm, mk, mv)


def _head_rows(x, n_heads, head_dim):
    head_of_lane = lax.broadcasted_iota(jnp.int32, x.shape, 1) // head_dim
    return jnp.concatenate([jnp.where(head_of_lane == h, x, 0.0) for h in range(n_heads)], axis=0)


def _head_diagonal(o_full, t, n_heads, head_dim):
    head_of_lane = lax.broadcasted_iota(jnp.int32, (t, o_full.shape[1]), 1) // head_dim
    out = jnp.zeros((t, o_full.shape[1]), F32)
    for h in range(n_heads):
        out = out + jnp.where(head_of_lane == h, o_full[h * t:(h + 1) * t], 0.0)
    return out


def _mem_attn_sample_kernel(q_ref, mkt_ref, mvt_ref, o_ref):
    group, t, _ = q_ref.shape
    head_dim = D_MEM // MEM_HEADS
    scale = head_dim ** -0.5
    for g in range(group):
        qbd = _head_rows(q_ref[g].astype(F32) * scale, MEM_HEADS, head_dim).astype(BF16)
        s = jnp.dot(qbd, mkt_ref[g].astype(BF16), preferred_element_type=F32)
        p = jnp.exp(s - jnp.max(s, axis=1, keepdims=True))
        l = jnp.sum(p, axis=1, keepdims=True)
        o_full = lax.dot_general(p.astype(BF16), mvt_ref[g].astype(BF16), _NT, preferred_element_type=F32) / l
        o_ref[g] = _head_diagonal(o_full, t, MEM_HEADS, head_dim)


def _mem_attn_sample(qm, mkt, mvt, group):
    db, t, _ = qm.shape
    n_mem = mkt.shape[2]
    blk = lambda i: (i, 0, 0)
    return pl.pallas_call(
        _mem_attn_sample_kernel,
        grid=(db // group,),
        in_specs=[pl.BlockSpec((group, t, D_MEM), blk),
                  pl.BlockSpec((group, D_MEM, n_mem), blk),
                  pl.BlockSpec((group, D_MEM, n_mem), blk)],
        out_specs=pl.BlockSpec((group, t, D_MEM), blk),
        out_shape=jax.ShapeDtypeStruct((db, t, D_MEM), F32),
        name="mem_attn_sample",
    )(qm, mkt, mvt)


def _merge_kernel(x_ref, attn_ref, ssm_ref, mem_ref, gate_ref, w_ref, g_ref, b_ref, y_ref):
    branches = jnp.concatenate([r[...].astype(F32) for r in (attn_ref, ssm_ref, mem_ref)], axis=1)
    mixed = branches * jax.nn.silu(gate_ref[...].astype(F32))
    out = jnp.dot(mixed.astype(BF16), w_ref[...], preferred_element_type=F32)
    z = DEEPNORM_ALPHA * x_ref[...] + out
    mu = jnp.mean(z, axis=1, keepdims=True)
    zc = z - mu
    var = jnp.mean(zc * zc, axis=1, keepdims=True)
    y_ref[...] = zc * lax.rsqrt(var + LN_EPS) * g_ref[...] + b_ref[...]


def _merge(x2d, attn, ssm, mem, gate, w_bf, g_row, b_row, tm, ssm_map):
    m = x2d.shape[0]
    row = lambda i: (i, 0)
    const = lambda i: (0, 0)
    return pl.pallas_call(
        _merge_kernel,
        grid=(m // tm,),
        in_specs=[pl.BlockSpec((tm, D_MODEL), row),
                  pl.BlockSpec((tm, D_ATTN), row),
                  pl.BlockSpec((tm, D_SSM), ssm_map),
                  pl.BlockSpec((tm, D_MEM), row),
                  pl.BlockSpec((tm, D_MIX), row),
                  pl.BlockSpec((D_MIX, D_MODEL), const),
                  pl.BlockSpec((1, D_MODEL), const),
                  pl.BlockSpec((1, D_MODEL), const)],
        out_specs=pl.BlockSpec((tm, D_MODEL), row),
        out_shape=jax.ShapeDtypeStruct((m, D_MODEL), F32),
        compiler_params=_vmem_limit(48 * 2**20),
        name="merge",
    )(x2d, attn, ssm, mem, gate, w_bf, g_row, b_row)


_PAGE_RING = 4


def _moba_sample_parts(b, n_seq, pt_ref, q_ref, kn_ref, vn_ref, ck_ref, cv_ref, o_ref,
                       buf_ref, sem_ref, s_ref, p_ref, qbd_ref, pown_ref, l_ref, acc_ref,
                       *, pages_per_chunk, chunks, page_size):
    ppc = pages_per_chunk
    items = 2 * chunks
    t = q_ref.shape[1]
    rows = N_HEADS_A * t
    n_pages = ppc * chunks
    pages_per_block = MOBA_BLOCK // page_size
    n_blocks = n_pages // pages_per_block

    def pad_rows(x):
        return jnp.concatenate([x, jnp.zeros((page_size - x.shape[0], x.shape[1]), x.dtype)], axis=0)

    ring = buf_ref.shape[0]
    ahead = ring - 1
    assert items % ring == 0 and ahead <= items

    def chunk_copies(seq, r):
        src_ref, chunk = (ck_ref, r) if r < chunks else (cv_ref, r - chunks)
        slot = r % ring
        return [pltpu.make_async_copy(src_ref.at[pt_ref[seq, chunk * ppc + i]], buf_ref.at[slot, i],
                                      sem_ref.at[slot]) for i in range(ppc)]

    def prologue():
        @pl.when(b == 0)
        def _first_chunks():
            for r in range(ahead):
                for cp in chunk_copies(b, r):
                    cp.start()
        qbd_ref[...] = _head_rows(q_ref[0] * (HEAD_DIM ** -0.5), N_HEADS_A, HEAD_DIM).astype(BF16)

    def key_chunk(r, slot):
        qbd_bf = qbd_ref[...]
        for i in range(ppc):
            s_ref[r * ppc + i] = jnp.dot(qbd_bf, buf_ref[slot, i].astype(BF16), preferred_element_type=F32)

    def select_and_softmax():
        qbd_bf = qbd_ref[...]
        lane = lax.broadcasted_iota(jnp.int32, (rows, page_size), 1)
        sc = jnp.zeros((rows, page_size), F32)
        for j in range(n_blocks):
            bs = s_ref[j * pages_per_block]
            for pg in range(1, pages_per_block):
                bs = bs + s_ref[j * pages_per_block + pg]
            sc = jnp.where(lane == j, jnp.sum(bs, axis=1, keepdims=True), sc)
        sel = _select_topk(sc, lane < n_blocks, lane.astype(F32), axis=1)
        lo = lax.dot_general(qbd_bf, pad_rows(kn_ref[0]).astype(BF16), _NT, preferred_element_type=F32)
        t_row = lax.broadcasted_iota(jnp.int32, (rows, page_size), 0) % t
        lo = jnp.where(lane <= t_row, lo, NEG_INF)
        mx = lo
        for j in range(n_blocks):
            bm = s_ref[j * pages_per_block]
            for pg in range(1, pages_per_block):
                bm = jnp.maximum(bm, s_ref[j * pages_per_block + pg])
            mx = jnp.maximum(mx, jnp.where(sel[:, j:j + 1], bm, NEG_INF))
        m = jnp.max(mx, axis=1, keepdims=True)
        po = jnp.exp(lo - m)
        lsum = po
        for j in range(n_blocks):
            sj = sel[:, j:j + 1]
            for pg in range(j * pages_per_block, (j + 1) * pages_per_block):
                p = jnp.where(sj, jnp.exp(s_ref[pg] - m), 0.0)
                lsum = lsum + p
                p_ref[pg] = p.astype(BF16)
        pown_ref[...] = po.astype(BF16)
        l_ref[...] = jnp.broadcast_to(jnp.sum(lsum, axis=1, keepdims=True), l_ref.shape)
        acc_ref[...] = jnp.zeros_like(acc_ref)

    def value_chunk(r, slot):
        acc = acc_ref[...]
        for i in range(ppc):
            acc = acc + lax.dot_general(p_ref[(r - chunks) * ppc + i], buf_ref[slot, i].astype(BF16), _NT,
                                        preferred_element_type=F32)
        acc_ref[...] = acc

    def item(r):
        slot = r % ring
        if r + ahead < items:
            for cp in chunk_copies(b, r + ahead):
                cp.start()
        else:
            @pl.when(b + 1 < n_seq)
            def _next_sequence():
                for cp in chunk_copies(b + 1, r + ahead - items):
                    cp.start()
        for cp in chunk_copies(b, r):
            cp.wait()
        if r < chunks:
            key_chunk(r, slot)
        else:
            if r == chunks:
                select_and_softmax()
            value_chunk(r, slot)

    def finish():
        acc = acc_ref[...] + jnp.dot(pown_ref[...], pad_rows(vn_ref[0]).astype(BF16), preferred_element_type=F32)
        l = jnp.concatenate([l_ref[...]] * (D_ATTN // page_size), axis=1)
        o_ref[0] = _head_diagonal(acc / l, t, N_HEADS_A, HEAD_DIM)

    return prologue, item, finish


def _moba_sample_scratch(t, n_pages, pages_per_chunk, page_size):
    rows = N_HEADS_A * t
    return [pltpu.VMEM((_PAGE_RING, pages_per_chunk, D_ATTN, page_size), F32),
            pltpu.SemaphoreType.DMA((_PAGE_RING,)),
            pltpu.VMEM((n_pages, rows, page_size), F32),
            pltpu.VMEM((n_pages, rows, page_size), BF16),
            pltpu.VMEM((rows, D_ATTN), BF16),
            pltpu.VMEM((rows, page_size), BF16),
            pltpu.VMEM((rows, page_size), F32),
            pltpu.VMEM((rows, D_ATTN), F32)]


_N_PROMPT_SCRATCH = 8


def _moba_kernel(pt_ref, q_ref, k_ref, v_ref, kmean_ref, qs_ref, kn_ref, vn_ref, ck_ref, cv_ref,
                 o_ref, os_ref, *scratch, n_tiles, tiles_per_head, n_seq, pages_per_chunk, chunks, page_size):
    n = pl.program_id(0)
    both = n_tiles == n_seq
    has_tile = None if both else n < n_tiles
    has_seq = None if both else n < n_seq

    def when(cond):
        return (lambda f: f()) if cond is None else pl.when(cond)

    ti = lax.rem(n if both else jnp.minimum(n, n_tiles - 1), tiles_per_head)
    p_setup, p_pairs, p_finish = _moba_prompt_parts(
        ti, q_ref, k_ref, v_ref, kmean_ref, o_ref, *scratch[:_N_PROMPT_SCRATCH])
    s_prologue, s_item, s_finish = _moba_sample_parts(
        n, n_seq, pt_ref, qs_ref, kn_ref, vn_ref, ck_ref, cv_ref, os_ref, *scratch[_N_PROMPT_SCRATCH:],
        pages_per_chunk=pages_per_chunk, chunks=chunks, page_size=page_size)

    when(has_tile)(p_setup)
    when(has_seq)(s_prologue)
    pairs = ti * (q_ref.shape[1] // MOBA_BLOCK) // 2
    if not both:
        pairs = jnp.where(has_tile, pairs, 0)
    items = 2 * chunks
    for r in range(items):
        when(has_seq)(functools.partial(s_item, r))
        p_pairs(r * pairs // items, (r + 1) * pairs // items)
    when(has_tile)(p_finish)
    when(has_seq)(s_finish)


def _moba(q, kb, vb, kmean, q_s, k_new, v_new, cache_k, cache_v, page_table, pages_per_chunk):
    b, s, _ = q.shape
    nb = s // MOBA_BLOCK
    tq = _MOBA_Q_BLOCKS * MOBA_BLOCK
    assert nb <= LANES // 2 and nb % SUBLANES == 0 and s % tq == 0 and _MOBA_Q_BLOCKS % 2 == 0
    head_pairs = D_ATTN // LANES
    tiles_per_head = s // tq
    n_tiles = b * head_pairs * tiles_per_head

    db, t, _ = q_s.shape
    n_pages = page_table.shape[1]
    page_size = cache_k.shape[2]
    chunks = n_pages // pages_per_chunk
    assert (n_pages * page_size) % MOBA_BLOCK == 0 and MOBA_BLOCK % page_size == 0
    assert n_pages * page_size // MOBA_BLOCK >= MOBA_TOPK and t <= page_size and page_size == LANES
    assert n_pages % pages_per_chunk == 0

    def tile_of(n):
        n = jnp.minimum(n, n_tiles - 1)
        return n // (head_pairs * tiles_per_head), (n // tiles_per_head) % head_pairs, n % tiles_per_head

    def tile_map(n, pt):
        bi, hp, ti = tile_of(n)
        return (bi, ti, hp)

    def head_map(n, pt):
        bi, hp, _ = tile_of(n)
        return (bi, 0, hp)

    def seq_map(n, pt):
        return (jnp.minimum(n, db - 1), 0, 0)

    grid_spec = pltpu.PrefetchScalarGridSpec(
        num_scalar_prefetch=1,
        grid=(max(n_tiles, db),),
        in_specs=[pl.BlockSpec((1, tq, LANES), tile_map),
                  pl.BlockSpec((1, s, LANES), head_map),
                  pl.BlockSpec((1, s, LANES), head_map),
                  pl.BlockSpec((1, nb, LANES), head_map)]
                 + [pl.BlockSpec((1, t, D_ATTN), seq_map)] * 3
                 + [pl.BlockSpec(memory_space=pl.ANY)] * 2,
        out_specs=(pl.BlockSpec((1, tq, LANES), tile_map), pl.BlockSpec((1, t, D_ATTN), seq_map)),
        scratch_shapes=_moba_prompt_scratch(s, tq) + _moba_sample_scratch(t, n_pages, pages_per_chunk, page_size))
    return pl.pallas_call(
        functools.partial(_moba_kernel, n_tiles=n_tiles, tiles_per_head=tiles_per_head, n_seq=db,
                          pages_per_chunk=pages_per_chunk, chunks=chunks, page_size=page_size),
        grid_spec=grid_spec,
        out_shape=(jax.ShapeDtypeStruct((b, s, D_ATTN), BF16), jax.ShapeDtypeStruct((db, t, D_ATTN), F32)),
        compiler_params=pltpu.CompilerParams(dimension_semantics=("arbitrary",),
                                             vmem_limit_bytes=52 * 2**20),
        name="moba",
    )(page_table, q, kb, vb, kmean, q_s, k_new, v_new, cache_k, cache_v)


def kernel(x_prompt, x_sample, mem_prompt, cache_k, cache_v, state_ssm_re, state_ssm_im,
           cache_mem_k, cache_mem_v, page_table, w_in, w_mem_kv, a_re, a_im, log_dt,
           b_re, b_im, c_re, c_im, d_skip, w_glu, b_glu, w_out, ln_g, ln_b):
    batch, seq, _ = x_prompt.shape
    db, t_new, _ = x_sample.shape
    n_mem = mem_prompt.shape[1]
    n_phys, page_size = cache_k.shape[:2]
    n_seg = SSM_CHAINS // batch
    seg_len = seq // n_seg
    tm = min(512, seg_len)

    w_in_bf = w_in.astype(BF16)
    w_mem_bf = w_mem_kv.astype(BF16)
    w_out_bf = w_out.astype(BF16)
    wg_bf = w_glu.astype(BF16)
    bg_row = b_glu.astype(F32).reshape(1, 2 * D_SSM)
    d_row = d_skip.astype(F32).reshape(1, D_SSM)
    g_row = ln_g.astype(F32).reshape(1, D_MODEL)
    b_row = ln_b.astype(F32).reshape(1, D_MODEL)
    a_row, b_mat, c_mat = _ssm_operands(a_re, a_im, log_dt, b_re, b_im, c_re, c_im)
    b_bf = b_mat.astype(BF16)
    c_bf = c_mat.astype(BF16)

    x2d = x_prompt.reshape(batch * seq, D_MODEL)
    q, k_t, v_t, u_chain, qm, gate, kb, vb, kmean = _in_proj_prompt(x2d, w_in_bf, batch, seq, tm)
    xs2d = x_sample.reshape(db * t_new, D_MODEL)
    tms = min(512, db * t_new)
    q_s, k_s, v_s, u_s, qm_s, gate_s = _in_proj_sample(xs2d, w_in_bf, tms)
    attn, attn_s = _moba(q.reshape(batch, seq, D_ATTN), kb.reshape(batch, seq, D_ATTN),
                         vb.reshape(batch, seq, D_ATTN), kmean.reshape(batch, seq // MOBA_BLOCK, D_ATTN),
                         q_s.reshape(db, t_new, D_ATTN), k_s.reshape(db, t_new, D_ATTN),
                         v_s.reshape(db, t_new, D_ATTN),
                         cache_k.transpose(0, 2, 3, 1).reshape(n_phys, D_ATTN, page_size),
                         cache_v.transpose(0, 2, 3, 1).reshape(n_phys, D_ATTN, page_size), page_table,
                         pages_per_chunk=min(16, page_table.shape[1] // 2))

    y_chain, hfin = _ssm_prompt(u_chain.reshape(seg_len * SSM_CHAINS, D_SSM), a_row, b_bf, c_bf, d_row,
                                wg_bf, bg_row, n_seg, steps_per_chunk=min(64, seg_len))
    mk, mv = _mem_kv(mem_prompt.reshape(batch * n_mem, D_MODEL), w_mem_bf)
    mem_o = _mem_attn_prompt(qm, mk, mv, batch, seq, tm)
    tiles_per_seq = seq // tm
    tiles_per_seg = seg_len // tm

    def chain_map(i):
        b = i // tiles_per_seq
        ti = i % tiles_per_seq
        return (ti % tiles_per_seg, b * n_seg + ti // tiles_per_seg)

    y_prompt = _merge(x2d, attn.reshape(batch * seq, D_ATTN), y_chain.reshape(seg_len, SSM_CHAINS * D_SSM),
                      mem_o, gate, w_out_bf, g_row, b_row, tm, chain_map)
    h_last = hfin[n_seg - 1::n_seg]
    ssm_re_p = h_last[:, :SSM_LANES].reshape(batch, SSM_GROUPS, SSM_STATE)
    ssm_im_p = h_last[:, SSM_LANES:].reshape(batch, SSM_GROUPS, SSM_STATE)

    h0 =jnp.concatenate([state_ssm_re.astype(F32).reshape(db, SSM_LANES),
                          state_ssm_im.astype(F32).reshape(db, SSM_LANES)], axis=1)
    y_tb, hfin_s = _ssm_sample(u_s.reshape(db, t_new, D_SSM).transpose(1, 0, 2), h0, a_row, b_mat, c_bf,
                               d_row, wg_bf, bg_row)
    ssm_s = y_tb.transpose(1, 0, 2).reshape(db * t_new, D_SSM)
    mem_s = _mem_attn_sample(qm_s.reshape(db, t_new, D_MEM),
                             cache_mem_k.transpose(0, 2, 3, 1).reshape(db, D_MEM, n_mem),
                             cache_mem_v.transpose(0, 2, 3, 1).reshape(db, D_MEM, n_mem), group=min(8, db))
    y_sample = _merge(xs2d, attn_s.reshape(db * t_new, D_ATTN), ssm_s, mem_s.reshape(db * t_new, D_MEM),
                      gate_s, w_out_bf, g_row, b_row, tms, lambda i: (i, 0))

    heads = (N_HEADS_A, HEAD_DIM)
    mem_heads = (MEM_HEADS, D_MEM // MEM_HEADS)
    return (y_prompt.reshape(batch, seq, D_MODEL), y_sample.reshape(db, t_new, D_MODEL),
            k_t.reshape(batch, *heads, seq).transpose(0, 3, 1, 2),
            v_t.reshape(batch, *heads, seq).transpose(0, 3, 1, 2), ssm_re_p, ssm_im_p,
            mk.reshape(batch, n_mem, *mem_heads), mv.reshape(batch, n_mem, *mem_heads),
            k_s.reshape(db, t_new, *heads), v_s.reshape(db, t_new, *heads),
            hfin_s[:, :SSM_LANES].reshape(db, SSM_GROUPS, SSM_STATE),
            hfin_s[:, SSM_LANES:].reshape(db, SSM_GROUPS, SSM_STATE))
```

```python
import functools
import math

import jax
import jax.numpy as jnp
from jax import lax
from jax.experimental import pallas as pl
from jax.experimental.pallas import tpu as pltpu

F32 = jnp.float32
BF16 = jnp.bfloat16

D_MODEL = 1024
D_ATTN = 512
D_SSM = 256
D_MEM = 256
D_MIX = D_ATTN + D_SSM + D_MEM
HEAD_DIM = 64
N_HEADS_A = D_ATTN // HEAD_DIM
MOBA_BLOCK = 256
MOBA_TOPK = 3
SSM_CH = 16
SSM_GROUPS = D_SSM // SSM_CH
SSM_STATE = 64
SSM_LANES = SSM_GROUPS * SSM_STATE
MEM_HEADS = 4
D_IN = 3 * D_ATTN + D_SSM + D_MEM + D_MIX
DEPTH = 1
DEEPNORM_ALPHA = (2.0 * DEPTH) ** 0.25
LN_EPS = 1e-5
NEG_INF = -1e30

LANES = 128
SUBLANES = 8
SSM_CHAINS = SUBLANES

_NT = (((1,), (1,)), ((), ()))


def _vmem_limit(nbytes):
    return pltpu.CompilerParams(vmem_limit_bytes=int(nbytes))


def _proj_kernel(x_ref, w_ref, *out_refs, with_prompt_extras):
    xb = x_ref[...].astype(BF16)

    def seg(lo, hi):
        return jnp.dot(xb, w_ref[:, lo:hi], preferred_element_type=F32)

    c0, c1, c2, c3, c4 = D_ATTN, 2 * D_ATTN, 3 * D_ATTN, 3 * D_ATTN + D_SSM, 3 * D_ATTN + D_SSM + D_MEM
    if with_prompt_extras:
        q_ref, k_ref, v_ref, u_ref, qm_ref, gate_ref, kb_ref, vb_ref, kmean_ref = out_refs
    else:
        q_ref, k_ref, v_ref, u_ref, qm_ref, gate_ref = out_refs
    q_ref[...] = seg(0, c0)
    k = seg(c0, c1)
    v = seg(c1, c2)
    u_ref[...] = seg(c2, c3)
    qm_ref[...] = seg(c3, c4).astype(qm_ref.dtype)
    gate_ref[...] = seg(c4, D_IN).astype(gate_ref.dtype)
    if not with_prompt_extras:
        k_ref[...] = k
        v_ref[...] = v
    else:
        k_ref[0] = k.T
        v_ref[0] = v.T
        kb_ref[...] = k.astype(BF16)
        vb_ref[...] = v.astype(BF16)
        tm = k.shape[0]
        kmean_ref[0] = jnp.sum(k.reshape(tm // MOBA_BLOCK, MOBA_BLOCK, D_ATTN), axis=1) * (1.0 / MOBA_BLOCK)


def _in_proj_prompt(x2d, w_bf, batch, seq, tm):
    m = x2d.shape[0]
    n_seg = SSM_CHAINS // batch
    seg_len = seq // n_seg
    tiles_per_seq = seq // tm
    tiles_per_seg = seg_len // tm
    row = lambda i: (i, 0)

    def u_map(i):
        b = i // tiles_per_seq
        t = i % tiles_per_seq
        return (t % tiles_per_seg, b * n_seg + t // tiles_per_seg)

    out_shape = (
        jax.ShapeDtypeStruct((m, D_ATTN), F32),
        jax.ShapeDtypeStruct((batch, D_ATTN, seq), F32),
        jax.ShapeDtypeStruct((batch, D_ATTN, seq), F32),
        jax.ShapeDtypeStruct((seg_len, SSM_CHAINS * D_SSM), F32),
        jax.ShapeDtypeStruct((m, D_MEM), BF16),
        jax.ShapeDtypeStruct((m, D_MIX), BF16),
        jax.ShapeDtypeStruct((m, D_ATTN), BF16),
        jax.ShapeDtypeStruct((m, D_ATTN), BF16),
        jax.ShapeDtypeStruct((m // tm, tm // MOBA_BLOCK, D_ATTN), F32),
    )
    kv_t = lambda i: (i // tiles_per_seq, 0, i % tiles_per_seq)
    out_specs = (
        pl.BlockSpec((tm, D_ATTN), row),
        pl.BlockSpec((1, D_ATTN, tm), kv_t),
        pl.BlockSpec((1, D_ATTN, tm), kv_t),
        pl.BlockSpec((tm, D_SSM), u_map),
        pl.BlockSpec((tm, D_MEM), row),
        pl.BlockSpec((tm, D_MIX), row),
        pl.BlockSpec((tm, D_ATTN), row),
        pl.BlockSpec((tm, D_ATTN), row),
        pl.BlockSpec((1, tm // MOBA_BLOCK, D_ATTN), lambda i: (i, 0, 0)),
    )
    return pl.pallas_call(
        functools.partial(_proj_kernel, with_prompt_extras=True),
        grid=(m // tm,),
        in_specs=[pl.BlockSpec((tm, D_MODEL), row),
                  pl.BlockSpec((D_MODEL, D_IN), lambda i: (0, 0), pipeline_mode=pl.Buffered(1))],
        out_specs=out_specs,
        out_shape=out_shape,
        compiler_params=_vmem_limit(52 * 2**20),
        name="in_proj_prompt",
    )(x2d, w_bf)


def _in_proj_sample(x2d, w_bf, tm):
    m = x2d.shape[0]
    row = lambda i: (i, 0)
    out_shape = (
        jax.ShapeDtypeStruct((m, D_ATTN), F32),
        jax.ShapeDtypeStruct((m, D_ATTN), F32),
        jax.ShapeDtypeStruct((m, D_ATTN), F32),
        jax.ShapeDtypeStruct((m, D_SSM), F32),
        jax.ShapeDtypeStruct((m, D_MEM), BF16),
        jax.ShapeDtypeStruct((m, D_MIX), BF16),
    )
    widths = (D_ATTN, D_ATTN, D_ATTN, D_SSM, D_MEM, D_MIX)
    return pl.pallas_call(
        functools.partial(_proj_kernel, with_prompt_extras=False),
        grid=(m // tm,),
        in_specs=[pl.BlockSpec((tm, D_MODEL), row),
                  pl.BlockSpec((D_MODEL, D_IN), lambda i: (0, 0), pipeline_mode=pl.Buffered(1))],
        out_specs=tuple(pl.BlockSpec((tm, w), row) for w in widths),
        out_shape=out_shape,
        compiler_params=_vmem_limit(52 * 2**20),
        name="in_proj_sample",
    )(x2d, w_bf)


def _mem_kv_kernel(mem_ref, w_ref, mk_ref, mv_ref):
    kv = jnp.dot(mem_ref[...].astype(BF16), w_ref[...], preferred_element_type=F32)
    mk_ref[...] = kv[:, :D_MEM]
    mv_ref[...] = kv[:, D_MEM:]


def _mem_kv(mem2d, w_bf):
    m = mem2d.shape[0]
    tm = min(m, 512)
    row = lambda i: (i, 0)
    return pl.pallas_call(
        _mem_kv_kernel,
        grid=(m // tm,),
        in_specs=[pl.BlockSpec((tm, D_MODEL), row),
                  pl.BlockSpec((D_MODEL, 2 * D_MEM), lambda i: (0, 0))],
        out_specs=(pl.BlockSpec((tm, D_MEM), row), pl.BlockSpec((tm, D_MEM), row)),
        out_shape=(jax.ShapeDtypeStruct((m, D_MEM), F32), jax.ShapeDtypeStruct((m, D_MEM), F32)),
        name="mem_kv",
    )(mem2d, w_bf)


def _select_topk(scores, valid, index, axis):
    lowest = jnp.finfo(F32).min
    s = jnp.where(valid, scores, lowest)
    sel = jnp.zeros(scores.shape, dtype=jnp.bool_)
    for _ in range(MOBA_TOPK):
        m = jnp.max(s, axis=axis, keepdims=True)
        first = jnp.min(jnp.where(s == m, index, jnp.float32(2**30)), axis=axis, keepdims=True)
        pick = index == first
        sel = sel | (pick & valid)
        s = jnp.where(pick, lowest, s)
    return sel


_MOBA_Q_BLOCKS = 2
_MOBA_ROW_GROUP = 128


def _moba_prompt_parts(ti, q_ref, k_ref, v_ref, kmean_ref, o_ref,
                       kaug_ref, vaug_ref, lhs_ref, s_ref, p_ref, alpha_ref, m_ref, acc_ref):
    blk = MOBA_BLOCK
    tq = q_ref.shape[1]
    qb = tq // blk
    nb = k_ref.shape[1] // blk
    half = LANES // 2

    def build_operands():
        lane_b = lax.broadcasted_iota(jnp.int32, (blk, LANES), 1)
        head0_b = lane_b < half

        def body(j, carry):
            rows = pl.ds(pl.multiple_of(j * blk, blk), blk)
            kb = k_ref[0, rows, :].astype(F32)
            vb = v_ref[0, rows, :].astype(F32)
            hot0 = jnp.where(lane_b == half + j, 1.0, 0.0)
            hot1 = jnp.where(lane_b == j, 1.0, 0.0)
            kaug_ref[0, rows, :] = jnp.where(head0_b, kb, hot0).astype(BF16)
            kaug_ref[1, rows, :] = jnp.where(head0_b, hot1, kb).astype(BF16)
            vaug_ref[0, rows, :] = jnp.where(head0_b, vb, 1.0).astype(BF16)
            vaug_ref[1, rows, :] = jnp.where(head0_b, 1.0, vb).astype(BF16)
            return carry
        lax.fori_loop(0, nb, body, 0)

    def select_blocks():
        q2 = q_ref[0]
        means = kmean_ref[0]
        lane_m = lax.broadcasted_iota(jnp.int32, (nb, LANES), 1)
        pad = [jnp.zeros((half - nb, LANES), F32)] if nb < half else []
        means_p = jnp.concatenate([jnp.where(lane_m >= half, means, 0.0)] + pad
                                  + [jnp.where(lane_m < half, means, 0.0)] + pad, axis=0)
        sc_t = lax.dot_general(means_p, q2, _NT, precision=lax.Precision.HIGHEST,
                               preferred_element_type=F32)
        slot = lax.broadcasted_iota(jnp.int32, (nb, tq), 0)
        slot_f = slot.astype(F32)
        n_full = ti * qb + lax.broadcasted_iota(jnp.int32, (nb, tq), 1) // blk
        bias_t = []
        for off in (0, half):
            sel = _select_topk(sc_t[off:off + nb], slot < n_full, slot_f, axis=0)
            bias_t.append(jnp.where(sel | (slot == n_full), 0.0, NEG_INF))
            if nb < half:
                bias_t.append(jnp.zeros((half - nb, tq), F32))
        bias = jnp.concatenate(bias_t, axis=0).T
        head0_q = lax.broadcasted_iota(jnp.int32, (tq, LANES), 1) < half
        qs = q2 * (HEAD_DIM ** -0.5)
        lhs_ref[0] = jnp.where(head0_q, qs, bias).astype(BF16)
        lhs_ref[1] = jnp.where(head0_q, bias, qs).astype(BF16)
        m_ref[...] = jnp.full(m_ref.shape, NEG_INF, F32)
        acc_ref[...] = jnp.zeros_like(acc_ref)

    rg = _MOBA_ROW_GROUP
    c_i = lax.broadcasted_iota(jnp.int32, (rg, blk), 1)
    r_i = lax.broadcasted_iota(jnp.int32, (rg, blk), 0)

    def key_rows(j):
        return pl.ds(j * blk if isinstance(j, int) else pl.multiple_of(j * blk, blk), blk)

    def score_block(j, slot):
        for e in (0, 1):
            s_ref[slot, e] = lax.dot_general(lhs_ref[e], kaug_ref[e, key_rows(j), :], _NT,
                                             preferred_element_type=F32)

    def softmax_pv_block(j, slot, causal):
        for e in (0, 1):
            for g in range(tq // rg):
                rows = pl.ds(g * rg, rg)
                s = s_ref[slot, e, rows, :]
                if causal:
                    s = jnp.where(j * blk + c_i <= ti * tq + g * rg + r_i, s, NEG_INF)
                m_old = m_ref[e, rows, :]
                m_new = jnp.maximum(m_old, jnp.max(s, axis=1, keepdims=True))
                alpha_ref[e, rows, :] = jnp.exp(m_old - m_new)
                p_ref[e, rows, :] = jnp.exp(s - jnp.concatenate([m_new] * (blk // LANES), axis=1)).astype(BF16)
                m_ref[e, rows, :] = m_new
            pv = jnp.dot(p_ref[e], vaug_ref[e, key_rows(j), :], preferred_element_type=F32)
            acc_ref[e] = alpha_ref[e] * acc_ref[e] + pv

    n_past = ti * qb

    def setup():
        pl.when(ti == 0)(build_operands)
        select_blocks()
        score_block(0, 0)

    def past_pair(jj, carry):
        j = 2 * jj
        score_block(j + 1, 1)
        softmax_pv_block(j, 0, causal=False)
        score_block(j + 2, 0)
        softmax_pv_block(j + 1, 1, causal=False)
        return carry

    def past_pairs(lo, hi):
        lax.fori_loop(lo, hi, past_pair, 0)

    def finish():
        for d in range(qb):
            if d + 1 < qb:
                score_block(n_past + d + 1, (d + 1) % 2)
            softmax_pv_block(n_past + d, d % 2, causal=True)
        a0 = acc_ref[0]
        a1 = acc_ref[1]
        o0 = a0 / pltpu.roll(a0, half, axis=1)
        o1 = a1 / pltpu.roll(a1, half, axis=1)
        head0_q = lax.broadcasted_iota(jnp.int32, (tq, LANES), 1) < half
        o_ref[0] = jnp.where(head0_q, o0, o1).astype(o_ref.dtype)

    return setup, past_pairs, finish


def _moba_prompt_scratch(s, tq):
    return [pltpu.VMEM((2, s, LANES), BF16),
            pltpu.VMEM((2, s, LANES), BF16),
            pltpu.VMEM((2, tq, LANES), BF16),
            pltpu.VMEM((2, 2, tq, MOBA_BLOCK), F32),
            pltpu.VMEM((2, tq, MOBA_BLOCK), BF16),
            pltpu.VMEM((2, tq, LANES), F32),
            pltpu.VMEM((2, tq, LANES), F32),
            pltpu.VMEM((2, tq, LANES), F32)]


def _ssm_disc_kernel(a_re_ref, a_im_ref, log_dt_ref, bt_re_ref, bt_im_ref,
                     abar_re_ref, abar_im_ref, bbt_re_ref, bbt_im_ref):
    a_re = a_re_ref[...]
    a_im = a_im_ref[...]
    dt = jnp.exp(log_dt_ref[...])
    mag = jnp.exp(a_re * dt)
    ang = a_im * dt
    abar_re = mag * jnp.cos(ang)
    abar_im = mag * jnp.sin(ang)
    den = a_re * a_re + a_im * a_im
    f_re = ((abar_re - 1.0) * a_re + abar_im * a_im) / den
    f_im = (abar_im * a_re - (abar_re - 1.0) * a_im) / den
    abar_re_ref[...] = abar_re
    abar_im_ref[...] = abar_im
    bt_re = bt_re_ref[...]
    bt_im = bt_im_ref[...]
    bbt_re_ref[...] = f_re[:, None, :] * bt_re - f_im[:, None, :] * bt_im
    bbt_im_ref[...] = f_re[:, None, :] * bt_im + f_im[:, None, :] * bt_re


def _ssm_operands(a_re, a_im, log_dt, b_re, b_im, c_re, c_im):
    g, p, c = b_re.shape
    abar_re, abar_im, bbt_re, bbt_im = pl.pallas_call(
        _ssm_disc_kernel,
        out_shape=(jax.ShapeDtypeStruct((g, p), F32), jax.ShapeDtypeStruct((g, p), F32),
                   jax.ShapeDtypeStruct((g, c, p), F32), jax.ShapeDtypeStruct((g, c, p), F32)),
        name="ssm_discretize",
    )(a_re.astype(F32), a_im.astype(F32), log_dt.astype(F32).reshape(g, 1),
      b_re.astype(F32).transpose(0, 2, 1), b_im.astype(F32).transpose(0, 2, 1))
    eye = jnp.eye(g, dtype=F32)

    def in_block_diag(bbt):
        return jnp.einsum('gcp,gh->gchp', bbt, eye).reshape(g * c, g * p)

    def out_block_diag(cm):
        return jnp.einsum('gcp,gh->gphc', cm, eye).reshape(g * p, g * c)

    b_mat = jnp.concatenate([in_block_diag(bbt_re), in_block_diag(bbt_im)], axis=1)
    c_mat = jnp.concatenate([out_block_diag(c_re.astype(F32)),
                             -out_block_diag(c_im.astype(F32))], axis=0)
    a_row = jnp.concatenate([abar_re.reshape(1, g * p), abar_im.reshape(1, g * p)], axis=1)
    return a_row, b_mat, c_mat


def _complex_step(a_re, a_im, h_re, h_im, bu_re, bu_im):
    return (a_re * h_re - a_im * h_im + bu_re, a_re * h_im + a_im * h_re + bu_im)


def _ssm_output(h_all, u, c_ref, d_ref, wg_ref, bg_ref):
    y = jnp.dot(h_all.astype(BF16), c_ref[...], preferred_element_type=F32) + d_ref[...] * u
    y = jax.nn.gelu(y)
    z = jnp.dot(y.astype(BF16), wg_ref[...], preferred_element_type=F32) + bg_ref[...]
    return z[:, :D_SSM] * jax.nn.sigmoid(z[:, D_SSM:])


def _ssm_prompt_kernel(u_ref, a_ref, b_ref, c_ref, d_ref, wg_ref, bg_ref, y_ref, hfin_ref,
                       hb_ref, h_ref, init_ref, *, n_seg):
    pss = pl.program_id(0)
    c = pl.program_id(1)
    n_chunks = pl.num_programs(1)
    rows = u_ref.shape[0]
    steps = rows // SSM_CHAINS
    n = SSM_LANES
    a_re = jnp.broadcast_to(a_ref[:, :n], (SSM_CHAINS, n))
    a_im = jnp.broadcast_to(a_ref[:, n:], (SSM_CHAINS, n))

    @pl.when((pss == 0) & (c == 0))
    def _zero_state():
        h_ref[...] = jnp.zeros_like(h_ref)

    @pl.when((pss == 1) & (c == 0))
    def _true_initial_state():
        h_ref[...] = init_ref[...]

    u = u_ref[...]
    hb_ref[...] = jnp.dot(u.astype(BF16), b_ref[...], preferred_element_type=F32)

    def scan(store):
        def step(s, carry):
            h_re, h_im = carry
            r = pl.ds(pl.multiple_of(s * SSM_CHAINS, SSM_CHAINS), SSM_CHAINS)
            h_re, h_im = _complex_step(a_re, a_im, h_re, h_im, hb_ref[r, :n], hb_ref[r, n:])
            if store:
                hb_ref[r, :n] = h_re
                hb_ref[r, n:] = h_im
            return h_re, h_im
        h_re, h_im = lax.fori_loop(0, steps, step, (h_ref[:, :n], h_ref[:, n:]), unroll=4)
        h_ref[:, :n] = h_re
        h_ref[:, n:] = h_im

    @pl.when(pss == 0)
    def _local_pass():
        scan(False)

    @pl.when((pss == 0) & (c == n_chunks - 1))
    def _segment_initial_states():
        seg_len = steps * n_chunks
        p_re, p_im = a_re, a_im
        for _ in range(int(math.log2(seg_len))):
            p_re, p_im = p_re * p_re - p_im * p_im, 2.0 * p_re * p_im
        e_re = h_ref[:, :n]
        e_im = h_ref[:, n:]
        first = (lax.broadcasted_iota(jnp.int32, (SSM_CHAINS, n), 0) % n_seg) == 0
        i_re = jnp.zeros((SSM_CHAINS, n), F32)
        i_im = jnp.zeros((SSM_CHAINS, n), F32)
        for _ in range(n_seg - 1):
            t_re = p_re * i_re - p_im * i_im + e_re
            t_im = p_re * i_im + p_im * i_re + e_im
            i_re = jnp.where(first, 0.0, pltpu.roll(t_re, 1, axis=0))
            i_im = jnp.where(first, 0.0, pltpu.roll(t_im, 1, axis=0))
        init_ref[:, :n] = i_re
        init_ref[:, n:] = i_im

    @pl.when(pss == 1)
    def _output_pass():
        scan(True)
        y_ref[...] = _ssm_output(hb_ref[...], u, c_ref, d_ref, wg_ref, bg_ref).astype(y_ref.dtype)

    @pl.when((pss == 1) & (c == n_chunks - 1))
    def _final_state():
        hfin_ref[...] = h_ref[...]


def _ssm_prompt(u_chain, a_row, b_bf, c_bf, d_row, wg_bf, bg_row, n_seg, steps_per_chunk):
    total = u_chain.shape[0]
    rows = steps_per_chunk * SSM_CHAINS
    n_chunks = total // rows
    seg_len = total // SSM_CHAINS
    assert 2 ** int(math.log2(seg_len)) == seg_len
    const = lambda p, c: (0, 0)
    return pl.pallas_call(
        functools.partial(_ssm_prompt_kernel, n_seg=n_seg),
        grid=(2, n_chunks),
        in_specs=[pl.BlockSpec((rows, D_SSM), lambda p, c: (c, 0)),
                  pl.BlockSpec((1, 2 * SSM_LANES), const),
                  pl.BlockSpec((D_SSM, 2 * SSM_LANES), const),
                  pl.BlockSpec((2 * SSM_LANES, D_SSM), const),
                  pl.BlockSpec((1, D_SSM), const),
                  pl.BlockSpec((D_SSM, 2 * D_SSM), const),
                  pl.BlockSpec((1, 2 * D_SSM), const)],
        out_specs=(pl.BlockSpec((rows, D_SSM), lambda p, c: (p * c, 0)),
                   pl.BlockSpec((SSM_CHAINS, 2 * SSM_LANES), const)),
        out_shape=(jax.ShapeDtypeStruct((total, D_SSM), BF16),
                   jax.ShapeDtypeStruct((SSM_CHAINS, 2 * SSM_LANES), F32)),
        scratch_shapes=[pltpu.VMEM((rows, 2 * SSM_LANES), F32),
                        pltpu.VMEM((SSM_CHAINS, 2 * SSM_LANES), F32),
                        pltpu.VMEM((SSM_CHAINS, 2 * SSM_LANES), F32)],
        compiler_params=pltpu.CompilerParams(dimension_semantics=("arbitrary", "arbitrary"),
                                             vmem_limit_bytes=40 * 2**20),
        name="ssm_prompt",
    )(u_chain, a_row, b_bf, c_bf, d_row, wg_bf, bg_row)


def _ssm_sample_kernel(u_ref, h0_ref, a_ref, b_ref, c_ref, d_ref, wg_ref, bg_ref, y_ref, hfin_ref):
    steps, batch, _ = u_ref.shape
    n = SSM_LANES
    a_re = jnp.broadcast_to(a_ref[:, :n], (batch, n))
    a_im = jnp.broadcast_to(a_ref[:, n:], (batch, n))
    h_re = h0_ref[:, :n]
    h_im = h0_ref[:, n:]
    for t in range(steps):
        u = u_ref[t]
        bu = jnp.dot(u, b_ref[...], precision=lax.Precision.HIGHEST, preferred_element_type=F32)
        h_re, h_im = _complex_step(a_re, a_im, h_re, h_im, bu[:, :n], bu[:, n:])
        h_all = jnp.concatenate([h_re, h_im], axis=1)
        y_ref[t] = _ssm_output(h_all, u, c_ref, d_ref, wg_ref, bg_ref)
    hfin_ref[:, :n] = h_re
    hfin_ref[:, n:] = h_im


def _ssm_sample(u_tb, h0, a_row, b_f32, c_bf, d_row, wg_bf, bg_row):
    steps, batch, _ = u_tb.shape
    return pl.pallas_call(
        _ssm_sample_kernel,
        out_shape=(jax.ShapeDtypeStruct((steps, batch, D_SSM), F32),
                   jax.ShapeDtypeStruct((batch, 2 * SSM_LANES), F32)),
        compiler_params=_vmem_limit(40 * 2**20),
        name="ssm_sample",
    )(u_tb, h0, a_row, b_f32, c_bf, d_row, wg_bf, bg_row)


def _mem_attn_prompt_kernel(q_ref, mk_ref, mv_ref, o_ref):
    rows = q_ref.shape[0]
    half = LANES // 2
    scale = (D_MEM // MEM_HEADS) ** -0.5
    head0 = lax.broadcasted_iota(jnp.int32, (rows, LANES), 1) < half
    outs = []
    for hp in range(D_MEM // LANES):
        cols = slice(hp * LANES, (hp + 1) * LANES)
        q2 = q_ref[:, cols].astype(F32)
        k2 = mk_ref[:, cols].astype(BF16)
        v2 = mv_ref[:, cols].astype(BF16)
        res = []
        for e in (0, 1):
            mine = head0 if e == 0 else jnp.logical_not(head0)
            lhs = jnp.where(mine, q2, 0.0).astype(BF16)
            s = lax.dot_general(lhs, k2, _NT, preferred_element_type=F32) * scale
            p = jnp.exp(s - jnp.max(s, axis=1, keepdims=True))
            l = jnp.sum(p, axis=1, keepdims=True)
            res.append(jnp.dot(p.astype(BF16), v2, preferred_element_type=F32) / l)
        outs.append(jnp.where(head0, res[0], res[1]))
    o_ref[...] = jnp.concatenate(outs, axis=1).astype(o_ref.dtype)


def _mem_attn_prompt(qm, mk, mv, batch, seq, tm):
    n_mem = mk.shape[0] // batch
    tiles = seq // tm
    return pl.pallas_call(
        _mem_attn_prompt_kernel,
        grid=(batch, tiles),
        in_specs=[pl.BlockSpec((tm, D_MEM), lambda b, i: (b * tiles + i, 0)),
                  pl.BlockSpec((n_mem, D_MEM), lambda b, i: (b, 0)),
                  pl.BlockSpec((n_mem, D_MEM), lambda b, i: (b, 0))],
        out_specs=pl.BlockSpec((tm, D_MEM), lambda b, i: (b * tiles + i, 0)),
        out_shape=jax.ShapeDtypeStruct((batch * seq, D_MEM), BF16),
        name="mem_attn_prompt",
    )(qm, mk, mv)


def _head_rows(x, n_heads, head_dim):
    head_of_lane = lax.broadcasted_iota(jnp.int32, x.shape, 1) // head_dim
    return jnp.concatenate([jnp.where(head_of_lane == h, x, 0.0) for h in range(n_heads)], axis=0)


def _head_diagonal(o_full, t, n_heads, head_dim):
    head_of_lane = lax.broadcasted_iota(jnp.int32, (t, o_full.shape[1]), 1) // head_dim
    out = jnp.zeros((t, o_full.shape[1]), F32)
    for h in range(n_heads):
        out = out + jnp.where(head_of_lane == h, o_full[h * t:(h + 1) * t], 0.0)
    return out


def _mem_attn_sample_kernel(q_ref, mkt_ref, mvt_ref, o_ref):
    group, t, _ = q_ref.shape
    head_dim = D_MEM // MEM_HEADS
    scale = head_dim ** -0.5
    for g in range(group):
        qbd = _head_rows(q_ref[g].astype(F32) * scale, MEM_HEADS, head_dim).astype(BF16)
        s = jnp.dot(qbd, mkt_ref[g].astype(BF16), preferred_element_type=F32)
        p = jnp.exp(s - jnp.max(s, axis=1, keepdims=True))
        l = jnp.sum(p, axis=1, keepdims=True)
        o_full = lax.dot_general(p.astype(BF16), mvt_ref[g].astype(BF16), _NT, preferred_element_type=F32) / l
        o_ref[g] = _head_diagonal(o_full, t, MEM_HEADS, head_dim)


def _mem_attn_sample(qm, mkt, mvt, group):
    db, t, _ = qm.shape
    n_mem = mkt.shape[2]
    blk = lambda i: (i, 0, 0)
    return pl.pallas_call(
        _mem_attn_sample_kernel,
        grid=(db // group,),
        in_specs=[pl.BlockSpec((group, t, D_MEM), blk),
                  pl.BlockSpec((group, D_MEM, n_mem), blk),
                  pl.BlockSpec((group, D_MEM, n_mem), blk)],
        out_specs=pl.BlockSpec((group, t, D_MEM), blk),
        out_shape=jax.ShapeDtypeStruct((db, t, D_MEM), F32),
        name="mem_attn_sample",
    )(qm, mkt, mvt)


def _merge_kernel(x_ref, attn_ref, ssm_ref, mem_ref, gate_ref, w_ref, g_ref, b_ref, y_ref):
    branches = jnp.concatenate([r[...].astype(F32) for r in (attn_ref, ssm_ref, mem_ref)], axis=1)
    mixed = branches * jax.nn.silu(gate_ref[...].astype(F32))
    out = jnp.dot(mixed.astype(BF16), w_ref[...], preferred_element_type=F32)
    z = DEEPNORM_ALPHA * x_ref[...] + out
    mu = jnp.mean(z, axis=1, keepdims=True)
    zc = z - mu
    var = jnp.mean(zc * zc, axis=1, keepdims=True)
    y_ref[...] = zc * lax.rsqrt(var + LN_EPS) * g_ref[...] + b_ref[...]


def _merge(x2d, attn, ssm, mem, gate, w_bf, g_row, b_row, tm, ssm_map):
    m = x2d.shape[0]
    row = lambda i: (i, 0)
    const = lambda i: (0, 0)
    return pl.pallas_call(
        _merge_kernel,
        grid=(m // tm,),
        in_specs=[pl.BlockSpec((tm, D_MODEL), row),
                  pl.BlockSpec((tm, D_ATTN), row),
                  pl.BlockSpec((tm, D_SSM), ssm_map),
                  pl.BlockSpec((tm, D_MEM), row),
                  pl.BlockSpec((tm, D_MIX), row),
                  pl.BlockSpec((D_MIX, D_MODEL), const),
                  pl.BlockSpec((1, D_MODEL), const),
                  pl.BlockSpec((1, D_MODEL), const)],
        out_specs=pl.BlockSpec((tm, D_MODEL), row),
        out_shape=jax.ShapeDtypeStruct((m, D_MODEL), F32),
        compiler_params=_vmem_limit(48 * 2**20),
        name="merge",
    )(x2d, attn, ssm, mem, gate, w_bf, g_row, b_row)


_PAGE_RING = 4
_SAMPLE_ROW_GROUPS = 4


def _moba_sample_parts(b, n_seq, pt_ref, q_ref, kn_ref, vn_ref, ck_ref, cv_ref, o_ref,
                       buf_ref, sem_ref, s_ref, p_ref, blk_ref, qbd_ref, pown_ref, l_ref, acc_ref,
                       *, pages_per_chunk, chunks, page_size):
    ppc = pages_per_chunk
    items = 2 * chunks
    t = q_ref.shape[1]
    rows = N_HEADS_A * t
    n_pages = ppc * chunks
    pages_per_block = MOBA_BLOCK // page_size
    n_blocks = n_pages // pages_per_block

    def pad_rows(x):
        return jnp.concatenate([x, jnp.zeros((page_size - x.shape[0], x.shape[1]), x.dtype)], axis=0)

    ring = buf_ref.shape[0]
    ahead = ring - 1
    assert items % ring == 0 and ahead <= items

    def chunk_copies(seq, r):
        src_ref, chunk = (ck_ref, r) if r < chunks else (cv_ref, r - chunks)
        slot = r % ring
        return [pltpu.make_async_copy(src_ref.at[pt_ref[seq, chunk * ppc + i]], buf_ref.at[slot, i],
                                      sem_ref.at[slot]) for i in range(ppc)]

    def prologue():
        @pl.when(b == 0)
        def _first_chunks():
            for r in range(ahead):
                for cp in chunk_copies(b, r):
                    cp.start()
        qbd_ref[...] = _head_rows(q_ref[0] * (HEAD_DIM ** -0.5), N_HEADS_A, HEAD_DIM).astype(BF16)

    def key_chunk(r, slot):
        qbd_bf = qbd_ref[...]
        for i in range(ppc):
            s_ref[r * ppc + i] = jnp.dot(qbd_bf, buf_ref[slot, i].astype(BF16), preferred_element_type=F32)

    def select_and_softmax():
        lo_all = lax.dot_general(qbd_ref[...], pad_rows(kn_ref[0]).astype(BF16), _NT, preferred_element_type=F32)
        gr = rows // _SAMPLE_ROW_GROUPS
        lane = lax.broadcasted_iota(jnp.int32, (gr, page_size), 1)
        t_row = lax.broadcasted_iota(jnp.int32, (gr, page_size), 0) % t
        for g in range(_SAMPLE_ROW_GROUPS):
            rs = pl.ds(g * gr, gr)
            sc = jnp.zeros((gr, page_size), F32)
            for j in range(n_blocks):
                bs = s_ref[j * pages_per_block, rs, :]
                bm = bs
                for pg in range(1, pages_per_block):
                    bs = bs + s_ref[j * pages_per_block + pg, rs, :]
                    bm = jnp.maximum(bm, s_ref[j * pages_per_block + pg, rs, :])
                sc = jnp.where(lane == j, jnp.sum(bs, axis=1, keepdims=True), sc)
                blk_ref[j, rs, :] = bm
            sel = _select_topk(sc, lane < n_blocks, lane.astype(F32), axis=1)
            lo = jnp.where(lane <= t_row, lo_all[g * gr:(g + 1) * gr], NEG_INF)
            mx = lo
            for j in range(n_blocks):
                mask = jnp.broadcast_to(jnp.where(sel[:, j:j + 1], 0.0, NEG_INF), (gr, page_size))
                mx = jnp.maximum(mx, blk_ref[j, rs, :] + mask)
                blk_ref[j, rs, :] = mask
            m = jnp.max(mx, axis=1, keepdims=True)
            po = jnp.exp(lo - m)
            lsum = po
            for j in range(n_blocks):
                shift = blk_ref[j, rs, :] - m
                for pg in range(j * pages_per_block, (j + 1) * pages_per_block):
                    p = jnp.exp(s_ref[pg, rs, :] + shift)
                    lsum = lsum + p
                    p_ref[pg, rs, :] = p.astype(BF16)
            pown_ref[rs, :] = po.astype(BF16)
            l_ref[rs, :] = jnp.broadcast_to(jnp.sum(lsum, axis=1, keepdims=True), (gr, page_size))
        acc_ref[...] = jnp.zeros_like(acc_ref)

    def value_chunk(r, slot):
        acc = acc_ref[...]
        for i in range(ppc):
            acc = acc + lax.dot_general(p_ref[(r - chunks) * ppc + i], buf_ref[slot, i].astype(BF16), _NT,
                                        preferred_element_type=F32)
        acc_ref[...] = acc

    def item(r):
        slot = r % ring
        if r + ahead < items:
            for cp in chunk_copies(b, r + ahead):
                cp.start()
        else:
            @pl.when(b + 1 < n_seq)
            def _next_sequence():
                for cp in chunk_copies(b + 1, r + ahead - items):
                    cp.start()
        for cp in chunk_copies(b, r):
            cp.wait()
        if r < chunks:
            key_chunk(r, slot)
        else:
            if r == chunks:
                select_and_softmax()
            value_chunk(r, slot)

    def finish():
        acc = acc_ref[...] + jnp.dot(pown_ref[...], pad_rows(vn_ref[0]).astype(BF16), preferred_element_type=F32)
        l = jnp.concatenate([l_ref[...]] * (D_ATTN // page_size), axis=1)
        o_ref[0] = _head_diagonal(acc / l, t, N_HEADS_A, HEAD_DIM)

    return prologue, item, finish


def _moba_sample_scratch(t, n_pages, pages_per_chunk, page_size):
    rows = N_HEADS_A * t
    return [pltpu.VMEM((_PAGE_RING, pages_per_chunk, D_ATTN, page_size), F32),
            pltpu.SemaphoreType.DMA((_PAGE_RING,)),
            pltpu.VMEM((n_pages, rows, page_size), F32),
            pltpu.VMEM((n_pages, rows, page_size), BF16),
            pltpu.VMEM((n_pages * page_size // MOBA_BLOCK, rows, page_size), F32),
            pltpu.VMEM((rows, D_ATTN), BF16),
            pltpu.VMEM((rows, page_size), BF16),
            pltpu.VMEM((rows, page_size), F32),
            pltpu.VMEM((rows, D_ATTN), F32)]


_N_PROMPT_SCRATCH = 8


_N_SHARED_PROMPT_SCRATCH = 2


def _moba_kernel(pt_ref, qa_ref, qb_ref, k_ref, v_ref, kmean_ref, qs_ref, kn_ref, vn_ref, ck_ref, cv_ref,
                 oa_ref, ob_ref, os_ref, *scratch, half_tiles, n_seq, pages_per_chunk, chunks, page_size):
    n = pl.program_id(0)
    ti_a = lax.rem(n, half_tiles)
    ti_b = 2 * half_tiles - 1 - ti_a
    shared = scratch[:_N_SHARED_PROMPT_SCRATCH]
    per_tile = _N_PROMPT_SCRATCH - _N_SHARED_PROMPT_SCRATCH
    set_a = scratch[_N_SHARED_PROMPT_SCRATCH:_N_SHARED_PROMPT_SCRATCH + per_tile]
    set_b = scratch[_N_SHARED_PROMPT_SCRATCH + per_tile:_N_SHARED_PROMPT_SCRATCH + 2 * per_tile]
    sample_scratch = scratch[_N_SHARED_PROMPT_SCRATCH + 2 * per_tile:]
    a_setup, a_pairs, a_finish = _moba_prompt_parts(ti_a, qa_ref, k_ref, v_ref, kmean_ref, oa_ref, *shared, *set_a)
    b_setup, b_pairs, b_finish = _moba_prompt_parts(ti_b, qb_ref, k_ref, v_ref, kmean_ref, ob_ref, *shared, *set_b)
    seqs = []
    for i in range(2):
        one = pl.ds(i, 1)
        seqs.append(_moba_sample_parts(
            2 * n + i, n_seq, pt_ref, qs_ref.at[one], kn_ref.at[one], vn_ref.at[one], ck_ref, cv_ref,
            os_ref.at[one], *sample_scratch, pages_per_chunk=pages_per_chunk, chunks=chunks, page_size=page_size))

    a_setup()
    b_setup()
    qb = qa_ref.shape[1] // MOBA_BLOCK
    pairs_a = ti_a * qb // 2
    pairs_b = ti_b * qb // 2
    items = 2 * chunks
    total = 2 * items
    for i, (s_prologue, s_item, s_finish) in enumerate(seqs):
        s_prologue()
        for r in range(items):
            s_item(r)
            g = i * items + r
            a_pairs(g * pairs_a // total, (g + 1) * pairs_a // total)
            b_pairs(g * pairs_b // total, (g + 1) * pairs_b // total)
        s_finish()
    a_finish()
    b_finish()


def _moba(q, kb, vb, kmean, q_s, k_new, v_new, cache_k, cache_v, page_table, pages_per_chunk):
    b, s, _ = q.shape
    nb = s // MOBA_BLOCK
    tq = _MOBA_Q_BLOCKS * MOBA_BLOCK
    assert nb <= LANES // 2 and nb % SUBLANES == 0 and s % (2 * tq) == 0 and _MOBA_Q_BLOCKS % 2 == 0
    head_pairs = D_ATTN // LANES
    half_tiles = s // tq // 2
    n_steps = b * head_pairs * half_tiles

    db, t, _ = q_s.shape
    n_pages = page_table.shape[1]
    page_size = cache_k.shape[2]
    chunks = n_pages // pages_per_chunk
    assert (n_pages * page_size) % MOBA_BLOCK == 0 and MOBA_BLOCK % page_size == 0
    assert n_pages * page_size // MOBA_BLOCK >= MOBA_TOPK and t <= page_size and page_size == LANES
    assert n_pages % pages_per_chunk == 0
    assert db == 2 * n_steps

    def head_of(n):
        return n // (head_pairs * half_tiles), (n // half_tiles) % head_pairs

    def early_q(n, pt):
        bi, hp = head_of(n)
        return (bi, n % half_tiles, hp)

    def late_q(n, pt):
        bi, hp = head_of(n)
        return (bi, 2 * half_tiles - 1 - n % half_tiles, hp)

    def late_o(n, pt):
        bi, hp = head_of(n)
        return (bi, half_tiles - 1 - n % half_tiles, hp)

    def head_map(n, pt):
        bi, hp = head_of(n)
        return (bi, 0, hp)

    seq_map = lambda n, pt: (n, 0, 0)
    prompt_scratch = _moba_prompt_scratch(s, tq)
    grid_spec = pltpu.PrefetchScalarGridSpec(
        num_scalar_prefetch=1,
        grid=(n_steps,),
        in_specs=[pl.BlockSpec((1, tq, LANES), early_q),
                  pl.BlockSpec((1, tq, LANES), late_q),
                  pl.BlockSpec((1, s, LANES), head_map, pipeline_mode=pl.Buffered(1)),
                  pl.BlockSpec((1, s, LANES), head_map, pipeline_mode=pl.Buffered(1)),
                  pl.BlockSpec((1, nb, LANES), head_map)]
                 + [pl.BlockSpec((2, t, D_ATTN), seq_map)] * 3
                 + [pl.BlockSpec(memory_space=pl.ANY)] * 2,
        out_specs=(pl.BlockSpec((1, tq, LANES), early_q), pl.BlockSpec((1, tq, LANES), late_o),
                   pl.BlockSpec((2, t, D_ATTN), seq_map)),
        scratch_shapes=prompt_scratch + prompt_scratch[_N_SHARED_PROMPT_SCRATCH:]
                       + _moba_sample_scratch(t, n_pages, pages_per_chunk, page_size))
    half = jax.ShapeDtypeStruct((b, s // 2, D_ATTN), BF16)
    return pl.pallas_call(
        functools.partial(_moba_kernel, half_tiles=half_tiles, n_seq=db,
                          pages_per_chunk=pages_per_chunk, chunks=chunks, page_size=page_size),
        grid_spec=grid_spec,
        out_shape=(half, half, jax.ShapeDtypeStruct((db, t, D_ATTN), F32)),
        compiler_params=pltpu.CompilerParams(dimension_semantics=("arbitrary",),
                                             vmem_limit_bytes=54 * 2**20),
        name="moba",
    )(page_table, q, q, kb, vb, kmean, q_s, k_new, v_new, cache_k, cache_v)


def kernel(x_prompt, x_sample, mem_prompt, cache_k, cache_v, state_ssm_re, state_ssm_im,
           cache_mem_k, cache_mem_v, page_table, w_in, w_mem_kv, a_re, a_im, log_dt,
           b_re, b_im, c_re, c_im, d_skip, w_glu, b_glu, w_out, ln_g, ln_b):
    batch, seq, _ = x_prompt.shape
    db, t_new, _ = x_sample.shape
    n_mem = mem_prompt.shape[1]
    n_phys, page_size = cache_k.shape[:2]
    n_seg = SSM_CHAINS // batch
    seg_len = seq // n_seg
    tm = min(512, seg_len)

    w_in_bf = w_in.astype(BF16)
    w_mem_bf = w_mem_kv.astype(BF16)
    w_out_bf = w_out.astype(BF16)
    wg_bf = w_glu.astype(BF16)
    bg_row = b_glu.astype(F32).reshape(1, 2 * D_SSM)
    d_row = d_skip.astype(F32).reshape(1, D_SSM)
    g_row = ln_g.astype(F32).reshape(1, D_MODEL)
    b_row = ln_b.astype(F32).reshape(1, D_MODEL)
    a_row, b_mat, c_mat = _ssm_operands(a_re, a_im, log_dt, b_re, b_im, c_re, c_im)
    b_bf = b_mat.astype(BF16)
    c_bf = c_mat.astype(BF16)

    x2d = x_prompt.reshape(batch * seq, D_MODEL)
    q, k_t, v_t, u_chain, qm, gate, kb, vb, kmean = _in_proj_prompt(x2d, w_in_bf, batch, seq, tm)
    xs2d = x_sample.reshape(db * t_new, D_MODEL)
    tms = min(512, db * t_new)
    q_s, k_s, v_s, u_s, qm_s, gate_s = _in_proj_sample(xs2d, w_in_bf, tms)
    attn_early, attn_late, attn_s = _moba(
                         q.reshape(batch, seq, D_ATTN), kb.reshape(batch, seq, D_ATTN),
                         vb.reshape(batch, seq, D_ATTN), kmean.reshape(batch, seq // MOBA_BLOCK, D_ATTN),
                         q_s.reshape(db, t_new, D_ATTN), k_s.reshape(db, t_new, D_ATTN),
                         v_s.reshape(db, t_new, D_ATTN),
                         cache_k.transpose(0, 2, 3, 1).reshape(n_phys, D_ATTN, page_size),
                         cache_v.transpose(0, 2, 3, 1).reshape(n_phys, D_ATTN, page_size), page_table,
                         pages_per_chunk=min(16, page_table.shape[1] // 2))
    attn = jnp.concatenate([attn_early, attn_late], axis=1)

    y_chain, hfin = _ssm_prompt(u_chain.reshape(seg_len * SSM_CHAINS, D_SSM), a_row, b_bf, c_bf, d_row,
                                wg_bf, bg_row, n_seg, steps_per_chunk=min(64, seg_len))
    mk, mv = _mem_kv(mem_prompt.reshape(batch * n_mem, D_MODEL), w_mem_bf)
    mem_o = _mem_attn_prompt(qm, mk, mv, batch, seq, tm)
    tiles_per_seq = seq // tm
    tiles_per_seg = seg_len // tm

    def chain_map(i):
        b = i // tiles_per_seq
        ti = i % tiles_per_seq
        return (ti % tiles_per_seg, b * n_seg + ti // tiles_per_seg)

    y_prompt = _merge(x2d, attn.reshape(batch * seq, D_ATTN), y_chain.reshape(seg_len, SSM_CHAINS * D_SSM),
                      mem_o, gate, w_out_bf, g_row, b_row, tm, chain_map)
    h_last = hfin[n_seg - 1::n_seg]
    ssm_re_p = h_last[:, :SSM_LANES].reshape(batch, SSM_GROUPS, SSM_STATE)
    ssm_im_p = h_last[:, SSM_LANES:].reshape(batch, SSM_GROUPS, SSM_STATE)

    h0 =jnp.concatenate([state_ssm_re.astype(F32).reshape(db, SSM_LANES),
                          state_ssm_im.astype(F32).reshape(db, SSM_LANES)], axis=1)
    y_tb, hfin_s = _ssm_sample(u_s.reshape(db, t_new, D_SSM).transpose(1, 0, 2), h0, a_row, b_mat, c_bf,
                               d_row, wg_bf, bg_row)
    ssm_s = y_tb.transpose(1, 0, 2).reshape(db * t_new, D_SSM)
    mem_s = _mem_attn_sample(qm_s.reshape(db, t_new, D_MEM),
                             cache_mem_k.transpose(0, 2, 3, 1).reshape(db, D_MEM, n_mem),
                             cache_mem_v.transpose(0, 2, 3, 1).reshape(db, D_MEM, n_mem), group=min(8, db))
    y_sample = _merge(xs2d, attn_s.reshape(db * t_new, D_ATTN), ssm_s, mem_s.reshape(db * t_new, D_MEM),
                      gate_s, w_out_bf, g_row, b_row, tms, lambda i: (i, 0))

    heads = (N_HEADS_A, HEAD_DIM)
    mem_heads = (MEM_HEADS, D_MEM // MEM_HEADS)
    return (y_prompt.reshape(batch, seq, D_MODEL), y_sample.reshape(db, t_new, D_MODEL),
            k_t.reshape(batch, *heads, seq).transpose(0, 3, 1, 2),
            v_t.reshape(batch, *heads, seq).transpose(0, 3, 1, 2), ssm_re_p, ssm_im_p,
            mk.reshape(batch, n_mem, *mem_heads), mv.reshape(batch, n_mem, *mem_heads),
            k_s.reshape(db, t_new, *heads), v_s.reshape(db, t_new, *heads),
            hfin_s[:, :SSM_LANES].reshape(db, SSM_GROUPS, SSM_STATE),
            hfin_s[:, SSM_LANES:].reshape(db, SSM_GROUPS, SSM_STATE))
```

```python
import functools
import math

import jax
import jax.numpy as jnp
from jax import lax
from jax.experimental import pallas as pl
from jax.experimental.pallas import tpu as pltpu

F32 = jnp.float32
BF16 = jnp.bfloat16

D_MODEL = 1024
D_ATTN = 512
D_SSM = 256
D_MEM = 256
D_MIX = D_ATTN + D_SSM + D_MEM
HEAD_DIM = 64
N_HEADS_A = D_ATTN // HEAD_DIM
MOBA_BLOCK = 256
MOBA_TOPK = 3
SSM_CH = 16
SSM_GROUPS = D_SSM // SSM_CH
SSM_STATE = 64
SSM_LANES = SSM_GROUPS * SSM_STATE
MEM_HEADS = 4
D_IN = 3 * D_ATTN + D_SSM + D_MEM + D_MIX
DEPTH = 1
DEEPNORM_ALPHA = (2.0 * DEPTH) ** 0.25
LN_EPS = 1e-5
NEG_INF = -1e30

LANES = 128
SUBLANES = 8
SSM_CHAINS = SUBLANES

_NT = (((1,), (1,)), ((), ()))


def _vmem_limit(nbytes):
    return pltpu.CompilerParams(vmem_limit_bytes=int(nbytes))


def _proj_kernel(x_ref, w_ref, *out_refs, with_prompt_extras):
    xb = x_ref[...].astype(BF16)

    def seg(lo, hi):
        return jnp.dot(xb, w_ref[:, lo:hi], preferred_element_type=F32)

    c0, c1, c2, c3, c4 = D_ATTN, 2 * D_ATTN, 3 * D_ATTN, 3 * D_ATTN + D_SSM, 3 * D_ATTN + D_SSM + D_MEM
    if with_prompt_extras:
        q_ref, k_ref, v_ref, u_ref, qm_ref, gate_ref, kb_ref, vb_ref, kmean_ref = out_refs
    else:
        q_ref, k_ref, v_ref, u_ref, qm_ref, gate_ref = out_refs
    q_ref[...] = seg(0, c0)
    k = seg(c0, c1)
    v = seg(c1, c2)
    u_ref[...] = seg(c2, c3)
    qm_ref[...] = seg(c3, c4).astype(qm_ref.dtype)
    gate_ref[...] = seg(c4, D_IN).astype(gate_ref.dtype)
    if not with_prompt_extras:
        k_ref[...] = k
        v_ref[...] = v
    else:
        k_ref[0] = k.T
        v_ref[0] = v.T
        kb_ref[...] = k.astype(BF16)
        vb_ref[...] = v.astype(BF16)
        tm = k.shape[0]
        kmean_ref[0] = jnp.sum(k.reshape(tm // MOBA_BLOCK, MOBA_BLOCK, D_ATTN), axis=1) * (1.0 / MOBA_BLOCK)


def _in_proj_prompt(x2d, w_bf, batch, seq, tm):
    m = x2d.shape[0]
    n_seg = SSM_CHAINS // batch
    seg_len = seq // n_seg
    tiles_per_seq = seq // tm
    tiles_per_seg = seg_len // tm
    row = lambda i: (i, 0)

    def u_map(i):
        b = i // tiles_per_seq
        t = i % tiles_per_seq
        return (t % tiles_per_seg, b * n_seg + t // tiles_per_seg)

    out_shape = (
        jax.ShapeDtypeStruct((m, D_ATTN), F32),
        jax.ShapeDtypeStruct((batch, D_ATTN, seq), F32),
        jax.ShapeDtypeStruct((batch, D_ATTN, seq), F32),
        jax.ShapeDtypeStruct((seg_len, SSM_CHAINS * D_SSM), F32),
        jax.ShapeDtypeStruct((m, D_MEM), BF16),
        jax.ShapeDtypeStruct((m, D_MIX), BF16),
        jax.ShapeDtypeStruct((m, D_ATTN), BF16),
        jax.ShapeDtypeStruct((m, D_ATTN), BF16),
        jax.ShapeDtypeStruct((m // tm, tm // MOBA_BLOCK, D_ATTN), F32),
    )
    kv_t = lambda i: (i // tiles_per_seq, 0, i % tiles_per_seq)
    out_specs = (
        pl.BlockSpec((tm, D_ATTN), row),
        pl.BlockSpec((1, D_ATTN, tm), kv_t),
        pl.BlockSpec((1, D_ATTN, tm), kv_t),
        pl.BlockSpec((tm, D_SSM), u_map),
        pl.BlockSpec((tm, D_MEM), row),
        pl.BlockSpec((tm, D_MIX), row),
        pl.BlockSpec((tm, D_ATTN), row),
        pl.BlockSpec((tm, D_ATTN), row),
        pl.BlockSpec((1, tm // MOBA_BLOCK, D_ATTN), lambda i: (i, 0, 0)),
    )
    return pl.pallas_call(
        functools.partial(_proj_kernel, with_prompt_extras=True),
        grid=(m // tm,),
        in_specs=[pl.BlockSpec((tm, D_MODEL), row),
                  pl.BlockSpec((D_MODEL, D_IN), lambda i: (0, 0), pipeline_mode=pl.Buffered(1))],
        out_specs=out_specs,
        out_shape=out_shape,
        compiler_params=_vmem_limit(52 * 2**20),
        name="in_proj_prompt",
    )(x2d, w_bf)


def _in_proj_sample(x2d, w_bf, tm):
    m = x2d.shape[0]
    row = lambda i: (i, 0)
    out_shape = (
        jax.ShapeDtypeStruct((m, D_ATTN), F32),
        jax.ShapeDtypeStruct((m, D_ATTN), F32),
        jax.ShapeDtypeStruct((m, D_ATTN), F32),
        jax.ShapeDtypeStruct((m, D_SSM), F32),
        jax.ShapeDtypeStruct((m, D_MEM), BF16),
        jax.ShapeDtypeStruct((m, D_MIX), BF16),
    )
    widths = (D_ATTN, D_ATTN, D_ATTN, D_SSM, D_MEM, D_MIX)
    return pl.pallas_call(
        functools.partial(_proj_kernel, with_prompt_extras=False),
        grid=(m // tm,),
        in_specs=[pl.BlockSpec((tm, D_MODEL), row),
                  pl.BlockSpec((D_MODEL, D_IN), lambda i: (0, 0), pipeline_mode=pl.Buffered(1))],
        out_specs=tuple(pl.BlockSpec((tm, w), row) for w in widths),
        out_shape=out_shape,
        compiler_params=_vmem_limit(52 * 2**20),
        name="in_proj_sample",
    )(x2d, w_bf)


def _mem_kv_kernel(mem_ref, w_ref, mk_ref, mv_ref):
    kv = jnp.dot(mem_ref[...].astype(BF16), w_ref[...], preferred_element_type=F32)
    mk_ref[...] = kv[:, :D_MEM]
    mv_ref[...] = kv[:, D_MEM:]


def _mem_kv(mem2d, w_bf):
    m = mem2d.shape[0]
    tm = min(m, 512)
    row = lambda i: (i, 0)
    return pl.pallas_call(
        _mem_kv_kernel,
        grid=(m // tm,),
        in_specs=[pl.BlockSpec((tm, D_MODEL), row),
                  pl.BlockSpec((D_MODEL, 2 * D_MEM), lambda i: (0, 0))],
        out_specs=(pl.BlockSpec((tm, D_MEM), row), pl.BlockSpec((tm, D_MEM), row)),
        out_shape=(jax.ShapeDtypeStruct((m, D_MEM), F32), jax.ShapeDtypeStruct((m, D_MEM), F32)),
        name="mem_kv",
    )(mem2d, w_bf)


def _select_topk(scores, valid, index, axis):
    lowest = jnp.finfo(F32).min
    s = jnp.where(valid, scores, lowest)
    sel = jnp.zeros(scores.shape, dtype=jnp.bool_)
    for _ in range(MOBA_TOPK):
        m = jnp.max(s, axis=axis, keepdims=True)
        first = jnp.min(jnp.where(s == m, index, jnp.float32(2**30)), axis=axis, keepdims=True)
        pick = index == first
        sel = sel | (pick & valid)
        s = jnp.where(pick, lowest, s)
    return sel


_MOBA_Q_BLOCKS = 2
_MOBA_ROW_GROUP = 128


def _moba_prompt_parts(ti, q_ref, k_ref, v_ref, kmean_ref, o_ref,
                       kaug_ref, vaug_ref, lhs_ref, s_ref, p_ref, alpha_ref, m_ref, acc_ref):
    blk = MOBA_BLOCK
    tq = q_ref.shape[1]
    qb = tq // blk
    nb = k_ref.shape[1] // blk
    half = LANES // 2

    def build_operands():
        lane_b = lax.broadcasted_iota(jnp.int32, (blk, LANES), 1)
        head0_b = lane_b < half

        def body(j, carry):
            rows = pl.ds(pl.multiple_of(j * blk, blk), blk)
            kb = k_ref[0, rows, :].astype(F32)
            vb = v_ref[0, rows, :].astype(F32)
            hot0 = jnp.where(lane_b == half + j, 1.0, 0.0)
            hot1 = jnp.where(lane_b == j, 1.0, 0.0)
            kaug_ref[0, rows, :] = jnp.where(head0_b, kb, hot0).astype(BF16)
            kaug_ref[1, rows, :] = jnp.where(head0_b, hot1, kb).astype(BF16)
            vaug_ref[0, rows, :] = jnp.where(head0_b, vb, 1.0).astype(BF16)
            vaug_ref[1, rows, :] = jnp.where(head0_b, 1.0, vb).astype(BF16)
            return carry
        lax.fori_loop(0, nb, body, 0)

    def select_blocks():
        q2 = q_ref[0]
        means = kmean_ref[0]
        lane_m = lax.broadcasted_iota(jnp.int32, (nb, LANES), 1)
        pad = [jnp.zeros((half - nb, LANES), F32)] if nb < half else []
        means_p = jnp.concatenate([jnp.where(lane_m >= half, means, 0.0)] + pad
                                  + [jnp.where(lane_m < half, means, 0.0)] + pad, axis=0)
        sc_t = lax.dot_general(means_p, q2, _NT, precision=lax.Precision.HIGHEST,
                               preferred_element_type=F32)
        slot = lax.broadcasted_iota(jnp.int32, (nb, tq), 0)
        slot_f = slot.astype(F32)
        n_full = ti * qb + lax.broadcasted_iota(jnp.int32, (nb, tq), 1) // blk
        bias_t = []
        for off in (0, half):
            sel = _select_topk(sc_t[off:off + nb], slot < n_full, slot_f, axis=0)
            bias_t.append(jnp.where(sel | (slot == n_full), 0.0, NEG_INF))
            if nb < half:
                bias_t.append(jnp.zeros((half - nb, tq), F32))
        bias = jnp.concatenate(bias_t, axis=0).T
        head0_q = lax.broadcasted_iota(jnp.int32, (tq, LANES), 1) < half
        qs = q2 * (HEAD_DIM ** -0.5)
        lhs_ref[0] = jnp.where(head0_q, qs, bias).astype(BF16)
        lhs_ref[1] = jnp.where(head0_q, bias, qs).astype(BF16)
        m_ref[...] = jnp.full(m_ref.shape, NEG_INF, F32)
        acc_ref[...] = jnp.zeros_like(acc_ref)

    rg = _MOBA_ROW_GROUP
    c_i = lax.broadcasted_iota(jnp.int32, (rg, blk), 1)
    r_i = lax.broadcasted_iota(jnp.int32, (rg, blk), 0)

    def key_rows(j):
        return pl.ds(j * blk if isinstance(j, int) else pl.multiple_of(j * blk, blk), blk)

    def score_block(j, slot):
        for e in (0, 1):
            s_ref[slot, e] = lax.dot_general(lhs_ref[e], kaug_ref[e, key_rows(j), :], _NT,
                                             preferred_element_type=F32)

    def softmax_pv_block(j, slot, causal):
        for e in (0, 1):
            for g in range(tq // rg):
                rows = pl.ds(g * rg, rg)
                s = s_ref[slot, e, rows, :]
                if causal:
                    s = jnp.where(j * blk + c_i <= ti * tq + g * rg + r_i, s, NEG_INF)
                m_old = m_ref[e, rows, :]
                m_new = jnp.maximum(m_old, jnp.max(s, axis=1, keepdims=True))
                alpha_ref[e, rows, :] = jnp.exp(m_old - m_new)
                p_ref[e, rows, :] = jnp.exp(s - jnp.concatenate([m_new] * (blk // LANES), axis=1)).astype(BF16)
                m_ref[e, rows, :] = m_new
            pv = jnp.dot(p_ref[e], vaug_ref[e, key_rows(j), :], preferred_element_type=F32)
            acc_ref[e] = alpha_ref[e] * acc_ref[e] + pv

    n_past = ti * qb

    def setup():
        pl.when(ti == 0)(build_operands)
        select_blocks()
        score_block(0, 0)

    def past_pair(jj, carry):
        j = 2 * jj
        score_block(j + 1, 1)
        softmax_pv_block(j, 0, causal=False)
        score_block(j + 2, 0)
        softmax_pv_block(j + 1, 1, causal=False)
        return carry

    def past_pairs(lo, hi):
        lax.fori_loop(lo, hi, past_pair, 0)

    def finish():
        for d in range(qb):
            if d + 1 < qb:
                score_block(n_past + d + 1, (d + 1) % 2)
            softmax_pv_block(n_past + d, d % 2, causal=True)
        a0 = acc_ref[0]
        a1 = acc_ref[1]
        o0 = a0 / pltpu.roll(a0, half, axis=1)
        o1 = a1 / pltpu.roll(a1, half, axis=1)
        head0_q = lax.broadcasted_iota(jnp.int32, (tq, LANES), 1) < half
        o_ref[0] = jnp.where(head0_q, o0, o1).astype(o_ref.dtype)

    return setup, past_pairs, finish


def _moba_prompt_scratch(s, tq):
    return [pltpu.VMEM((2, s, LANES), BF16),
            pltpu.VMEM((2, s, LANES), BF16),
            pltpu.VMEM((2, tq, LANES), BF16),
            pltpu.VMEM((2, 2, tq, MOBA_BLOCK), F32),
            pltpu.VMEM((2, tq, MOBA_BLOCK), BF16),
            pltpu.VMEM((2, tq, LANES), F32),
            pltpu.VMEM((2, tq, LANES), F32),
            pltpu.VMEM((2, tq, LANES), F32)]


def _ssm_disc_kernel(a_re_ref, a_im_ref, log_dt_ref, bt_re_ref, bt_im_ref,
                     abar_re_ref, abar_im_ref, bbt_re_ref, bbt_im_ref):
    a_re = a_re_ref[...]
    a_im = a_im_ref[...]
    dt = jnp.exp(log_dt_ref[...])
    mag = jnp.exp(a_re * dt)
    ang = a_im * dt
    abar_re = mag * jnp.cos(ang)
    abar_im = mag * jnp.sin(ang)
    den = a_re * a_re + a_im * a_im
    f_re = ((abar_re - 1.0) * a_re + abar_im * a_im) / den
    f_im = (abar_im * a_re - (abar_re - 1.0) * a_im) / den
    abar_re_ref[...] = abar_re
    abar_im_ref[...] = abar_im
    bt_re = bt_re_ref[...]
    bt_im = bt_im_ref[...]
    bbt_re_ref[...] = f_re[:, None, :] * bt_re - f_im[:, None, :] * bt_im
    bbt_im_ref[...] = f_re[:, None, :] * bt_im + f_im[:, None, :] * bt_re


def _ssm_operands(a_re, a_im, log_dt, b_re, b_im, c_re, c_im):
    g, p, c = b_re.shape
    abar_re, abar_im, bbt_re, bbt_im = pl.pallas_call(
        _ssm_disc_kernel,
        out_shape=(jax.ShapeDtypeStruct((g, p), F32), jax.ShapeDtypeStruct((g, p), F32),
                   jax.ShapeDtypeStruct((g, c, p), F32), jax.ShapeDtypeStruct((g, c, p), F32)),
        name="ssm_discretize",
    )(a_re.astype(F32), a_im.astype(F32), log_dt.astype(F32).reshape(g, 1),
      b_re.astype(F32).transpose(0, 2, 1), b_im.astype(F32).transpose(0, 2, 1))
    eye = jnp.eye(g, dtype=F32)

    def in_block_diag(bbt):
        return jnp.einsum('gcp,gh->gchp', bbt, eye).reshape(g * c, g * p)

    def out_block_diag(cm):
        return jnp.einsum('gcp,gh->gphc', cm, eye).reshape(g * p, g * c)

    b_mat = jnp.concatenate([in_block_diag(bbt_re), in_block_diag(bbt_im)], axis=1)
    c_mat = jnp.concatenate([out_block_diag(c_re.astype(F32)),
                             -out_block_diag(c_im.astype(F32))], axis=0)
    a_row = jnp.concatenate([abar_re.reshape(1, g * p), abar_im.reshape(1, g * p)], axis=1)
    return a_row, b_mat, c_mat


def _complex_step(a_re, a_im, h_re, h_im, bu_re, bu_im):
    return (a_re * h_re - a_im * h_im + bu_re, a_re * h_im + a_im * h_re + bu_im)


def _ssm_output(h_all, u, c_ref, d_ref, wg_ref, bg_ref):
    y = jnp.dot(h_all.astype(BF16), c_ref[...], preferred_element_type=F32) + d_ref[...] * u
    y = jax.nn.gelu(y)
    z = jnp.dot(y.astype(BF16), wg_ref[...], preferred_element_type=F32) + bg_ref[...]
    return z[:, :D_SSM] * jax.nn.sigmoid(z[:, D_SSM:])


def _ssm_prompt_kernel(u_ref, a_ref, b_ref, c_ref, d_ref, wg_ref, bg_ref, y_ref, hfin_ref,
                       hb_ref, h_ref, init_ref, *, n_seg):
    pss = pl.program_id(0)
    c = pl.program_id(1)
    n_chunks = pl.num_programs(1)
    rows = u_ref.shape[0]
    steps = rows // SSM_CHAINS
    n = SSM_LANES
    a_re = jnp.broadcast_to(a_ref[:, :n], (SSM_CHAINS, n))
    a_im = jnp.broadcast_to(a_ref[:, n:], (SSM_CHAINS, n))

    @pl.when((pss == 0) & (c == 0))
    def _zero_state():
        h_ref[...] = jnp.zeros_like(h_ref)

    @pl.when((pss == 1) & (c == 0))
    def _true_initial_state():
        h_ref[...] = init_ref[...]

    u = u_ref[...]
    hb_ref[...] = jnp.dot(u.astype(BF16), b_ref[...], preferred_element_type=F32)

    def scan(store):
        def step(s, carry):
            h_re, h_im = carry
            r = pl.ds(pl.multiple_of(s * SSM_CHAINS, SSM_CHAINS), SSM_CHAINS)
            h_re, h_im = _complex_step(a_re, a_im, h_re, h_im, hb_ref[r, :n], hb_ref[r, n:])
            if store:
                hb_ref[r, :n] = h_re
                hb_ref[r, n:] = h_im
            return h_re, h_im
        h_re, h_im = lax.fori_loop(0, steps, step, (h_ref[:, :n], h_ref[:, n:]), unroll=4)
        h_ref[:, :n] = h_re
        h_ref[:, n:] = h_im

    @pl.when(pss == 0)
    def _local_pass():
        scan(False)

    @pl.when((pss == 0) & (c == n_chunks - 1))
    def _segment_initial_states():
        seg_len = steps * n_chunks
        p_re, p_im = a_re, a_im
        for _ in range(int(math.log2(seg_len))):
            p_re, p_im = p_re * p_re - p_im * p_im, 2.0 * p_re * p_im
        e_re = h_ref[:, :n]
        e_im = h_ref[:, n:]
        first = (lax.broadcasted_iota(jnp.int32, (SSM_CHAINS, n), 0) % n_seg) == 0
        i_re = jnp.zeros((SSM_CHAINS, n), F32)
        i_im = jnp.zeros((SSM_CHAINS, n), F32)
        for _ in range(n_seg - 1):
            t_re = p_re * i_re - p_im * i_im + e_re
            t_im = p_re * i_im + p_im * i_re + e_im
            i_re = jnp.where(first, 0.0, pltpu.roll(t_re, 1, axis=0))
            i_im = jnp.where(first, 0.0, pltpu.roll(t_im, 1, axis=0))
        init_ref[:, :n] = i_re
        init_ref[:, n:] = i_im

    @pl.when(pss == 1)
    def _output_pass():
        scan(True)
        y_ref[...] = _ssm_output(hb_ref[...], u, c_ref, d_ref, wg_ref, bg_ref).astype(y_ref.dtype)

    @pl.when((pss == 1) & (c == n_chunks - 1))
    def _final_state():
        hfin_ref[...] = h_ref[...]


def _ssm_prompt(u_chain, a_row, b_bf, c_bf, d_row, wg_bf, bg_row, n_seg, steps_per_chunk):
    total = u_chain.shape[0]
    rows = steps_per_chunk * SSM_CHAINS
    n_chunks = total // rows
    seg_len = total // SSM_CHAINS
    assert 2 ** int(math.log2(seg_len)) == seg_len
    const = lambda p, c: (0, 0)
    return pl.pallas_call(
        functools.partial(_ssm_prompt_kernel, n_seg=n_seg),
        grid=(2, n_chunks),
        in_specs=[pl.BlockSpec((rows, D_SSM), lambda p, c: (c, 0)),
                  pl.BlockSpec((1, 2 * SSM_LANES), const),
                  pl.BlockSpec((D_SSM, 2 * SSM_LANES), const),
                  pl.BlockSpec((2 * SSM_LANES, D_SSM), const),
                  pl.BlockSpec((1, D_SSM), const),
                  pl.BlockSpec((D_SSM, 2 * D_SSM), const),
                  pl.BlockSpec((1, 2 * D_SSM), const)],
        out_specs=(pl.BlockSpec((rows, D_SSM), lambda p, c: (p * c, 0)),
                   pl.BlockSpec((SSM_CHAINS, 2 * SSM_LANES), const)),
        out_shape=(jax.ShapeDtypeStruct((total, D_SSM), BF16),
                   jax.ShapeDtypeStruct((SSM_CHAINS, 2 * SSM_LANES), F32)),
        scratch_shapes=[pltpu.VMEM((rows, 2 * SSM_LANES), F32),
                        pltpu.VMEM((SSM_CHAINS, 2 * SSM_LANES), F32),
                        pltpu.VMEM((SSM_CHAINS, 2 * SSM_LANES), F32)],
        compiler_params=pltpu.CompilerParams(dimension_semantics=("arbitrary", "arbitrary"),
                                             vmem_limit_bytes=40 * 2**20),
        name="ssm_prompt",
    )(u_chain, a_row, b_bf, c_bf, d_row, wg_bf, bg_row)


def _ssm_sample_kernel(u_ref, h0_ref, a_ref, b_ref, c_ref, d_ref, wg_ref, bg_ref, y_ref, hfin_ref):
    steps, batch, _ = u_ref.shape
    n = SSM_LANES
    a_re = jnp.broadcast_to(a_ref[:, :n], (batch, n))
    a_im = jnp.broadcast_to(a_ref[:, n:], (batch, n))
    h_re = h0_ref[:, :n]
    h_im = h0_ref[:, n:]
    for t in range(steps):
        u = u_ref[t]
        bu = jnp.dot(u, b_ref[...], precision=lax.Precision.HIGHEST, preferred_element_type=F32)
        h_re, h_im = _complex_step(a_re, a_im, h_re, h_im, bu[:, :n], bu[:, n:])
        h_all = jnp.concatenate([h_re, h_im], axis=1)
        y_ref[t] = _ssm_output(h_all, u, c_ref, d_ref, wg_ref, bg_ref)
    hfin_ref[:, :n] = h_re
    hfin_ref[:, n:] = h_im


def _ssm_sample(u_tb, h0, a_row, b_f32, c_bf, d_row, wg_bf, bg_row):
    steps, batch, _ = u_tb.shape
    return pl.pallas_call(
        _ssm_sample_kernel,
        out_shape=(jax.ShapeDtypeStruct((steps, batch, D_SSM), F32),
                   jax.ShapeDtypeStruct((batch, 2 * SSM_LANES), F32)),
        compiler_params=_vmem_limit(40 * 2**20),
        name="ssm_sample",
    )(u_tb, h0, a_row, b_f32, c_bf, d_row, wg_bf, bg_row)


def _mem_attn_prompt_kernel(q_ref, mk_ref, mv_ref, o_ref):
    rows = q_ref.shape[0]
    half = LANES // 2
    scale = (D_MEM // MEM_HEADS) ** -0.5
    head0 = lax.broadcasted_iota(jnp.int32, (rows, LANES), 1) < half
    outs = []
    for hp in range(D_MEM // LANES):
        cols = slice(hp * LANES, (hp + 1) * LANES)
        q2 = q_ref[:, cols].astype(F32)
        k2 = mk_ref[:, cols].astype(BF16)
        v2 = mv_ref[:, cols].astype(BF16)
        res = []
        for e in (0, 1):
            mine = head0 if e == 0 else jnp.logical_not(head0)
            lhs = jnp.where(mine, q2, 0.0).astype(BF16)
            s = lax.dot_general(lhs, k2, _NT, preferred_element_type=F32) * scale
            p = jnp.exp(s - jnp.max(s, axis=1, keepdims=True))
            l = jnp.sum(p, axis=1, keepdims=True)
            res.append(jnp.dot(p.astype(BF16), v2, preferred_element_type=F32) / l)
        outs.append(jnp.where(head0, res[0], res[1]))
    o_ref[...] = jnp.concatenate(outs, axis=1).astype(o_ref.dtype)


def _mem_attn_prompt(qm, mk, mv, batch, seq, tm):
    n_mem = mk.shape[0] // batch
    tiles = seq // tm
    return pl.pallas_call(
        _mem_attn_prompt_kernel,
        grid=(batch, tiles),
        in_specs=[pl.BlockSpec((tm, D_MEM), lambda b, i: (b * tiles + i, 0)),
                  pl.BlockSpec((n_mem, D_MEM), lambda b, i: (b, 0)),
                  pl.BlockSpec((n_mem, D_MEM), lambda b, i: (b, 0))],
        out_specs=pl.BlockSpec((tm, D_MEM), lambda b, i: (b * tiles + i, 0)),
        out_shape=jax.ShapeDtypeStruct((batch * seq, D_MEM), BF16),
        name="mem_attn_prompt",
    )(qm, mk, mv)


def _head_rows(x, n_heads, head_dim):
    head_of_lane = lax.broadcasted_iota(jnp.int32, x.shape, 1) // head_dim
    return jnp.concatenate([jnp.where(head_of_lane == h, x, 0.0) for h in range(n_heads)], axis=0)


def _head_diagonal(o_full, t, n_heads, head_dim):
    head_of_lane = lax.broadcasted_iota(jnp.int32, (t, o_full.shape[1]), 1) // head_dim
    out = jnp.zeros((t, o_full.shape[1]), F32)
    for h in range(n_heads):
        out = out + jnp.where(head_of_lane == h, o_full[h * t:(h + 1) * t], 0.0)
    return out


def _mem_attn_sample_kernel(q_ref, mkt_ref, mvt_ref, o_ref):
    group, t, _ = q_ref.shape
    head_dim = D_MEM // MEM_HEADS
    scale = head_dim ** -0.5
    for g in range(group):
        qbd = _head_rows(q_ref[g].astype(F32) * scale, MEM_HEADS, head_dim).astype(BF16)
        s = jnp.dot(qbd, mkt_ref[g].astype(BF16), preferred_element_type=F32)
        p = jnp.exp(s - jnp.max(s, axis=1, keepdims=True))
        l = jnp.sum(p, axis=1, keepdims=True)
        o_full = lax.dot_general(p.astype(BF16), mvt_ref[g].astype(BF16), _NT, preferred_element_type=F32) / l
        o_ref[g] = _head_diagonal(o_full, t, MEM_HEADS, head_dim)


def _mem_attn_sample(qm, mkt, mvt, group):
    db, t, _ = qm.shape
    n_mem = mkt.shape[2]
    blk = lambda i: (i, 0, 0)
    return pl.pallas_call(
        _mem_attn_sample_kernel,
        grid=(db // group,),
        in_specs=[pl.BlockSpec((group, t, D_MEM), blk),
                  pl.BlockSpec((group, D_MEM, n_mem), blk),
                  pl.BlockSpec((group, D_MEM, n_mem), blk)],
        out_specs=pl.BlockSpec((group, t, D_MEM), blk),
        out_shape=jax.ShapeDtypeStruct((db, t, D_MEM), F32),
        name="mem_attn_sample",
    )(qm, mkt, mvt)


def _merge_kernel(x_ref, attn_ref, ssm_ref, mem_ref, gate_ref, w_ref, g_ref, b_ref, y_ref):
    branches = jnp.concatenate([r[...].astype(F32) for r in (attn_ref, ssm_ref, mem_ref)], axis=1)
    mixed = branches * jax.nn.silu(gate_ref[...].astype(F32))
    out = jnp.dot(mixed.astype(BF16), w_ref[...], preferred_element_type=F32)
    z = DEEPNORM_ALPHA * x_ref[...] + out
    mu = jnp.mean(z, axis=1, keepdims=True)
    zc = z - mu
    var = jnp.mean(zc * zc, axis=1, keepdims=True)
    y_ref[...] = zc * lax.rsqrt(var + LN_EPS) * g_ref[...] + b_ref[...]


def _merge(x2d, attn, ssm, mem, gate, w_bf, g_row, b_row, tm, ssm_map):
    m = x2d.shape[0]
    row = lambda i: (i, 0)
    const = lambda i: (0, 0)
    return pl.pallas_call(
        _merge_kernel,
        grid=(m // tm,),
        in_specs=[pl.BlockSpec((tm, D_MODEL), row),
                  pl.BlockSpec((tm, D_ATTN), row),
                  pl.BlockSpec((tm, D_SSM), ssm_map),
                  pl.BlockSpec((tm, D_MEM), row),
                  pl.BlockSpec((tm, D_MIX), row),
                  pl.BlockSpec((D_MIX, D_MODEL), const),
                  pl.BlockSpec((1, D_MODEL), const),
                  pl.BlockSpec((1, D_MODEL), const)],
        out_specs=pl.BlockSpec((tm, D_MODEL), row),
        out_shape=jax.ShapeDtypeStruct((m, D_MODEL), F32),
        compiler_params=_vmem_limit(48 * 2**20),
        name="merge",
    )(x2d, attn, ssm, mem, gate, w_bf, g_row, b_row)


_PAGE_RING = 4
_SAMPLE_ROW_GROUPS = 4


def _moba_sample_parts(b, n_seq, pt_ref, q_ref, kn_ref, vn_ref, ck_ref, cv_ref, o_ref,
                       buf_ref, sem_ref, s_ref, p_ref, blk_ref, qbd_ref, pown_ref, l_ref, acc_ref,
                       *, pages_per_chunk, chunks, page_size):
    ppc = pages_per_chunk
    items = 2 * chunks
    t = q_ref.shape[1]
    rows = N_HEADS_A * t
    n_pages = ppc * chunks
    pages_per_block = MOBA_BLOCK // page_size
    n_blocks = n_pages // pages_per_block

    def pad_rows(x):
        return jnp.concatenate([x, jnp.zeros((page_size - x.shape[0], x.shape[1]), x.dtype)], axis=0)

    ring = buf_ref.shape[0]
    ahead = ring - 1
    assert items % ring == 0 and ahead <= items

    def chunk_copies(seq, r):
        src_ref, chunk = (ck_ref, r) if r < chunks else (cv_ref, r - chunks)
        slot = r % ring
        return [pltpu.make_async_copy(src_ref.at[pt_ref[seq, chunk * ppc + i]], buf_ref.at[slot, i],
                                      sem_ref.at[slot]) for i in range(ppc)]

    def prologue():
        @pl.when(b == 0)
        def _first_chunks():
            for r in range(ahead):
                for cp in chunk_copies(b, r):
                    cp.start()
        qbd_ref[...] = _head_rows(q_ref[0] * (HEAD_DIM ** -0.5), N_HEADS_A, HEAD_DIM).astype(BF16)

    def key_chunk(r, slot):
        qbd_bf = qbd_ref[...]
        for i in range(ppc):
            s_ref[r * ppc + i] = jnp.dot(qbd_bf, buf_ref[slot, i].astype(BF16), preferred_element_type=F32)

    def select_and_softmax():
        lo_all = lax.dot_general(qbd_ref[...], pad_rows(kn_ref[0]).astype(BF16), _NT, preferred_element_type=F32)
        gr = rows // _SAMPLE_ROW_GROUPS
        lane = lax.broadcasted_iota(jnp.int32, (gr, page_size), 1)
        t_row = lax.broadcasted_iota(jnp.int32, (gr, page_size), 0) % t
        for g in range(_SAMPLE_ROW_GROUPS):
            rs = pl.ds(g * gr, gr)
            sc = jnp.zeros((gr, page_size), F32)
            for j in range(n_blocks):
                bs = s_ref[j * pages_per_block, rs, :]
                bm = bs
                for pg in range(1, pages_per_block):
                    bs = bs + s_ref[j * pages_per_block + pg, rs, :]
                    bm = jnp.maximum(bm, s_ref[j * pages_per_block + pg, rs, :])
                sc = jnp.where(lane == j, jnp.sum(bs, axis=1, keepdims=True), sc)
                blk_ref[j, rs, :] = bm
            sel = _select_topk(sc, lane < n_blocks, lane.astype(F32), axis=1)
            lo = jnp.where(lane <= t_row, lo_all[g * gr:(g + 1) * gr], NEG_INF)
            mx = lo
            for j in range(n_blocks):
                mask = jnp.broadcast_to(jnp.where(sel[:, j:j + 1], 0.0, NEG_INF), (gr, page_size))
                mx = jnp.maximum(mx, blk_ref[j, rs, :] + mask)
                blk_ref[j, rs, :] = mask
            m = jnp.max(mx, axis=1, keepdims=True)
            po = jnp.exp(lo - m)
            lsum = po
            for j in range(n_blocks):
                shift = blk_ref[j, rs, :] - m
                for pg in range(j * pages_per_block, (j + 1) * pages_per_block):
                    p = jnp.exp(s_ref[pg, rs, :] + shift)
                    lsum = lsum + p
                    p_ref[pg, rs, :] = p.astype(BF16)
            pown_ref[rs, :] = po.astype(BF16)
            l_ref[rs, :] = jnp.broadcast_to(jnp.sum(lsum, axis=1, keepdims=True), (gr, page_size))
        acc_ref[...] = jnp.zeros_like(acc_ref)

    def value_chunk(r, slot):
        acc = acc_ref[...]
        for i in range(ppc):
            acc = acc + lax.dot_general(p_ref[(r - chunks) * ppc + i], buf_ref[slot, i].astype(BF16), _NT,
                                        preferred_element_type=F32)
        acc_ref[...] = acc

    def item(r):
        slot = r % ring
        if r + ahead < items:
            for cp in chunk_copies(b, r + ahead):
                cp.start()
        else:
            @pl.when(b + 1 < n_seq)
            def _next_sequence():
                for cp in chunk_copies(b + 1, r + ahead - items):
                    cp.start()
        for cp in chunk_copies(b, r):
            cp.wait()
        if r < chunks:
            key_chunk(r, slot)
        else:
            if r == chunks:
                select_and_softmax()
            value_chunk(r, slot)

    def finish():
        acc = acc_ref[...] + jnp.dot(pown_ref[...], pad_rows(vn_ref[0]).astype(BF16), preferred_element_type=F32)
        l = jnp.concatenate([l_ref[...]] * (D_ATTN // page_size), axis=1)
        o_ref[0] = _head_diagonal(acc / l, t, N_HEADS_A, HEAD_DIM)

    return prologue, item, finish


def _moba_sample_scratch(t, n_pages, pages_per_chunk, page_size):
    rows = N_HEADS_A * t
    return [pltpu.VMEM((_PAGE_RING, pages_per_chunk, D_ATTN, page_size), F32),
            pltpu.SemaphoreType.DMA((_PAGE_RING,)),
            pltpu.VMEM((n_pages, rows, page_size), F32),
            pltpu.VMEM((n_pages, rows, page_size), BF16),
            pltpu.VMEM((n_pages * page_size // MOBA_BLOCK, rows, page_size), F32),
            pltpu.VMEM((rows, D_ATTN), BF16),
            pltpu.VMEM((rows, page_size), BF16),
            pltpu.VMEM((rows, page_size), F32),
            pltpu.VMEM((rows, D_ATTN), F32)]


_N_PROMPT_SCRATCH = 8


def _moba_seq_schedule(ti, tiles_per_head):
    one_from = tiles_per_head // 4
    two_from = 3 * tiles_per_head // 4
    singles = two_from - one_from
    late = jnp.maximum(ti - two_from, 0)
    first = jnp.where(ti < two_from, jnp.maximum(ti - one_from, 0), singles + 2 * late)
    second = singles + 1 + 2 * late
    return first, second, ti >= one_from, ti >= two_from


def _moba_kernel(pt_ref, q_ref, k_ref, v_ref, kmean_ref, qsa_ref, kna_ref, vna_ref, qsb_ref, knb_ref, vnb_ref,
                 ck_ref, cv_ref, o_ref, os_ref, *scratch,
                 tiles_per_head, n_seq, pages_per_chunk, chunks, page_size):
    n = pl.program_id(0)
    ti = lax.rem(n, tiles_per_head)
    base = n - ti
    first, second, has_first, has_second = _moba_seq_schedule(ti, tiles_per_head)
    p_setup, p_pairs, p_finish = _moba_prompt_parts(
        ti, q_ref, k_ref, v_ref, kmean_ref, o_ref, *scratch[:_N_PROMPT_SCRATCH])
    sample = functools.partial(_moba_sample_parts, pages_per_chunk=pages_per_chunk, chunks=chunks,
                               page_size=page_size)
    sample_scratch = scratch[_N_PROMPT_SCRATCH:]
    osa_ref = os_ref.at[0, pl.ds(first & 1, 1)]
    osb_ref = os_ref.at[0, pl.ds(1, 1)]
    seqs = ((has_first, sample(base + first, n_seq, pt_ref, qsa_ref, kna_ref, vna_ref, ck_ref, cv_ref, osa_ref,
                               *sample_scratch)),
            (has_second, sample(base + second, n_seq, pt_ref, qsb_ref, knb_ref, vnb_ref, ck_ref, cv_ref, osb_ref,
                                *sample_scratch)))

    p_setup()
    pairs = ti * (q_ref.shape[1] // MOBA_BLOCK) // 2
    items = 2 * chunks
    every = 2
    per_seq = items // every
    shift = jnp.where(has_second, int(math.log2(2 * per_seq)), int(math.log2(per_seq)))

    def segment_start(g):
        return jnp.minimum(lax.shift_right_logical(g * pairs, shift), pairs)

    for i, (active, (s_prologue, s_item, s_finish)) in enumerate(seqs):
        pl.when(active)(s_prologue)
        for r in range(items):
            pl.when(active)(functools.partial(s_item, r))
            if r % every == every - 1:
                g = i * per_seq + r // every
                p_pairs(segment_start(g), segment_start(g + 1))
        pl.when(active)(s_finish)
    p_finish()


def _moba(q, kb, vb, kmean, q_s, k_new, v_new, cache_k, cache_v, page_table, pages_per_chunk):
    b, s, _ = q.shape
    nb = s // MOBA_BLOCK
    tq = _MOBA_Q_BLOCKS * MOBA_BLOCK
    assert nb <= LANES // 2 and nb % SUBLANES == 0 and s % tq == 0 and _MOBA_Q_BLOCKS % 2 == 0
    head_pairs = D_ATTN // LANES
    tiles_per_head = s // tq
    n_tiles = b * head_pairs * tiles_per_head

    db, t, _ = q_s.shape
    n_pages = page_table.shape[1]
    page_size = cache_k.shape[2]
    chunks = n_pages // pages_per_chunk
    assert (n_pages * page_size) % MOBA_BLOCK == 0 and MOBA_BLOCK % page_size == 0
    assert n_pages * page_size // MOBA_BLOCK >= MOBA_TOPK and t <= page_size and page_size == LANES
    assert n_pages % pages_per_chunk == 0
    assert db == n_tiles and tiles_per_head % 4 == 0

    def tile_map(n, pt):
        return (n // (head_pairs * tiles_per_head), n % tiles_per_head, (n // tiles_per_head) % head_pairs)

    def head_map(n, pt):
        return (n // (head_pairs * tiles_per_head), 0, (n // tiles_per_head) % head_pairs)

    def first_map(n, pt):
        ti = n % tiles_per_head
        return (n - ti + _moba_seq_schedule(ti, tiles_per_head)[0], 0, 0)

    def second_map(n, pt):
        ti = n % tiles_per_head
        return (n - ti + _moba_seq_schedule(ti, tiles_per_head)[1], 0, 0)

    def pair_map(n, pt):
        return (first_map(n, pt)[0] // 2, 0, 0, 0)

    seq_block = (1, t, D_ATTN)
    grid_spec = pltpu.PrefetchScalarGridSpec(
        num_scalar_prefetch=1,
        grid=(n_tiles,),
        in_specs=[pl.BlockSpec((1, tq, LANES), tile_map),
                  pl.BlockSpec((1, s, LANES), head_map),
                  pl.BlockSpec((1, s, LANES), head_map),
                  pl.BlockSpec((1, nb, LANES), head_map)]
                 + [pl.BlockSpec(seq_block, first_map)] * 3
                 + [pl.BlockSpec(seq_block, second_map)] * 3
                 + [pl.BlockSpec(memory_space=pl.ANY)] * 2,
        out_specs=(pl.BlockSpec((1, tq, LANES), tile_map), pl.BlockSpec((1, 2, t, D_ATTN), pair_map)),
        scratch_shapes=_moba_prompt_scratch(s, tq) + _moba_sample_scratch(t, n_pages, pages_per_chunk, page_size))
    attn, attn_s = pl.pallas_call(
        functools.partial(_moba_kernel, tiles_per_head=tiles_per_head, n_seq=db,
                          pages_per_chunk=pages_per_chunk, chunks=chunks, page_size=page_size),
        grid_spec=grid_spec,
        out_shape=(jax.ShapeDtypeStruct((b, s, D_ATTN), BF16), jax.ShapeDtypeStruct((db // 2, 2, t, D_ATTN), F32)),
        compiler_params=pltpu.CompilerParams(dimension_semantics=("arbitrary",),
                                             vmem_limit_bytes=52 * 2**20),
        name="moba",
    )(page_table, q, kb, vb, kmean, q_s, k_new, v_new, q_s, k_new, v_new, cache_k, cache_v)
    return attn, attn_s.reshape(db, t, D_ATTN)


def kernel(x_prompt, x_sample, mem_prompt, cache_k, cache_v, state_ssm_re, state_ssm_im,
           cache_mem_k, cache_mem_v, page_table, w_in, w_mem_kv, a_re, a_im, log_dt,
           b_re, b_im, c_re, c_im, d_skip, w_glu, b_glu, w_out, ln_g, ln_b):
    batch, seq, _ = x_prompt.shape
    db, t_new, _ = x_sample.shape
    n_mem = mem_prompt.shape[1]
    n_phys, page_size = cache_k.shape[:2]
    n_seg = SSM_CHAINS // batch
    seg_len = seq // n_seg
    tm = min(512, seg_len)

    w_in_bf = w_in.astype(BF16)
    w_mem_bf = w_mem_kv.astype(BF16)
    w_out_bf = w_out.astype(BF16)
    wg_bf = w_glu.astype(BF16)
    bg_row = b_glu.astype(F32).reshape(1, 2 * D_SSM)
    d_row = d_skip.astype(F32).reshape(1, D_SSM)
    g_row = ln_g.astype(F32).reshape(1, D_MODEL)
    b_row = ln_b.astype(F32).reshape(1, D_MODEL)
    a_row, b_mat, c_mat = _ssm_operands(a_re, a_im, log_dt, b_re, b_im, c_re, c_im)
    b_bf = b_mat.astype(BF16)
    c_bf = c_mat.astype(BF16)

    x2d = x_prompt.reshape(batch * seq, D_MODEL)
    q, k_t, v_t, u_chain, qm, gate, kb, vb, kmean = _in_proj_prompt(x2d, w_in_bf, batch, seq, tm)
    xs2d = x_sample.reshape(db * t_new, D_MODEL)
    tms = min(512, db * t_new)
    q_s, k_s, v_s, u_s, qm_s, gate_s = _in_proj_sample(xs2d, w_in_bf, tms)
    attn, attn_s = _moba(q.reshape(batch, seq, D_ATTN), kb.reshape(batch, seq, D_ATTN),
                         vb.reshape(batch, seq, D_ATTN), kmean.reshape(batch, seq // MOBA_BLOCK, D_ATTN),
                         q_s.reshape(db, t_new, D_ATTN), k_s.reshape(db, t_new, D_ATTN),
                         v_s.reshape(db, t_new, D_ATTN),
                         cache_k.transpose(0, 2, 3, 1).reshape(n_phys, D_ATTN, page_size),
                         cache_v.transpose(0, 2, 3, 1).reshape(n_phys, D_ATTN, page_size), page_table,
                         pages_per_chunk=min(16, page_table.shape[1] // 2))

    y_chain, hfin = _ssm_prompt(u_chain.reshape(seg_len * SSM_CHAINS, D_SSM), a_row, b_bf, c_bf, d_row,
                                wg_bf, bg_row, n_seg, steps_per_chunk=min(64, seg_len))
    mk, mv = _mem_kv(mem_prompt.reshape(batch * n_mem, D_MODEL), w_mem_bf)
    mem_o = _mem_attn_prompt(qm, mk, mv, batch, seq, tm)
    tiles_per_seq = seq // tm
    tiles_per_seg = seg_len // tm

    def chain_map(i):
        b = i // tiles_per_seq
        ti = i % tiles_per_seq
        return (ti % tiles_per_seg, b * n_seg + ti // tiles_per_seg)

    y_prompt = _merge(x2d, attn.reshape(batch * seq, D_ATTN), y_chain.reshape(seg_len, SSM_CHAINS * D_SSM),
                      mem_o, gate, w_out_bf, g_row, b_row, tm, chain_map)
    h_last = hfin[n_seg - 1::n_seg]
    ssm_re_p = h_last[:, :SSM_LANES].reshape(batch, SSM_GROUPS, SSM_STATE)
    ssm_im_p = h_last[:, SSM_LANES:].reshape(batch, SSM_GROUPS, SSM_STATE)

    h0 =jnp.concatenate([state_ssm_re.astype(F32).reshape(db, SSM_LANES),
                          state_ssm_im.astype(F32).reshape(db, SSM_LANES)], axis=1)
    y_tb, hfin_s = _ssm_sample(u_s.reshape(db, t_new, D_SSM).transpose(1, 0, 2), h0, a_row, b_mat, c_bf,
                               d_row, wg_bf, bg_row)
    ssm_s = y_tb.transpose(1, 0, 2).reshape(db * t_new, D_SSM)
    mem_s = _mem_attn_sample(qm_s.reshape(db, t_new, D_MEM),
                             cache_mem_k.transpose(0, 2, 3, 1).reshape(db, D_MEM, n_mem),
                             cache_mem_v.transpose(0, 2, 3, 1).reshape(db, D_MEM, n_mem), group=min(8, db))
    y_sample = _merge(xs2d, attn_s.reshape(db * t_new, D_ATTN), ssm_s, mem_s.reshape(db * t_new, D_MEM),
                      gate_s, w_out_bf, g_row, b_row, tms, lambda i: (i, 0))

    heads = (N_HEADS_A, HEAD_DIM)
    mem_heads = (MEM_HEADS, D_MEM // MEM_HEADS)
    return (y_prompt.reshape(batch, seq, D_MODEL), y_sample.reshape(db, t_new, D_MODEL),
            k_t.reshape(batch, *heads, seq).transpose(0, 3, 1, 2),
            v_t.reshape(batch, *heads, seq).transpose(0, 3, 1, 2), ssm_re_p, ssm_im_p,
            mk.reshape(batch, n_mem, *mem_heads), mv.reshape(batch, n_mem, *mem_heads),
            k_s.reshape(db, t_new, *heads), v_s.reshape(db, t_new, *heads),
            hfin_s[:, :SSM_LANES].reshape(db, SSM_GROUPS, SSM_STATE),
            hfin_s[:, SSM_LANES:].reshape(db, SSM_GROUPS, SSM_STATE))
```

```python
import functools
import math

import jax
import jax.numpy as jnp
from jax import lax
from jax.experimental import pallas as pl
from jax.experimental.pallas import tpu as pltpu

F32 = jnp.float32
BF16 = jnp.bfloat16

D_MODEL = 1024
D_ATTN = 512
D_SSM = 256
D_MEM = 256
D_MIX = D_ATTN + D_SSM + D_MEM
HEAD_DIM = 64
N_HEADS_A = D_ATTN // HEAD_DIM
MOBA_BLOCK = 256
MOBA_TOPK = 3
SSM_CH = 16
SSM_GROUPS = D_SSM // SSM_CH
SSM_STATE = 64
SSM_LANES = SSM_GROUPS * SSM_STATE
MEM_HEADS = 4
D_IN = 3 * D_ATTN + D_SSM + D_MEM + D_MIX
DEPTH = 1
DEEPNORM_ALPHA = (2.0 * DEPTH) ** 0.25
LN_EPS = 1e-5
NEG_INF = -1e30

LANES = 128
SUBLANES = 8
SSM_CHAINS = SUBLANES

_NT = (((1,), (1,)), ((), ()))


def _vmem_limit(nbytes):
    return pltpu.CompilerParams(vmem_limit_bytes=int(nbytes))


def _proj_kernel(x_ref, w_ref, *out_refs, with_prompt_extras):
    xb = x_ref[...].astype(BF16)

    def seg(lo, hi):
        return jnp.dot(xb, w_ref[:, lo:hi], preferred_element_type=F32)

    c0, c1, c2, c3, c4 = D_ATTN, 2 * D_ATTN, 3 * D_ATTN, 3 * D_ATTN + D_SSM, 3 * D_ATTN + D_SSM + D_MEM
    if with_prompt_extras:
        q_ref, k_ref, v_ref, u_ref, qm_ref, gate_ref, kb_ref, vb_ref, kmean_ref = out_refs
    else:
        q_ref, k_ref, v_ref, u_ref, qm_ref, gate_ref = out_refs
    q_ref[...] = seg(0, c0)
    k = seg(c0, c1)
    v = seg(c1, c2)
    u_ref[...] = seg(c2, c3)
    qm_ref[...] = seg(c3, c4).astype(qm_ref.dtype)
    gate_ref[...] = seg(c4, D_IN).astype(gate_ref.dtype)
    if not with_prompt_extras:
        k_ref[...] = k
        v_ref[...] = v
    else:
        k_ref[0] = k.T
        v_ref[0] = v.T
        kb_ref[...] = k.astype(BF16)
        vb_ref[...] = v.astype(BF16)
        tm = k.shape[0]
        kmean_ref[0] = jnp.sum(k.reshape(tm // MOBA_BLOCK, MOBA_BLOCK, D_ATTN), axis=1) * (1.0 / MOBA_BLOCK)


def _in_proj_prompt(x2d, w_bf, batch, seq, tm):
    m = x2d.shape[0]
    n_seg = SSM_CHAINS // batch
    seg_len = seq // n_seg
    tiles_per_seq = seq // tm
    tiles_per_seg = seg_len // tm
    row = lambda i: (i, 0)

    def u_map(i):
        b = i // tiles_per_seq
        t = i % tiles_per_seq
        return (t % tiles_per_seg, b * n_seg + t // tiles_per_seg)

    out_shape = (
        jax.ShapeDtypeStruct((m, D_ATTN), F32),
        jax.ShapeDtypeStruct((batch, D_ATTN, seq), F32),
        jax.ShapeDtypeStruct((batch, D_ATTN, seq), F32),
        jax.ShapeDtypeStruct((seg_len, SSM_CHAINS * D_SSM), F32),
        jax.ShapeDtypeStruct((m, D_MEM), BF16),
        jax.ShapeDtypeStruct((m, D_MIX), BF16),
        jax.ShapeDtypeStruct((m, D_ATTN), BF16),
        jax.ShapeDtypeStruct((m, D_ATTN), BF16),
        jax.ShapeDtypeStruct((m // tm, tm // MOBA_BLOCK, D_ATTN), F32),
    )
    kv_t = lambda i: (i // tiles_per_seq, 0, i % tiles_per_seq)
    out_specs = (
        pl.BlockSpec((tm, D_ATTN), row),
        pl.BlockSpec((1, D_ATTN, tm), kv_t),
        pl.BlockSpec((1, D_ATTN, tm), kv_t),
        pl.BlockSpec((tm, D_SSM), u_map),
        pl.BlockSpec((tm, D_MEM), row),
        pl.BlockSpec((tm, D_MIX), row),
        pl.BlockSpec((tm, D_ATTN), row),
        pl.BlockSpec((tm, D_ATTN), row),
        pl.BlockSpec((1, tm // MOBA_BLOCK, D_ATTN), lambda i: (i, 0, 0)),
    )
    return pl.pallas_call(
        functools.partial(_proj_kernel, with_prompt_extras=True),
        grid=(m // tm,),
        in_specs=[pl.BlockSpec((tm, D_MODEL), row),
                  pl.BlockSpec((D_MODEL, D_IN), lambda i: (0, 0), pipeline_mode=pl.Buffered(1))],
        out_specs=out_specs,
        out_shape=out_shape,
        compiler_params=_vmem_limit(52 * 2**20),
        name="in_proj_prompt",
    )(x2d, w_bf)


def _in_proj_sample(x2d, w_bf, tm):
    m = x2d.shape[0]
    row = lambda i: (i, 0)
    out_shape = (
        jax.ShapeDtypeStruct((m, D_ATTN), F32),
        jax.ShapeDtypeStruct((m, D_ATTN), F32),
        jax.ShapeDtypeStruct((m, D_ATTN), F32),
        jax.ShapeDtypeStruct((m, D_SSM), F32),
        jax.ShapeDtypeStruct((m, D_MEM), BF16),
        jax.ShapeDtypeStruct((m, D_MIX), BF16),
    )
    widths = (D_ATTN, D_ATTN, D_ATTN, D_SSM, D_MEM, D_MIX)
    return pl.pallas_call(
        functools.partial(_proj_kernel, with_prompt_extras=False),
        grid=(m // tm,),
        in_specs=[pl.BlockSpec((tm, D_MODEL), row),
                  pl.BlockSpec((D_MODEL, D_IN), lambda i: (0, 0), pipeline_mode=pl.Buffered(1))],
        out_specs=tuple(pl.BlockSpec((tm, w), row) for w in widths),
        out_shape=out_shape,
        compiler_params=_vmem_limit(52 * 2**20),
        name="in_proj_sample",
    )(x2d, w_bf)


def _mem_kv_kernel(mem_ref, w_ref, mk_ref, mv_ref):
    kv = jnp.dot(mem_ref[...].astype(BF16), w_ref[...], preferred_element_type=F32)
    mk_ref[...] = kv[:, :D_MEM]
    mv_ref[...] = kv[:, D_MEM:]


def _mem_kv(mem2d, w_bf):
    m = mem2d.shape[0]
    tm = min(m, 512)
    row = lambda i: (i, 0)
    return pl.pallas_call(
        _mem_kv_kernel,
        grid=(m // tm,),
        in_specs=[pl.BlockSpec((tm, D_MODEL), row),
                  pl.BlockSpec((D_MODEL, 2 * D_MEM), lambda i: (0, 0))],
        out_specs=(pl.BlockSpec((tm, D_MEM), row), pl.BlockSpec((tm, D_MEM), row)),
        out_shape=(jax.ShapeDtypeStruct((m, D_MEM), F32), jax.ShapeDtypeStruct((m, D_MEM), F32)),
        name="mem_kv",
    )(mem2d, w_bf)


def _select_topk(scores, valid, index, axis):
    lowest = jnp.finfo(F32).min
    s = jnp.where(valid, scores, lowest)
    sel = jnp.zeros(scores.shape, dtype=jnp.bool_)
    for _ in range(MOBA_TOPK):
        m = jnp.max(s, axis=axis, keepdims=True)
        first = jnp.min(jnp.where(s == m, index, jnp.float32(2**30)), axis=axis, keepdims=True)
        pick = index == first
        sel = sel | (pick & valid)
        s = jnp.where(pick, lowest, s)
    return sel


_MOBA_Q_BLOCKS = 2
_MOBA_ROW_GROUP = 128


def _moba_prompt_parts(ti, q_ref, k_ref, v_ref, kmean_ref, o_ref,
                       kaug_ref, vaug_ref, lhs_ref, s_ref, p_ref, alpha_ref, m_ref, acc_ref):
    blk = MOBA_BLOCK
    tq = q_ref.shape[1]
    qb = tq // blk
    nb = k_ref.shape[1] // blk
    half = LANES // 2

    def build_operands():
        lane_b = lax.broadcasted_iota(jnp.int32, (blk, LANES), 1)
        head0_b = lane_b < half

        def body(j, carry):
            rows = pl.ds(pl.multiple_of(j * blk, blk), blk)
            kb = k_ref[0, rows, :].astype(F32)
            vb = v_ref[0, rows, :].astype(F32)
            hot0 = jnp.where(lane_b == half + j, 1.0, 0.0)
            hot1 = jnp.where(lane_b == j, 1.0, 0.0)
            kaug_ref[0, rows, :] = jnp.where(head0_b, kb, hot0).astype(BF16)
            kaug_ref[1, rows, :] = jnp.where(head0_b, hot1, kb).astype(BF16)
            vaug_ref[0, rows, :] = jnp.where(head0_b, vb, 1.0).astype(BF16)
            vaug_ref[1, rows, :] = jnp.where(head0_b, 1.0, vb).astype(BF16)
            return carry
        lax.fori_loop(0, nb, body, 0)

    def select_blocks():
        q2 = q_ref[0]
        means = kmean_ref[0]
        lane_m = lax.broadcasted_iota(jnp.int32, (nb, LANES), 1)
        pad = [jnp.zeros((half - nb, LANES), F32)] if nb < half else []
        means_p = jnp.concatenate([jnp.where(lane_m >= half, means, 0.0)] + pad
                                  + [jnp.where(lane_m < half, means, 0.0)] + pad, axis=0)
        sc_t = lax.dot_general(means_p, q2, _NT, precision=lax.Precision.HIGHEST,
                               preferred_element_type=F32)
        slot = lax.broadcasted_iota(jnp.int32, (nb, tq), 0)
        slot_f = slot.astype(F32)
        n_full = ti * qb + lax.broadcasted_iota(jnp.int32, (nb, tq), 1) // blk
        bias_t = []
        for off in (0, half):
            sel = _select_topk(sc_t[off:off + nb], slot < n_full, slot_f, axis=0)
            bias_t.append(jnp.where(sel | (slot == n_full), 0.0, NEG_INF))
            if nb < half:
                bias_t.append(jnp.zeros((half - nb, tq), F32))
        bias = jnp.concatenate(bias_t, axis=0).T
        head0_q = lax.broadcasted_iota(jnp.int32, (tq, LANES), 1) < half
        qs = q2 * (HEAD_DIM ** -0.5)
        lhs_ref[0] = jnp.where(head0_q, qs, bias).astype(BF16)
        lhs_ref[1] = jnp.where(head0_q, bias, qs).astype(BF16)
        m_ref[...] = jnp.full(m_ref.shape, NEG_INF, F32)
        acc_ref[...] = jnp.zeros_like(acc_ref)

    rg = _MOBA_ROW_GROUP
    c_i = lax.broadcasted_iota(jnp.int32, (rg, blk), 1)
    r_i = lax.broadcasted_iota(jnp.int32, (rg, blk), 0)

    def key_rows(j):
        return pl.ds(j * blk if isinstance(j, int) else pl.multiple_of(j * blk, blk), blk)

    def score_block(j, slot):
        for e in (0, 1):
            s_ref[slot, e] = lax.dot_general(lhs_ref[e], kaug_ref[e, key_rows(j), :], _NT,
                                             preferred_element_type=F32)

    def softmax_pv_block(j, slot, causal):
        for e in (0, 1):
            for g in range(tq // rg):
                rows = pl.ds(g * rg, rg)
                s = s_ref[slot, e, rows, :]
                if causal:
                    s = jnp.where(j * blk + c_i <= ti * tq + g * rg + r_i, s, NEG_INF)
                m_old = m_ref[e, rows, :]
                m_new = jnp.maximum(m_old, jnp.max(s, axis=1, keepdims=True))
                alpha_ref[e, rows, :] = jnp.exp(m_old - m_new)
                p_ref[e, rows, :] = jnp.exp(s - jnp.concatenate([m_new] * (blk // LANES), axis=1)).astype(BF16)
                m_ref[e, rows, :] = m_new
            pv = jnp.dot(p_ref[e], vaug_ref[e, key_rows(j), :], preferred_element_type=F32)
            acc_ref[e] = alpha_ref[e] * acc_ref[e] + pv

    n_past = ti * qb

    def setup():
        pl.when(ti == 0)(build_operands)
        select_blocks()
        score_block(0, 0)

    def past_pair(jj, carry):
        j = 2 * jj
        score_block(j + 1, 1)
        softmax_pv_block(j, 0, causal=False)
        score_block(j + 2, 0)
        softmax_pv_block(j + 1, 1, causal=False)
        return carry

    def past_pairs(lo, hi):
        lax.fori_loop(lo, hi, past_pair, 0)

    def finish():
        for d in range(qb):
            if d + 1 < qb:
                score_block(n_past + d + 1, (d + 1) % 2)
            softmax_pv_block(n_past + d, d % 2, causal=True)
        a0 = acc_ref[0]
        a1 = acc_ref[1]
        o0 = a0 / pltpu.roll(a0, half, axis=1)
        o1 = a1 / pltpu.roll(a1, half, axis=1)
        head0_q = lax.broadcasted_iota(jnp.int32, (tq, LANES), 1) < half
        o_ref[0] = jnp.where(head0_q, o0, o1).astype(o_ref.dtype)

    return setup, past_pairs, finish


def _moba_prompt_scratch(s, tq):
    return [pltpu.VMEM((2, s, LANES), BF16),
            pltpu.VMEM((2, s, LANES), BF16),
            pltpu.VMEM((2, tq, LANES), BF16),
            pltpu.VMEM((2, 2, tq, MOBA_BLOCK), F32),
            pltpu.VMEM((2, tq, MOBA_BLOCK), BF16),
            pltpu.VMEM((2, tq, LANES), F32),
            pltpu.VMEM((2, tq, LANES), F32),
            pltpu.VMEM((2, tq, LANES), F32)]


def _ssm_disc_kernel(a_re_ref, a_im_ref, log_dt_ref, bt_re_ref, bt_im_ref,
                     abar_re_ref, abar_im_ref, bbt_re_ref, bbt_im_ref):
    a_re = a_re_ref[...]
    a_im = a_im_ref[...]
    dt = jnp.exp(log_dt_ref[...])
    mag = jnp.exp(a_re * dt)
    ang = a_im * dt
    abar_re = mag * jnp.cos(ang)
    abar_im = mag * jnp.sin(ang)
    den = a_re * a_re + a_im * a_im
    f_re = ((abar_re - 1.0) * a_re + abar_im * a_im) / den
    f_im = (abar_im * a_re - (abar_re - 1.0) * a_im) / den
    abar_re_ref[...] = abar_re
    abar_im_ref[...] = abar_im
    bt_re = bt_re_ref[...]
    bt_im = bt_im_ref[...]
    bbt_re_ref[...] = f_re[:, None, :] * bt_re - f_im[:, None, :] * bt_im
    bbt_im_ref[...] = f_re[:, None, :] * bt_im + f_im[:, None, :] * bt_re


def _ssm_operands(a_re, a_im, log_dt, b_re, b_im, c_re, c_im):
    g, p, c = b_re.shape
    abar_re, abar_im, bbt_re, bbt_im = pl.pallas_call(
        _ssm_disc_kernel,
        out_shape=(jax.ShapeDtypeStruct((g, p), F32), jax.ShapeDtypeStruct((g, p), F32),
                   jax.ShapeDtypeStruct((g, c, p), F32), jax.ShapeDtypeStruct((g, c, p), F32)),
        name="ssm_discretize",
    )(a_re.astype(F32), a_im.astype(F32), log_dt.astype(F32).reshape(g, 1),
      b_re.astype(F32).transpose(0, 2, 1), b_im.astype(F32).transpose(0, 2, 1))
    eye = jnp.eye(g, dtype=F32)

    def in_block_diag(bbt):
        return jnp.einsum('gcp,gh->gchp', bbt, eye).reshape(g * c, g * p)

    def out_block_diag(cm):
        return jnp.einsum('gcp,gh->gphc', cm, eye).reshape(g * p, g * c)

    b_mat = jnp.concatenate([in_block_diag(bbt_re), in_block_diag(bbt_im)], axis=1)
    c_mat = jnp.concatenate([out_block_diag(c_re.astype(F32)),
                             -out_block_diag(c_im.astype(F32))], axis=0)
    a_row = jnp.concatenate([abar_re.reshape(1, g * p), abar_im.reshape(1, g * p)], axis=1)
    return a_row, b_mat, c_mat


def _complex_step(a_re, a_im, h_re, h_im, bu_re, bu_im):
    return (a_re * h_re - a_im * h_im + bu_re, a_re * h_im + a_im * h_re + bu_im)


def _ssm_output(h_all, u, c_ref, d_ref, wg_ref, bg_ref):
    y = jnp.dot(h_all.astype(BF16), c_ref[...], preferred_element_type=F32) + d_ref[...] * u
    y = jax.nn.gelu(y)
    z = jnp.dot(y.astype(BF16), wg_ref[...], preferred_element_type=F32) + bg_ref[...]
    return z[:, :D_SSM] * jax.nn.sigmoid(z[:, D_SSM:])


def _ssm_prompt_kernel(u_ref, a_ref, b_ref, c_ref, d_ref, wg_ref, bg_ref, y_ref, hfin_ref,
                       hb_ref, h_ref, init_ref, *, n_seg):
    pss = pl.program_id(0)
    c = pl.program_id(1)
    n_chunks = pl.num_programs(1)
    rows = u_ref.shape[0]
    steps = rows // SSM_CHAINS
    n = SSM_LANES
    a_re = jnp.broadcast_to(a_ref[:, :n], (SSM_CHAINS, n))
    a_im = jnp.broadcast_to(a_ref[:, n:], (SSM_CHAINS, n))

    @pl.when((pss == 0) & (c == 0))
    def _zero_state():
        h_ref[...] = jnp.zeros_like(h_ref)

    @pl.when((pss == 1) & (c == 0))
    def _true_initial_state():
        h_ref[...] = init_ref[...]

    u = u_ref[...]
    hb_ref[...] = jnp.dot(u.astype(BF16), b_ref[...], preferred_element_type=F32)

    def scan(store):
        def step(s, carry):
            h_re, h_im = carry
            r = pl.ds(pl.multiple_of(s * SSM_CHAINS, SSM_CHAINS), SSM_CHAINS)
            h_re, h_im = _complex_step(a_re, a_im, h_re, h_im, hb_ref[r, :n], hb_ref[r, n:])
            if store:
                hb_ref[r, :n] = h_re
                hb_ref[r, n:] = h_im
            return h_re, h_im
        h_re, h_im = lax.fori_loop(0, steps, step, (h_ref[:, :n], h_ref[:, n:]), unroll=4)
        h_ref[:, :n] = h_re
        h_ref[:, n:] = h_im

    @pl.when(pss == 0)
    def _local_pass():
        scan(False)

    @pl.when((pss == 0) & (c == n_chunks - 1))
    def _segment_initial_states():
        seg_len = steps * n_chunks
        p_re, p_im = a_re, a_im
        for _ in range(int(math.log2(seg_len))):
            p_re, p_im = p_re * p_re - p_im * p_im, 2.0 * p_re * p_im
        e_re = h_ref[:, :n]
        e_im = h_ref[:, n:]
        first = (lax.broadcasted_iota(jnp.int32, (SSM_CHAINS, n), 0) % n_seg) == 0
        i_re = jnp.zeros((SSM_CHAINS, n), F32)
        i_im = jnp.zeros((SSM_CHAINS, n), F32)
        for _ in range(n_seg - 1):
            t_re = p_re * i_re - p_im * i_im + e_re
            t_im = p_re * i_im + p_im * i_re + e_im
            i_re = jnp.where(first, 0.0, pltpu.roll(t_re, 1, axis=0))
            i_im = jnp.where(first, 0.0, pltpu.roll(t_im, 1, axis=0))
        init_ref[:, :n] = i_re
        init_ref[:, n:] = i_im

    @pl.when(pss == 1)
    def _output_pass():
        scan(True)
        y_ref[...] = _ssm_output(hb_ref[...], u, c_ref, d_ref, wg_ref, bg_ref).astype(y_ref.dtype)

    @pl.when((pss == 1) & (c == n_chunks - 1))
    def _final_state():
        hfin_ref[...] = h_ref[...]


def _ssm_prompt(u_chain, a_row, b_bf, c_bf, d_row, wg_bf, bg_row, n_seg, steps_per_chunk):
    total = u_chain.shape[0]
    rows = steps_per_chunk * SSM_CHAINS
    n_chunks = total // rows
    seg_len = total // SSM_CHAINS
    assert 2 ** int(math.log2(seg_len)) == seg_len
    const = lambda p, c: (0, 0)
    return pl.pallas_call(
        functools.partial(_ssm_prompt_kernel, n_seg=n_seg),
        grid=(2, n_chunks),
        in_specs=[pl.BlockSpec((rows, D_SSM), lambda p, c: (c, 0)),
                  pl.BlockSpec((1, 2 * SSM_LANES), const),
                  pl.BlockSpec((D_SSM, 2 * SSM_LANES), const),
                  pl.BlockSpec((2 * SSM_LANES, D_SSM), const),
                  pl.BlockSpec((1, D_SSM), const),
                  pl.BlockSpec((D_SSM, 2 * D_SSM), const),
                  pl.BlockSpec((1, 2 * D_SSM), const)],
        out_specs=(pl.BlockSpec((rows, D_SSM), lambda p, c: (p * c, 0)),
                   pl.BlockSpec((SSM_CHAINS, 2 * SSM_LANES), const)),
        out_shape=(jax.ShapeDtypeStruct((total, D_SSM), BF16),
                   jax.ShapeDtypeStruct((SSM_CHAINS, 2 * SSM_LANES), F32)),
        scratch_shapes=[pltpu.VMEM((rows, 2 * SSM_LANES), F32),
                        pltpu.VMEM((SSM_CHAINS, 2 * SSM_LANES), F32),
                        pltpu.VMEM((SSM_CHAINS, 2 * SSM_LANES), F32)],
        compiler_params=pltpu.CompilerParams(dimension_semantics=("arbitrary", "arbitrary"),
                                             vmem_limit_bytes=40 * 2**20),
        name="ssm_prompt",
    )(u_chain, a_row, b_bf, c_bf, d_row, wg_bf, bg_row)


def _ssm_sample_kernel(u_ref, h0_ref, a_ref, b_ref, c_ref, d_ref, wg_ref, bg_ref, y_ref, hfin_ref):
    steps, batch, _ = u_ref.shape
    n = SSM_LANES
    a_re = jnp.broadcast_to(a_ref[:, :n], (batch, n))
    a_im = jnp.broadcast_to(a_ref[:, n:], (batch, n))
    h_re = h0_ref[:, :n]
    h_im = h0_ref[:, n:]
    for t in range(steps):
        u = u_ref[t]
        bu = jnp.dot(u, b_ref[...], precision=lax.Precision.HIGHEST, preferred_element_type=F32)
        h_re, h_im = _complex_step(a_re, a_im, h_re, h_im, bu[:, :n], bu[:, n:])
        h_all = jnp.concatenate([h_re, h_im], axis=1)
        y_ref[t] = _ssm_output(h_all, u, c_ref, d_ref, wg_ref, bg_ref)
    hfin_ref[:, :n] = h_re
    hfin_ref[:, n:] = h_im


def _ssm_sample(u_tb, h0, a_row, b_f32, c_bf, d_row, wg_bf, bg_row):
    steps, batch, _ = u_tb.shape
    return pl.pallas_call(
        _ssm_sample_kernel,
        out_shape=(jax.ShapeDtypeStruct((steps, batch, D_SSM), F32),
                   jax.ShapeDtypeStruct((batch, 2 * SSM_LANES), F32)),
        compiler_params=_vmem_limit(40 * 2**20),
        name="ssm_sample",
    )(u_tb, h0, a_row, b_f32, c_bf, d_row, wg_bf, bg_row)


def _mem_attn_prompt_kernel(q_ref, mk_ref, mv_ref, o_ref):
    rows = q_ref.shape[0]
    half = LANES // 2
    scale = (D_MEM // MEM_HEADS) ** -0.5
    head0 = lax.broadcasted_iota(jnp.int32, (rows, LANES), 1) < half
    outs = []
    for hp in range(D_MEM // LANES):
        cols = slice(hp * LANES, (hp + 1) * LANES)
        q2 = q_ref[:, cols].astype(F32)
        k2 = mk_ref[:, cols].astype(BF16)
        v2 = mv_ref[:, cols].astype(BF16)
        res = []
        for e in (0, 1):
            mine = head0 if e == 0 else jnp.logical_not(head0)
            lhs = jnp.where(mine, q2, 0.0).astype(BF16)
            s = lax.dot_general(lhs, k2, _NT, preferred_element_type=F32) * scale
            p = jnp.exp(s - jnp.max(s, axis=1, keepdims=True))
            l = jnp.sum(p, axis=1, keepdims=True)
            res.append(jnp.dot(p.astype(BF16), v2, preferred_element_type=F32) / l)
        outs.append(jnp.where(head0, res[0], res[1]))
    o_ref[...] = jnp.concatenate(outs, axis=1).astype(o_ref.dtype)


def _mem_attn_prompt(qm, mk, mv, batch, seq, tm):
    n_mem = mk.shape[0] // batch
    tiles = seq // tm
    return pl.pallas_call(
        _mem_attn_prompt_kernel,
        grid=(batch, tiles),
        in_specs=[pl.BlockSpec((tm, D_MEM), lambda b, i: (b * tiles + i, 0)),
                  pl.BlockSpec((n_mem, D_MEM), lambda b, i: (b, 0)),
                  pl.BlockSpec((n_mem, D_MEM), lambda b, i: (b, 0))],
        out_specs=pl.BlockSpec((tm, D_MEM), lambda b, i: (b * tiles + i, 0)),
        out_shape=jax.ShapeDtypeStruct((batch * seq, D_MEM), BF16),
        name="mem_attn_prompt",
    )(qm, mk, mv)


def _head_rows(x, n_heads, head_dim):
    head_of_lane = lax.broadcasted_iota(jnp.int32, x.shape, 1) // head_dim
    return jnp.concatenate([jnp.where(head_of_lane == h, x, 0.0) for h in range(n_heads)], axis=0)


def _head_diagonal(o_full, t, n_heads, head_dim):
    head_of_lane = lax.broadcasted_iota(jnp.int32, (t, o_full.shape[1]), 1) // head_dim
    out = jnp.zeros((t, o_full.shape[1]), F32)
    for h in range(n_heads):
        out = out + jnp.where(head_of_lane == h, o_full[h * t:(h + 1) * t], 0.0)
    return out


def _mem_attn_sample_kernel(q_ref, mkt_ref, mvt_ref, o_ref):
    group, t, _ = q_ref.shape
    head_dim = D_MEM // MEM_HEADS
    scale = head_dim ** -0.5
    for g in range(group):
        qbd = _head_rows(q_ref[g].astype(F32) * scale, MEM_HEADS, head_dim).astype(BF16)
        s = jnp.dot(qbd, mkt_ref[g].astype(BF16), preferred_element_type=F32)
        p = jnp.exp(s - jnp.max(s, axis=1, keepdims=True))
        l = jnp.sum(p, axis=1, keepdims=True)
        o_full = lax.dot_general(p.astype(BF16), mvt_ref[g].astype(BF16), _NT, preferred_element_type=F32) / l
        o_ref[g] = _head_diagonal(o_full, t, MEM_HEADS, head_dim)


def _mem_attn_sample(qm, mkt, mvt, group):
    db, t, _ = qm.shape
    n_mem = mkt.shape[2]
    blk = lambda i: (i, 0, 0)
    return pl.pallas_call(
        _mem_attn_sample_kernel,
        grid=(db // group,),
        in_specs=[pl.BlockSpec((group, t, D_MEM), blk),
                  pl.BlockSpec((group, D_MEM, n_mem), blk),
                  pl.BlockSpec((group, D_MEM, n_mem), blk)],
        out_specs=pl.BlockSpec((group, t, D_MEM), blk),
        out_shape=jax.ShapeDtypeStruct((db, t, D_MEM), F32),
        name="mem_attn_sample",
    )(qm, mkt, mvt)


def _merge_kernel(x_ref, attn_ref, ssm_ref, mem_ref, gate_ref, w_ref, g_ref, b_ref, y_ref):
    branches = jnp.concatenate([r[...].astype(F32) for r in (attn_ref, ssm_ref, mem_ref)], axis=1)
    mixed = branches * jax.nn.silu(gate_ref[...].astype(F32))
    out = jnp.dot(mixed.astype(BF16), w_ref[...], preferred_element_type=F32)
    z = DEEPNORM_ALPHA * x_ref[...] + out
    mu = jnp.mean(z, axis=1, keepdims=True)
    zc = z - mu
    var = jnp.mean(zc * zc, axis=1, keepdims=True)
    y_ref[...] = zc * lax.rsqrt(var + LN_EPS) * g_ref[...] + b_ref[...]


def _merge(x2d, attn, ssm, mem, gate, w_bf, g_row, b_row, tm, ssm_map):
    m = x2d.shape[0]
    row = lambda i: (i, 0)
    const = lambda i: (0, 0)
    return pl.pallas_call(
        _merge_kernel,
        grid=(m // tm,),
        in_specs=[pl.BlockSpec((tm, D_MODEL), row),
                  pl.BlockSpec((tm, D_ATTN), row),
                  pl.BlockSpec((tm, D_SSM), ssm_map),
                  pl.BlockSpec((tm, D_MEM), row),
                  pl.BlockSpec((tm, D_MIX), row),
                  pl.BlockSpec((D_MIX, D_MODEL), const),
                  pl.BlockSpec((1, D_MODEL), const),
                  pl.BlockSpec((1, D_MODEL), const)],
        out_specs=pl.BlockSpec((tm, D_MODEL), row),
        out_shape=jax.ShapeDtypeStruct((m, D_MODEL), F32),
        compiler_params=_vmem_limit(48 * 2**20),
        name="merge",
    )(x2d, attn, ssm, mem, gate, w_bf, g_row, b_row)


_PAGE_RING = 4
_SAMPLE_ROW_GROUPS = 4


def _moba_sample_parts(b, n_seq, pt_ref, q_ref, kn_ref, vn_ref, ck_ref, cv_ref, o_ref,
                       buf_ref, sem_ref, s_ref, p_ref, blk_ref, qbd_ref, pown_ref, l_ref, acc_ref,
                       *, pages_per_chunk, chunks, page_size):
    ppc = pages_per_chunk
    items = 2 * chunks
    t = q_ref.shape[1]
    rows = N_HEADS_A * t
    n_pages = ppc * chunks
    pages_per_block = MOBA_BLOCK // page_size
    n_blocks = n_pages // pages_per_block

    def pad_rows(x):
        return jnp.concatenate([x, jnp.zeros((page_size - x.shape[0], x.shape[1]), x.dtype)], axis=0)

    ring = buf_ref.shape[0]
    ahead = ring - 1
    assert items % ring == 0 and ahead <= items

    def chunk_copies(seq, r):
        src_ref, chunk = (ck_ref, r) if r < chunks else (cv_ref, r - chunks)
        slot = r % ring
        return [pltpu.make_async_copy(src_ref.at[pt_ref[seq, chunk * ppc + i]], buf_ref.at[slot, i],
                                      sem_ref.at[slot]) for i in range(ppc)]

    def prologue():
        @pl.when(b == 0)
        def _first_chunks():
            for r in range(ahead):
                for cp in chunk_copies(b, r):
                    cp.start()
        qbd_ref[...] = _head_rows(q_ref[0] * (HEAD_DIM ** -0.5), N_HEADS_A, HEAD_DIM).astype(BF16)

    def page_pair(slot, i):
        return jnp.concatenate([buf_ref[slot, i].astype(BF16), buf_ref[slot, i + 1].astype(BF16)], axis=1)

    def key_chunk(r, slot):
        qbd_bf = qbd_ref[...]
        for i in range(0, ppc, 2):
            s2 = jnp.dot(qbd_bf, page_pair(slot, i), preferred_element_type=F32)
            s_ref[r * ppc + i] = s2[:, :page_size]
            s_ref[r * ppc + i + 1] = s2[:, page_size:]

    def select_and_softmax():
        lo_all = lax.dot_general(qbd_ref[...], pad_rows(kn_ref[0]).astype(BF16), _NT, preferred_element_type=F32)
        gr = rows // _SAMPLE_ROW_GROUPS
        lane = lax.broadcasted_iota(jnp.int32, (gr, page_size), 1)
        t_row = lax.broadcasted_iota(jnp.int32, (gr, page_size), 0) % t
        for g in range(_SAMPLE_ROW_GROUPS):
            rs = pl.ds(g * gr, gr)
            sc = jnp.zeros((gr, page_size), F32)
            for j in range(n_blocks):
                bs = s_ref[j * pages_per_block, rs, :]
                bm = bs
                for pg in range(1, pages_per_block):
                    bs = bs + s_ref[j * pages_per_block + pg, rs, :]
                    bm = jnp.maximum(bm, s_ref[j * pages_per_block + pg, rs, :])
                sc = jnp.where(lane == j, jnp.sum(bs, axis=1, keepdims=True), sc)
                blk_ref[j, rs, :] = bm
            sel = _select_topk(sc, lane < n_blocks, lane.astype(F32), axis=1)
            lo = jnp.where(lane <= t_row, lo_all[g * gr:(g + 1) * gr], NEG_INF)
            mx = lo
            for j in range(n_blocks):
                mask = jnp.broadcast_to(jnp.where(sel[:, j:j + 1], 0.0, NEG_INF), (gr, page_size))
                mx = jnp.maximum(mx, blk_ref[j, rs, :] + mask)
                blk_ref[j, rs, :] = mask
            m = jnp.max(mx, axis=1, keepdims=True)
            po = jnp.exp(lo - m)
            lsum = po
            for j in range(n_blocks):
                shift = blk_ref[j, rs, :] - m
                for pg in range(j * pages_per_block, (j + 1) * pages_per_block):
                    p = jnp.exp(s_ref[pg, rs, :] + shift)
                    lsum = lsum + p
                    p_ref[pg, rs, :] = p.astype(BF16)
            pown_ref[rs, :] = po.astype(BF16)
            l_ref[rs, :] = jnp.broadcast_to(jnp.sum(lsum, axis=1, keepdims=True), (gr, page_size))
        acc_ref[...] = jnp.zeros_like(acc_ref)

    def value_chunk(r, slot):
        acc = acc_ref[...]
        for i in range(0, ppc, 2):
            pg = (r - chunks) * ppc + i
            p2 = jnp.concatenate([p_ref[pg], p_ref[pg + 1]], axis=1)
            acc = acc + lax.dot_general(p2, page_pair(slot, i), _NT, preferred_element_type=F32)
        acc_ref[...] = acc

    def item(r):
        slot = r % ring
        if r + ahead < items:
            for cp in chunk_copies(b, r + ahead):
                cp.start()
        else:
            @pl.when(b + 1 < n_seq)
            def _next_sequence():
                for cp in chunk_copies(b + 1, r + ahead - items):
                    cp.start()
        for cp in chunk_copies(b, r):
            cp.wait()
        if r < chunks:
            key_chunk(r, slot)
        else:
            if r == chunks:
                select_and_softmax()
            value_chunk(r, slot)

    def finish():
        acc = acc_ref[...] + jnp.dot(pown_ref[...], pad_rows(vn_ref[0]).astype(BF16), preferred_element_type=F32)
        l = jnp.concatenate([l_ref[...]] * (D_ATTN // page_size), axis=1)
        o_ref[0] = _head_diagonal(acc / l, t, N_HEADS_A, HEAD_DIM)

    return prologue, item, finish


def _moba_sample_scratch(t, n_pages, pages_per_chunk, page_size):
    rows = N_HEADS_A * t
    return [pltpu.VMEM((_PAGE_RING, pages_per_chunk, D_ATTN, page_size), F32),
            pltpu.SemaphoreType.DMA((_PAGE_RING,)),
            pltpu.VMEM((n_pages, rows, page_size), F32),
            pltpu.VMEM((n_pages, rows, page_size), BF16),
            pltpu.VMEM((n_pages * page_size // MOBA_BLOCK, rows, page_size), F32),
            pltpu.VMEM((rows, D_ATTN), BF16),
            pltpu.VMEM((rows, page_size), BF16),
            pltpu.VMEM((rows, page_size), F32),
            pltpu.VMEM((rows, D_ATTN), F32)]


_N_PROMPT_SCRATCH = 8


def _moba_seq_schedule(ti, tiles_per_head):
    singles = max(2, tiles_per_head // 4)
    two_from = tiles_per_head - (tiles_per_head - singles) // 2
    one_from = two_from - singles
    late = jnp.maximum(ti - two_from, 0)
    first = jnp.where(ti < two_from, jnp.maximum(ti - one_from, 0), singles + 2 * late)
    second = singles + 1 + 2 * late
    return first, second, ti >= one_from, ti >= two_from


def _moba_kernel(pt_ref, q_ref, k_ref, v_ref, kmean_ref, qsa_ref, kna_ref, vna_ref, qsb_ref, knb_ref, vnb_ref,
                 ck_ref, cv_ref, o_ref, os_ref, *scratch,
                 tiles_per_head, n_seq, pages_per_chunk, chunks, page_size):
    n = pl.program_id(0)
    ti = lax.rem(n, tiles_per_head)
    base = n - ti
    first, second, has_first, has_second = _moba_seq_schedule(ti, tiles_per_head)
    p_setup, p_pairs, p_finish = _moba_prompt_parts(
        ti, q_ref, k_ref, v_ref, kmean_ref, o_ref, *scratch[:_N_PROMPT_SCRATCH])
    sample = functools.partial(_moba_sample_parts, pages_per_chunk=pages_per_chunk, chunks=chunks,
                               page_size=page_size)
    sample_scratch = scratch[_N_PROMPT_SCRATCH:]
    osa_ref = os_ref.at[0, pl.ds(first & 1, 1)]
    osb_ref = os_ref.at[0, pl.ds(1, 1)]
    seqs = ((has_first, sample(base + first, n_seq, pt_ref, qsa_ref, kna_ref, vna_ref, ck_ref, cv_ref, osa_ref,
                               *sample_scratch)),
            (has_second, sample(base + second, n_seq, pt_ref, qsb_ref, knb_ref, vnb_ref, ck_ref, cv_ref, osb_ref,
                                *sample_scratch)))

    p_setup()
    pairs = ti * (q_ref.shape[1] // MOBA_BLOCK) // 2
    items = 2 * chunks
    every = 2
    per_seq = items // every
    shift = jnp.where(has_second, int(math.log2(2 * per_seq)), int(math.log2(per_seq)))

    def segment_start(g):
        return jnp.minimum(lax.shift_right_logical(g * pairs, shift), pairs)

    for i, (active, (s_prologue, s_item, s_finish)) in enumerate(seqs):
        pl.when(active)(s_prologue)
        for r in range(items):
            pl.when(active)(functools.partial(s_item, r))
            if r % every == every - 1:
                g = i * per_seq + r // every
                p_pairs(segment_start(g), segment_start(g + 1))
        pl.when(active)(s_finish)
    p_finish()


def _moba(q, kb, vb, kmean, q_s, k_new, v_new, cache_k, cache_v, page_table, pages_per_chunk):
    b, s, _ = q.shape
    nb = s // MOBA_BLOCK
    tq = _MOBA_Q_BLOCKS * MOBA_BLOCK
    assert nb <= LANES // 2 and nb % SUBLANES == 0 and s % tq == 0 and _MOBA_Q_BLOCKS % 2 == 0
    head_pairs = D_ATTN // LANES
    tiles_per_head = s // tq
    n_tiles = b * head_pairs * tiles_per_head

    db, t, _ = q_s.shape
    n_pages = page_table.shape[1]
    page_size = cache_k.shape[2]
    chunks = n_pages // pages_per_chunk
    assert (n_pages * page_size) % MOBA_BLOCK == 0 and MOBA_BLOCK % page_size == 0
    assert n_pages * page_size // MOBA_BLOCK >= MOBA_TOPK and t <= page_size and page_size == LANES
    assert n_pages % pages_per_chunk == 0 and pages_per_chunk % 2 == 0
    assert db == n_tiles and tiles_per_head % 4 == 0

    def tile_map(n, pt):
        return (n // (head_pairs * tiles_per_head), n % tiles_per_head, (n // tiles_per_head) % head_pairs)

    def head_map(n, pt):
        return (n // (head_pairs * tiles_per_head), 0, (n // tiles_per_head) % head_pairs)

    def first_map(n, pt):
        ti = n % tiles_per_head
        return (n - ti + _moba_seq_schedule(ti, tiles_per_head)[0], 0, 0)

    def second_map(n, pt):
        ti = n % tiles_per_head
        return (n - ti + _moba_seq_schedule(ti, tiles_per_head)[1], 0, 0)

    def pair_map(n, pt):
        return (first_map(n, pt)[0] // 2, 0, 0, 0)

    seq_block = (1, t, D_ATTN)
    grid_spec = pltpu.PrefetchScalarGridSpec(
        num_scalar_prefetch=1,
        grid=(n_tiles,),
        in_specs=[pl.BlockSpec((1, tq, LANES), tile_map),
                  pl.BlockSpec((1, s, LANES), head_map),
                  pl.BlockSpec((1, s, LANES), head_map),
                  pl.BlockSpec((1, nb, LANES), head_map)]
                 + [pl.BlockSpec(seq_block, first_map)] * 3
                 + [pl.BlockSpec(seq_block, second_map)] * 3
                 + [pl.BlockSpec(memory_space=pl.ANY)] * 2,
        out_specs=(pl.BlockSpec((1, tq, LANES), tile_map), pl.BlockSpec((1, 2, t, D_ATTN), pair_map)),
        scratch_shapes=_moba_prompt_scratch(s, tq) + _moba_sample_scratch(t, n_pages, pages_per_chunk, page_size))
    attn, attn_s = pl.pallas_call(
        functools.partial(_moba_kernel, tiles_per_head=tiles_per_head, n_seq=db,
                          pages_per_chunk=pages_per_chunk, chunks=chunks, page_size=page_size),
        grid_spec=grid_spec,
        out_shape=(jax.ShapeDtypeStruct((b, s, D_ATTN), BF16), jax.ShapeDtypeStruct((db // 2, 2, t, D_ATTN), F32)),
        compiler_params=pltpu.CompilerParams(dimension_semantics=("arbitrary",),
                                             vmem_limit_bytes=52 * 2**20),
        name="moba",
    )(page_table, q, kb, vb, kmean, q_s, k_new, v_new, q_s, k_new, v_new, cache_k, cache_v)
    return attn, attn_s.reshape(db, t, D_ATTN)


def kernel(x_prompt, x_sample, mem_prompt, cache_k, cache_v, state_ssm_re, state_ssm_im,
           cache_mem_k, cache_mem_v, page_table, w_in, w_mem_kv, a_re, a_im, log_dt,
           b_re, b_im, c_re, c_im, d_skip, w_glu, b_glu, w_out, ln_g, ln_b):
    batch, seq, _ = x_prompt.shape
    db, t_new, _ = x_sample.shape
    n_mem = mem_prompt.shape[1]
    n_phys, page_size = cache_k.shape[:2]
    n_seg = SSM_CHAINS // batch
    seg_len = seq // n_seg
    tm = min(512, seg_len)

    w_in_bf = w_in.astype(BF16)
    w_mem_bf = w_mem_kv.astype(BF16)
    w_out_bf = w_out.astype(BF16)
    wg_bf = w_glu.astype(BF16)
    bg_row = b_glu.astype(F32).reshape(1, 2 * D_SSM)
    d_row = d_skip.astype(F32).reshape(1, D_SSM)
    g_row = ln_g.astype(F32).reshape(1, D_MODEL)
    b_row = ln_b.astype(F32).reshape(1, D_MODEL)
    a_row, b_mat, c_mat = _ssm_operands(a_re, a_im, log_dt, b_re, b_im, c_re, c_im)
    b_bf = b_mat.astype(BF16)
    c_bf = c_mat.astype(BF16)

    x2d = x_prompt.reshape(batch * seq, D_MODEL)
    q, k_t, v_t, u_chain, qm, gate, kb, vb, kmean = _in_proj_prompt(x2d, w_in_bf, batch, seq, tm)
    xs2d = x_sample.reshape(db * t_new, D_MODEL)
    tms = min(512, db * t_new)
    q_s, k_s, v_s, u_s, qm_s, gate_s = _in_proj_sample(xs2d, w_in_bf, tms)
    attn, attn_s = _moba(q.reshape(batch, seq, D_ATTN), kb.reshape(batch, seq, D_ATTN),
                         vb.reshape(batch, seq, D_ATTN), kmean.reshape(batch, seq // MOBA_BLOCK, D_ATTN),
                         q_s.reshape(db, t_new, D_ATTN), k_s.reshape(db, t_new, D_ATTN),
                         v_s.reshape(db, t_new, D_ATTN),
                         cache_k.transpose(0, 2, 3, 1).reshape(n_phys, D_ATTN, page_size),
                         cache_v.transpose(0, 2, 3, 1).reshape(n_phys, D_ATTN, page_size), page_table,
                         pages_per_chunk=min(16, page_table.shape[1] // 2))

    y_chain, hfin = _ssm_prompt(u_chain.reshape(seg_len * SSM_CHAINS, D_SSM), a_row, b_bf, c_bf, d_row,
                                wg_bf, bg_row, n_seg, steps_per_chunk=min(128, seg_len))
    mk, mv = _mem_kv(mem_prompt.reshape(batch * n_mem, D_MODEL), w_mem_bf)
    mem_o = _mem_attn_prompt(qm, mk, mv, batch, seq, tm)
    tiles_per_seq = seq // tm
    tiles_per_seg = seg_len // tm

    def chain_map(i):
        b = i // tiles_per_seq
        ti = i % tiles_per_seq
        return (ti % tiles_per_seg, b * n_seg + ti // tiles_per_seg)

    y_prompt = _merge(x2d, attn.reshape(batch * seq, D_ATTN), y_chain.reshape(seg_len, SSM_CHAINS * D_SSM),
                      mem_o, gate, w_out_bf, g_row, b_row, tm, chain_map)
    h_last = hfin[n_seg - 1::n_seg]
    ssm_re_p = h_last[:, :SSM_LANES].reshape(batch, SSM_GROUPS, SSM_STATE)
    ssm_im_p = h_last[:, SSM_LANES:].reshape(batch, SSM_GROUPS, SSM_STATE)

    h0 =jnp.concatenate([state_ssm_re.astype(F32).reshape(db, SSM_LANES),
                          state_ssm_im.astype(F32).reshape(db, SSM_LANES)], axis=1)
    y_tb, hfin_s = _ssm_sample(u_s.reshape(db, t_new, D_SSM).transpose(1, 0, 2), h0, a_row, b_mat, c_bf,
                               d_row, wg_bf, bg_row)
    ssm_s = y_tb.transpose(1, 0, 2).reshape(db * t_new, D_SSM)
    mem_s = _mem_attn_sample(qm_s.reshape(db, t_new, D_MEM),
                             cache_mem_k.transpose(0, 2, 3, 1).reshape(db, D_MEM, n_mem),
                             cache_mem_v.transpose(0, 2, 3, 1).reshape(db, D_MEM, n_mem), group=min(8, db))
    y_sample = _merge(xs2d, attn_s.reshape(db * t_new, D_ATTN), ssm_s, mem_s.reshape(db * t_new, D_MEM),
                      gate_s, w_out_bf, g_row, b_row, tms, lambda i: (i, 0))

    heads = (N_HEADS_A, HEAD_DIM)
    mem_heads = (MEM_HEADS, D_MEM // MEM_HEADS)
    return (y_prompt.reshape(batch, seq, D_MODEL), y_sample.reshape(db, t_new, D_MODEL),
            k_t.reshape(batch, *heads, seq).transpose(0, 3, 1, 2),
            v_t.reshape(batch, *heads, seq).transpose(0, 3, 1, 2), ssm_re_p, ssm_im_p,
            mk.reshape(batch, n_mem, *mem_heads), mv.reshape(batch, n_mem, *mem_heads),
            k_s.reshape(db, t_new, *heads), v_s.reshape(db, t_new, *heads),
            hfin_s[:, :SSM_LANES].reshape(db, SSM_GROUPS, SSM_STATE),
            hfin_s[:, SSM_LANES:].reshape(db, SSM_GROUPS, SSM_STATE))
```

```python
import functools
import math

import jax
import jax.numpy as jnp
from jax import lax
from jax.experimental import pallas as pl
from jax.experimental.pallas import tpu as pltpu

F32 = jnp.float32
BF16 = jnp.bfloat16

D_MODEL = 1024
D_ATTN = 512
D_SSM = 256
D_MEM = 256
D_MIX = D_ATTN + D_SSM + D_MEM
HEAD_DIM = 64
N_HEADS_A = D_ATTN // HEAD_DIM
MOBA_BLOCK = 256
MOBA_TOPK = 3
SSM_CH = 16
SSM_GROUPS = D_SSM // SSM_CH
SSM_STATE = 64
SSM_LANES = SSM_GROUPS * SSM_STATE
MEM_HEADS = 4
D_IN = 3 * D_ATTN + D_SSM + D_MEM + D_MIX
DEPTH = 1
DEEPNORM_ALPHA = (2.0 * DEPTH) ** 0.25
LN_EPS = 1e-5
NEG_INF = -1e30

LANES = 128
SUBLANES = 8
SSM_CHAINS = SUBLANES

_NT = (((1,), (1,)), ((), ()))


def _vmem_limit(nbytes):
    return pltpu.CompilerParams(vmem_limit_bytes=int(nbytes))


def _proj_kernel(x_ref, w_ref, *out_refs, with_prompt_extras):
    xb = x_ref[...].astype(BF16)

    def seg(lo, hi):
        return jnp.dot(xb, w_ref[:, lo:hi], preferred_element_type=F32)

    c0, c1, c2, c3, c4 = D_ATTN, 2 * D_ATTN, 3 * D_ATTN, 3 * D_ATTN + D_SSM, 3 * D_ATTN + D_SSM + D_MEM
    if with_prompt_extras:
        q_ref, k_ref, v_ref, u_ref, qm_ref, gate_ref, kb_ref, vb_ref, kmean_ref = out_refs
    else:
        q_ref, k_ref, v_ref, u_ref, qm_ref, gate_ref = out_refs
    q_ref[...] = seg(0, c0)
    k = seg(c0, c1)
    v = seg(c1, c2)
    u_ref[...] = seg(c2, c3)
    qm_ref[...] = seg(c3, c4).astype(qm_ref.dtype)
    gate_ref[...] = seg(c4, D_IN).astype(gate_ref.dtype)
    if not with_prompt_extras:
        k_ref[...] = k
        v_ref[...] = v
    else:
        k_ref[0] = k.T
        v_ref[0] = v.T
        kb_ref[...] = k.astype(BF16)
        vb_ref[...] = v.astype(BF16)
        tm = k.shape[0]
        kmean_ref[0] = jnp.sum(k.reshape(tm // MOBA_BLOCK, MOBA_BLOCK, D_ATTN), axis=1) * (1.0 / MOBA_BLOCK)


def _in_proj_prompt(x2d, w_bf, batch, seq, tm):
    m = x2d.shape[0]
    n_seg = SSM_CHAINS // batch
    seg_len = seq // n_seg
    tiles_per_seq = seq // tm
    tiles_per_seg = seg_len // tm
    row = lambda i: (i, 0)

    def u_map(i):
        b = i // tiles_per_seq
        t = i % tiles_per_seq
        return (t % tiles_per_seg, b * n_seg + t // tiles_per_seg)

    out_shape = (
        jax.ShapeDtypeStruct((m, D_ATTN), F32),
        jax.ShapeDtypeStruct((batch, D_ATTN, seq), F32),
        jax.ShapeDtypeStruct((batch, D_ATTN, seq), F32),
        jax.ShapeDtypeStruct((seg_len, SSM_CHAINS * D_SSM), F32),
        jax.ShapeDtypeStruct((m, D_MEM), BF16),
        jax.ShapeDtypeStruct((m, D_MIX), BF16),
        jax.ShapeDtypeStruct((m, D_ATTN), BF16),
        jax.ShapeDtypeStruct((m, D_ATTN), BF16),
        jax.ShapeDtypeStruct((m // tm, tm // MOBA_BLOCK, D_ATTN), F32),
    )
    kv_t = lambda i: (i // tiles_per_seq, 0, i % tiles_per_seq)
    out_specs = (
        pl.BlockSpec((tm, D_ATTN), row),
        pl.BlockSpec((1, D_ATTN, tm), kv_t),
        pl.BlockSpec((1, D_ATTN, tm), kv_t),
        pl.BlockSpec((tm, D_SSM), u_map),
        pl.BlockSpec((tm, D_MEM), row),
        pl.BlockSpec((tm, D_MIX), row),
        pl.BlockSpec((tm, D_ATTN), row),
        pl.BlockSpec((tm, D_ATTN), row),
        pl.BlockSpec((1, tm // MOBA_BLOCK, D_ATTN), lambda i: (i, 0, 0)),
    )
    return pl.pallas_call(
        functools.partial(_proj_kernel, with_prompt_extras=True),
        grid=(m // tm,),
        in_specs=[pl.BlockSpec((tm, D_MODEL), row),
                  pl.BlockSpec((D_MODEL, D_IN), lambda i: (0, 0), pipeline_mode=pl.Buffered(1))],
        out_specs=out_specs,
        out_shape=out_shape,
        compiler_params=_vmem_limit(52 * 2**20),
        name="in_proj_prompt",
    )(x2d, w_bf)


def _in_proj_sample(x2d, w_bf, tm):
    m = x2d.shape[0]
    row = lambda i: (i, 0)
    out_shape = (
        jax.ShapeDtypeStruct((m, D_ATTN), F32),
        jax.ShapeDtypeStruct((m, D_ATTN), F32),
        jax.ShapeDtypeStruct((m, D_ATTN), F32),
        jax.ShapeDtypeStruct((m, D_SSM), F32),
        jax.ShapeDtypeStruct((m, D_MEM), BF16),
        jax.ShapeDtypeStruct((m, D_MIX), BF16),
    )
    widths = (D_ATTN, D_ATTN, D_ATTN, D_SSM, D_MEM, D_MIX)
    return pl.pallas_call(
        functools.partial(_proj_kernel, with_prompt_extras=False),
        grid=(m // tm,),
        in_specs=[pl.BlockSpec((tm, D_MODEL), row),
                  pl.BlockSpec((D_MODEL, D_IN), lambda i: (0, 0), pipeline_mode=pl.Buffered(1))],
        out_specs=tuple(pl.BlockSpec((tm, w), row) for w in widths),
        out_shape=out_shape,
        compiler_params=_vmem_limit(52 * 2**20),
        name="in_proj_sample",
    )(x2d, w_bf)


def _mem_kv_kernel(mem_ref, w_ref, mk_ref, mv_ref):
    kv = jnp.dot(mem_ref[...].astype(BF16), w_ref[...], preferred_element_type=F32)
    mk_ref[...] = kv[:, :D_MEM]
    mv_ref[...] = kv[:, D_MEM:]


def _mem_kv(mem2d, w_bf):
    m = mem2d.shape[0]
    tm = min(m, 512)
    row = lambda i: (i, 0)
    return pl.pallas_call(
        _mem_kv_kernel,
        grid=(m // tm,),
        in_specs=[pl.BlockSpec((tm, D_MODEL), row),
                  pl.BlockSpec((D_MODEL, 2 * D_MEM), lambda i: (0, 0))],
        out_specs=(pl.BlockSpec((tm, D_MEM), row), pl.BlockSpec((tm, D_MEM), row)),
        out_shape=(jax.ShapeDtypeStruct((m, D_MEM), F32), jax.ShapeDtypeStruct((m, D_MEM), F32)),
        name="mem_kv",
    )(mem2d, w_bf)


def _select_topk(scores, valid, index, axis):
    lowest = jnp.finfo(F32).min
    s = jnp.where(valid, scores, lowest)
    sel = jnp.zeros(scores.shape, dtype=jnp.bool_)
    for _ in range(MOBA_TOPK):
        m = jnp.max(s, axis=axis, keepdims=True)
        first = jnp.min(jnp.where(s == m, index, jnp.float32(2**30)), axis=axis, keepdims=True)
        pick = index == first
        sel = sel | (pick & valid)
        s = jnp.where(pick, lowest, s)
    return sel


_MOBA_Q_BLOCKS = 2
_MOBA_ROW_GROUP = 128


def _moba_prompt_parts(ti, q_ref, k_ref, v_ref, kmean_ref, o_ref,
                       kaug_ref, vaug_ref, lhs_ref, s_ref, p_ref, alpha_ref, m_ref, acc_ref):
    blk = MOBA_BLOCK
    tq = q_ref.shape[1]
    qb = tq // blk
    nb = k_ref.shape[1] // blk
    half = LANES // 2

    def build_operands():
        lane_b = lax.broadcasted_iota(jnp.int32, (blk, LANES), 1)
        head0_b = lane_b < half

        def body(j, carry):
            rows = pl.ds(pl.multiple_of(j * blk, blk), blk)
            kb = k_ref[0, rows, :].astype(F32)
            vb = v_ref[0, rows, :].astype(F32)
            hot0 = jnp.where(lane_b == half + j, 1.0, 0.0)
            hot1 = jnp.where(lane_b == j, 1.0, 0.0)
            kaug_ref[0, rows, :] = jnp.where(head0_b, kb, hot0).astype(BF16)
            kaug_ref[1, rows, :] = jnp.where(head0_b, hot1, kb).astype(BF16)
            vaug_ref[0, rows, :] = jnp.where(head0_b, vb, 1.0).astype(BF16)
            vaug_ref[1, rows, :] = jnp.where(head0_b, 1.0, vb).astype(BF16)
            return carry
        lax.fori_loop(0, nb, body, 0)

    def select_blocks():
        q2 = q_ref[0]
        means = kmean_ref[0]
        lane_m = lax.broadcasted_iota(jnp.int32, (nb, LANES), 1)
        pad = [jnp.zeros((half - nb, LANES), F32)] if nb < half else []
        means_p = jnp.concatenate([jnp.where(lane_m >= half, means, 0.0)] + pad
                                  + [jnp.where(lane_m < half, means, 0.0)] + pad, axis=0)
        sc_t = lax.dot_general(means_p, q2, _NT, precision=lax.Precision.HIGHEST,
                               preferred_element_type=F32)
        slot = lax.broadcasted_iota(jnp.int32, (nb, tq), 0)
        slot_f = slot.astype(F32)
        n_full = ti * qb + lax.broadcasted_iota(jnp.int32, (nb, tq), 1) // blk
        bias_t = []
        for off in (0, half):
            sel = _select_topk(sc_t[off:off + nb], slot < n_full, slot_f, axis=0)
            bias_t.append(jnp.where(sel | (slot == n_full), 0.0, NEG_INF))
            if nb < half:
                bias_t.append(jnp.zeros((half - nb, tq), F32))
        bias = jnp.concatenate(bias_t, axis=0).T
        head0_q = lax.broadcasted_iota(jnp.int32, (tq, LANES), 1) < half
        qs = q2 * (HEAD_DIM ** -0.5)
        lhs_ref[0] = jnp.where(head0_q, qs, bias).astype(BF16)
        lhs_ref[1] = jnp.where(head0_q, bias, qs).astype(BF16)
        m_ref[...] = jnp.full(m_ref.shape, NEG_INF, F32)
        acc_ref[...] = jnp.zeros_like(acc_ref)

    rg = _MOBA_ROW_GROUP
    c_i = lax.broadcasted_iota(jnp.int32, (rg, blk), 1)
    r_i = lax.broadcasted_iota(jnp.int32, (rg, blk), 0)

    def key_rows(j):
        return pl.ds(j * blk if isinstance(j, int) else pl.multiple_of(j * blk, blk), blk)

    def score_block(j, slot):
        for e in (0, 1):
            s_ref[slot, e] = lax.dot_general(lhs_ref[e], kaug_ref[e, key_rows(j), :], _NT,
                                             preferred_element_type=F32)

    def softmax_pv_block(j, slot, causal):
        for e in (0, 1):
            for g in range(tq // rg):
                rows = pl.ds(g * rg, rg)
                s = s_ref[slot, e, rows, :]
                if causal:
                    s = jnp.where(j * blk + c_i <= ti * tq + g * rg + r_i, s, NEG_INF)
                m_old = m_ref[e, rows, :]
                m_new = jnp.maximum(m_old, jnp.max(s, axis=1, keepdims=True))
                alpha_ref[e, rows, :] = jnp.exp(m_old - m_new)
                p_ref[e, rows, :] = jnp.exp(s - jnp.concatenate([m_new] * (blk // LANES), axis=1)).astype(BF16)
                m_ref[e, rows, :] = m_new
            pv = jnp.dot(p_ref[e], vaug_ref[e, key_rows(j), :], preferred_element_type=F32)
            acc_ref[e] = alpha_ref[e] * acc_ref[e] + pv

    n_past = ti * qb

    def setup():
        pl.when(ti == 0)(build_operands)
        select_blocks()
        score_block(0, 0)

    def past_pair(jj, carry):
        j = 2 * jj
        score_block(j + 1, 1)
        softmax_pv_block(j, 0, causal=False)
        score_block(j + 2, 0)
        softmax_pv_block(j + 1, 1, causal=False)
        return carry

    def past_pairs(lo, hi):
        lax.fori_loop(lo, hi, past_pair, 0)

    def finish():
        for d in range(qb):
            if d + 1 < qb:
                score_block(n_past + d + 1, (d + 1) % 2)
            softmax_pv_block(n_past + d, d % 2, causal=True)
        a0 = acc_ref[0]
        a1 = acc_ref[1]
        o0 = a0 / pltpu.roll(a0, half, axis=1)
        o1 = a1 / pltpu.roll(a1, half, axis=1)
        head0_q = lax.broadcasted_iota(jnp.int32, (tq, LANES), 1) < half
        o_ref[0] = jnp.where(head0_q, o0, o1).astype(o_ref.dtype)

    return setup, past_pairs, finish


def _moba_prompt_scratch(s, tq):
    return [pltpu.VMEM((2, s, LANES), BF16),
            pltpu.VMEM((2, s, LANES), BF16),
            pltpu.VMEM((2, tq, LANES), BF16),
            pltpu.VMEM((2, 2, tq, MOBA_BLOCK), F32),
            pltpu.VMEM((2, tq, MOBA_BLOCK), BF16),
            pltpu.VMEM((2, tq, LANES), F32),
            pltpu.VMEM((2, tq, LANES), F32),
            pltpu.VMEM((2, tq, LANES), F32)]


def _ssm_disc_kernel(a_re_ref, a_im_ref, log_dt_ref, bt_re_ref, bt_im_ref,
                     abar_re_ref, abar_im_ref, bbt_re_ref, bbt_im_ref):
    a_re = a_re_ref[...]
    a_im = a_im_ref[...]
    dt = jnp.exp(log_dt_ref[...])
    mag = jnp.exp(a_re * dt)
    ang = a_im * dt
    abar_re = mag * jnp.cos(ang)
    abar_im = mag * jnp.sin(ang)
    den = a_re * a_re + a_im * a_im
    f_re = ((abar_re - 1.0) * a_re + abar_im * a_im) / den
    f_im = (abar_im * a_re - (abar_re - 1.0) * a_im) / den
    abar_re_ref[...] = abar_re
    abar_im_ref[...] = abar_im
    bt_re = bt_re_ref[...]
    bt_im = bt_im_ref[...]
    bbt_re_ref[...] = f_re[:, None, :] * bt_re - f_im[:, None, :] * bt_im
    bbt_im_ref[...] = f_re[:, None, :] * bt_im + f_im[:, None, :] * bt_re


def _ssm_operands(a_re, a_im, log_dt, b_re, b_im, c_re, c_im):
    g, p, c = b_re.shape
    abar_re, abar_im, bbt_re, bbt_im = pl.pallas_call(
        _ssm_disc_kernel,
        out_shape=(jax.ShapeDtypeStruct((g, p), F32), jax.ShapeDtypeStruct((g, p), F32),
                   jax.ShapeDtypeStruct((g, c, p), F32), jax.ShapeDtypeStruct((g, c, p), F32)),
        name="ssm_discretize",
    )(a_re.astype(F32), a_im.astype(F32), log_dt.astype(F32).reshape(g, 1),
      b_re.astype(F32).transpose(0, 2, 1), b_im.astype(F32).transpose(0, 2, 1))
    eye = jnp.eye(g, dtype=F32)

    def in_block_diag(bbt):
        return jnp.einsum('gcp,gh->gchp', bbt, eye).reshape(g * c, g * p)

    def out_block_diag(cm):
        return jnp.einsum('gcp,gh->gphc', cm, eye).reshape(g * p, g * c)

    b_mat = jnp.concatenate([in_block_diag(bbt_re), in_block_diag(bbt_im)], axis=1)
    c_mat = jnp.concatenate([out_block_diag(c_re.astype(F32)),
                             -out_block_diag(c_im.astype(F32))], axis=0)
    a_row = jnp.concatenate([abar_re.reshape(1, g * p), abar_im.reshape(1, g * p)], axis=1)
    return a_row, b_mat, c_mat


def _complex_step(a_re, a_im, h_re, h_im, bu_re, bu_im):
    return (a_re * h_re - a_im * h_im + bu_re, a_re * h_im + a_im * h_re + bu_im)


def _ssm_output(h_all, u, c_ref, d_ref, wg_ref, bg_ref):
    y = jnp.dot(h_all.astype(BF16), c_ref[...], preferred_element_type=F32) + d_ref[...] * u
    y = jax.nn.gelu(y)
    z = jnp.dot(y.astype(BF16), wg_ref[...], preferred_element_type=F32) + bg_ref[...]
    return z[:, :D_SSM] * jax.nn.sigmoid(z[:, D_SSM:])


def _ssm_prompt_kernel(u_ref, a_ref, b_ref, c_ref, d_ref, wg_ref, bg_ref, y_ref, hfin_ref,
                       hb_ref, h_ref, init_ref, *, n_seg):
    pss = pl.program_id(0)
    c = pl.program_id(1)
    n_chunks = pl.num_programs(1)
    rows = u_ref.shape[0]
    steps = rows // SSM_CHAINS
    n = SSM_LANES
    a_re = jnp.broadcast_to(a_ref[:, :n], (SSM_CHAINS, n))
    a_im = jnp.broadcast_to(a_ref[:, n:], (SSM_CHAINS, n))

    @pl.when((pss == 0) & (c == 0))
    def _zero_state():
        h_ref[...] = jnp.zeros_like(h_ref)

    @pl.when((pss == 1) & (c == 0))
    def _true_initial_state():
        h_ref[...] = init_ref[...]

    u = u_ref[...]
    hb_ref[...] = jnp.dot(u.astype(BF16), b_ref[...], preferred_element_type=F32)

    def scan(store):
        def step(s, carry):
            h_re, h_im = carry
            r = pl.ds(pl.multiple_of(s * SSM_CHAINS, SSM_CHAINS), SSM_CHAINS)
            h_re, h_im = _complex_step(a_re, a_im, h_re, h_im, hb_ref[r, :n], hb_ref[r, n:])
            if store:
                hb_ref[r, :n] = h_re
                hb_ref[r, n:] = h_im
            return h_re, h_im
        h_re, h_im = lax.fori_loop(0, steps, step, (h_ref[:, :n], h_ref[:, n:]), unroll=4)
        h_ref[:, :n] = h_re
        h_ref[:, n:] = h_im

    @pl.when(pss == 0)
    def _local_pass():
        scan(False)

    @pl.when((pss == 0) & (c == n_chunks - 1))
    def _segment_initial_states():
        seg_len = steps * n_chunks
        p_re, p_im = a_re, a_im
        for _ in range(int(math.log2(seg_len))):
            p_re, p_im = p_re * p_re - p_im * p_im, 2.0 * p_re * p_im
        e_re = h_ref[:, :n]
        e_im = h_ref[:, n:]
        first = (lax.broadcasted_iota(jnp.int32, (SSM_CHAINS, n), 0) % n_seg) == 0
        i_re = jnp.zeros((SSM_CHAINS, n), F32)
        i_im = jnp.zeros((SSM_CHAINS, n), F32)
        for _ in range(n_seg - 1):
            t_re = p_re * i_re - p_im * i_im + e_re
            t_im = p_re * i_im + p_im * i_re + e_im
            i_re = jnp.where(first, 0.0, pltpu.roll(t_re, 1, axis=0))
            i_im = jnp.where(first, 0.0, pltpu.roll(t_im, 1, axis=0))
        init_ref[:, :n] = i_re
        init_ref[:, n:] = i_im

    @pl.when(pss == 1)
    def _output_pass():
        scan(True)
        y_ref[...] = _ssm_output(hb_ref[...], u, c_ref, d_ref, wg_ref, bg_ref).astype(y_ref.dtype)

    @pl.when((pss == 1) & (c == n_chunks - 1))
    def _final_state():
        hfin_ref[...] = h_ref[...]


def _ssm_prompt(u_chain, a_row, b_bf, c_bf, d_row, wg_bf, bg_row, n_seg, steps_per_chunk):
    total = u_chain.shape[0]
    rows = steps_per_chunk * SSM_CHAINS
    n_chunks = total // rows
    seg_len = total // SSM_CHAINS
    assert 2 ** int(math.log2(seg_len)) == seg_len
    const = lambda p, c: (0, 0)
    return pl.pallas_call(
        functools.partial(_ssm_prompt_kernel, n_seg=n_seg),
        grid=(2, n_chunks),
        in_specs=[pl.BlockSpec((rows, D_SSM), lambda p, c: (c, 0)),
                  pl.BlockSpec((1, 2 * SSM_LANES), const),
                  pl.BlockSpec((D_SSM, 2 * SSM_LANES), const),
                  pl.BlockSpec((2 * SSM_LANES, D_SSM), const),
                  pl.BlockSpec((1, D_SSM), const),
                  pl.BlockSpec((D_SSM, 2 * D_SSM), const),
                  pl.BlockSpec((1, 2 * D_SSM), const)],
        out_specs=(pl.BlockSpec((rows, D_SSM), lambda p, c: (p * c, 0)),
                   pl.BlockSpec((SSM_CHAINS, 2 * SSM_LANES), const)),
        out_shape=(jax.ShapeDtypeStruct((total, D_SSM), BF16),
                   jax.ShapeDtypeStruct((SSM_CHAINS, 2 * SSM_LANES), F32)),
        scratch_shapes=[pltpu.VMEM((rows, 2 * SSM_LANES), F32),
                        pltpu.VMEM((SSM_CHAINS, 2 * SSM_LANES), F32),
                        pltpu.VMEM((SSM_CHAINS, 2 * SSM_LANES), F32)],
        compiler_params=pltpu.CompilerParams(dimension_semantics=("arbitrary", "arbitrary"),
                                             vmem_limit_bytes=40 * 2**20),
        name="ssm_prompt",
    )(u_chain, a_row, b_bf, c_bf, d_row, wg_bf, bg_row)


def _ssm_sample_kernel(u_ref, h0_ref, a_ref, b_ref, c_ref, d_ref, wg_ref, bg_ref, y_ref, hfin_ref):
    steps, batch, _ = u_ref.shape
    n = SSM_LANES
    a_re = jnp.broadcast_to(a_ref[:, :n], (batch, n))
    a_im = jnp.broadcast_to(a_ref[:, n:], (batch, n))
    h_re = h0_ref[:, :n]
    h_im = h0_ref[:, n:]
    for t in range(steps):
        u = u_ref[t]
        bu = jnp.dot(u, b_ref[...], precision=lax.Precision.HIGHEST, preferred_element_type=F32)
        h_re, h_im = _complex_step(a_re, a_im, h_re, h_im, bu[:, :n], bu[:, n:])
        h_all = jnp.concatenate([h_re, h_im], axis=1)
        y_ref[t] = _ssm_output(h_all, u, c_ref, d_ref, wg_ref, bg_ref)
    hfin_ref[:, :n] = h_re
    hfin_ref[:, n:] = h_im


def _ssm_sample(u_tb, h0, a_row, b_f32, c_bf, d_row, wg_bf, bg_row):
    steps, batch, _ = u_tb.shape
    return pl.pallas_call(
        _ssm_sample_kernel,
        out_shape=(jax.ShapeDtypeStruct((steps, batch, D_SSM), F32),
                   jax.ShapeDtypeStruct((batch, 2 * SSM_LANES), F32)),
        compiler_params=_vmem_limit(40 * 2**20),
        name="ssm_sample",
    )(u_tb, h0, a_row, b_f32, c_bf, d_row, wg_bf, bg_row)


def _mem_attn_prompt_kernel(q_ref, mk_ref, mv_ref, o_ref):
    rows = q_ref.shape[0]
    half = LANES // 2
    scale = (D_MEM // MEM_HEADS) ** -0.5
    head0 = lax.broadcasted_iota(jnp.int32, (rows, LANES), 1) < half
    outs = []
    for hp in range(D_MEM // LANES):
        cols = slice(hp * LANES, (hp + 1) * LANES)
        q2 = q_ref[:, cols].astype(F32)
        k2 = mk_ref[:, cols].astype(BF16)
        v2 = mv_ref[:, cols].astype(BF16)
        res = []
        for e in (0, 1):
            mine = head0 if e == 0 else jnp.logical_not(head0)
            lhs = jnp.where(mine, q2, 0.0).astype(BF16)
            s = lax.dot_general(lhs, k2, _NT, preferred_element_type=F32) * scale
            p = jnp.exp(s - jnp.max(s, axis=1, keepdims=True))
            l = jnp.sum(p, axis=1, keepdims=True)
            res.append(jnp.dot(p.astype(BF16), v2, preferred_element_type=F32) / l)
        outs.append(jnp.where(head0, res[0], res[1]))
    o_ref[...] = jnp.concatenate(outs, axis=1).astype(o_ref.dtype)


def _mem_attn_prompt(qm, mk, mv, batch, seq, tm):
    n_mem = mk.shape[0] // batch
    tiles = seq // tm
    return pl.pallas_call(
        _mem_attn_prompt_kernel,
        grid=(batch, tiles),
        in_specs=[pl.BlockSpec((tm, D_MEM), lambda b, i: (b * tiles + i, 0)),
                  pl.BlockSpec((n_mem, D_MEM), lambda b, i: (b, 0)),
                  pl.BlockSpec((n_mem, D_MEM), lambda b, i: (b, 0))],
        out_specs=pl.BlockSpec((tm, D_MEM), lambda b, i: (b * tiles + i, 0)),
        out_shape=jax.ShapeDtypeStruct((batch * seq, D_MEM), BF16),
        name="mem_attn_prompt",
    )(qm, mk, mv)


def _head_rows(x, n_heads, head_dim):
    head_of_lane = lax.broadcasted_iota(jnp.int32, x.shape, 1) // head_dim
    return jnp.concatenate([jnp.where(head_of_lane == h, x, 0.0) for h in range(n_heads)], axis=0)


def _head_diagonal(o_full, t, n_heads, head_dim):
    head_of_lane = lax.broadcasted_iota(jnp.int32, (t, o_full.shape[1]), 1) // head_dim
    out = jnp.zeros((t, o_full.shape[1]), F32)
    for h in range(n_heads):
        out = out + jnp.where(head_of_lane == h, o_full[h * t:(h + 1) * t], 0.0)
    return out


def _mem_attn_sample_kernel(q_ref, mkt_ref, mvt_ref, o_ref):
    group, t, _ = q_ref.shape
    head_dim = D_MEM // MEM_HEADS
    scale = head_dim ** -0.5
    for g in range(group):
        qbd = _head_rows(q_ref[g].astype(F32) * scale, MEM_HEADS, head_dim).astype(BF16)
        s = jnp.dot(qbd, mkt_ref[g].astype(BF16), preferred_element_type=F32)
        p = jnp.exp(s - jnp.max(s, axis=1, keepdims=True))
        l = jnp.sum(p, axis=1, keepdims=True)
        o_full = lax.dot_general(p.astype(BF16), mvt_ref[g].astype(BF16), _NT, preferred_element_type=F32) / l
        o_ref[g] = _head_diagonal(o_full, t, MEM_HEADS, head_dim)


def _mem_attn_sample(qm, mkt, mvt, group):
    db, t, _ = qm.shape
    n_mem = mkt.shape[2]
    blk = lambda i: (i, 0, 0)
    return pl.pallas_call(
        _mem_attn_sample_kernel,
        grid=(db // group,),
        in_specs=[pl.BlockSpec((group, t, D_MEM), blk),
                  pl.BlockSpec((group, D_MEM, n_mem), blk),
                  pl.BlockSpec((group, D_MEM, n_mem), blk)],
        out_specs=pl.BlockSpec((group, t, D_MEM), blk),
        out_shape=jax.ShapeDtypeStruct((db, t, D_MEM), F32),
        name="mem_attn_sample",
    )(qm, mkt, mvt)


def _merge_kernel(x_ref, attn_ref, ssm_ref, mem_ref, gate_ref, w_ref, g_ref, b_ref, y_ref):
    branches = jnp.concatenate([r[...].astype(F32) for r in (attn_ref, ssm_ref, mem_ref)], axis=1)
    mixed = branches * jax.nn.silu(gate_ref[...].astype(F32))
    out = jnp.dot(mixed.astype(BF16), w_ref[...], preferred_element_type=F32)
    z = DEEPNORM_ALPHA * x_ref[...] + out
    mu = jnp.mean(z, axis=1, keepdims=True)
    zc = z - mu
    var = jnp.mean(zc * zc, axis=1, keepdims=True)
    y_ref[...] = zc * lax.rsqrt(var + LN_EPS) * g_ref[...] + b_ref[...]


def _merge(x2d, attn, ssm, mem, gate, w_bf, g_row, b_row, tm, ssm_map):
    m = x2d.shape[0]
    row = lambda i: (i, 0)
    const = lambda i: (0, 0)
    return pl.pallas_call(
        _merge_kernel,
        grid=(m // tm,),
        in_specs=[pl.BlockSpec((tm, D_MODEL), row),
                  pl.BlockSpec((tm, D_ATTN), row),
                  pl.BlockSpec((tm, D_SSM), ssm_map),
                  pl.BlockSpec((tm, D_MEM), row),
                  pl.BlockSpec((tm, D_MIX), row),
                  pl.BlockSpec((D_MIX, D_MODEL), const),
                  pl.BlockSpec((1, D_MODEL), const),
                  pl.BlockSpec((1, D_MODEL), const)],
        out_specs=pl.BlockSpec((tm, D_MODEL), row),
        out_shape=jax.ShapeDtypeStruct((m, D_MODEL), F32),
        compiler_params=_vmem_limit(48 * 2**20),
        name="merge",
    )(x2d, attn, ssm, mem, gate, w_bf, g_row, b_row)


_PAGE_RING = 4
_SAMPLE_ROW_GROUPS = 4


def _moba_sample_parts(b, n_seq, pt_ref, q_ref, kn_ref, vn_ref, ck_ref, cv_ref, o_ref,
                       buf_ref, sem_ref, s_ref, p_ref, blk_ref, qbd_ref, pown_ref, l_ref, acc_ref,
                       *, pages_per_chunk, chunks, page_size):
    ppc = pages_per_chunk
    items = 2 * chunks
    t = q_ref.shape[1]
    rows = N_HEADS_A * t
    n_pages = ppc * chunks
    pages_per_block = MOBA_BLOCK // page_size
    n_blocks = n_pages // pages_per_block

    def pad_rows(x):
        return jnp.concatenate([x, jnp.zeros((page_size - x.shape[0], x.shape[1]), x.dtype)], axis=0)

    ring = buf_ref.shape[0]
    ahead = ring - 1
    assert items % ring == 0 and ahead <= items

    def chunk_copies(seq, r):
        src_ref, chunk = (ck_ref, r) if r < chunks else (cv_ref, r - chunks)
        slot = r % ring
        return [pltpu.make_async_copy(src_ref.at[pt_ref[seq, chunk * ppc + i]], buf_ref.at[slot, i],
                                      sem_ref.at[slot]) for i in range(ppc)]

    def prologue():
        @pl.when(b == 0)
        def _first_chunks():
            for r in range(ahead):
                for cp in chunk_copies(b, r):
                    cp.start()
        qbd_ref[...] = _head_rows(q_ref[0] * (HEAD_DIM ** -0.5), N_HEADS_A, HEAD_DIM).astype(BF16)

    def page_pair(slot, i):
        return jnp.concatenate([buf_ref[slot, i].astype(BF16), buf_ref[slot, i + 1].astype(BF16)], axis=1)

    def key_chunk(r, slot):
        qbd_bf = qbd_ref[...]
        for i in range(0, ppc, 2):
            s2 = jnp.dot(qbd_bf, page_pair(slot, i), preferred_element_type=F32)
            s_ref[r * ppc + i] = s2[:, :page_size]
            s_ref[r * ppc + i + 1] = s2[:, page_size:]

    def select_and_softmax():
        lo_all = lax.dot_general(qbd_ref[...], pad_rows(kn_ref[0]).astype(BF16), _NT, preferred_element_type=F32)
        gr = rows // _SAMPLE_ROW_GROUPS
        lane = lax.broadcasted_iota(jnp.int32, (gr, page_size), 1)
        t_row = lax.broadcasted_iota(jnp.int32, (gr, page_size), 0) % t
        for g in range(_SAMPLE_ROW_GROUPS):
            rs = pl.ds(g * gr, gr)
            sc = jnp.zeros((gr, page_size), F32)
            for j in range(n_blocks):
                bs = s_ref[j * pages_per_block, rs, :]
                bm = bs
                for pg in range(1, pages_per_block):
                    bs = bs + s_ref[j * pages_per_block + pg, rs, :]
                    bm = jnp.maximum(bm, s_ref[j * pages_per_block + pg, rs, :])
                sc = jnp.where(lane == j, jnp.sum(bs, axis=1, keepdims=True), sc)
                blk_ref[j, rs, :] = bm
            sel = _select_topk(sc, lane < n_blocks, lane.astype(F32), axis=1)
            lo = jnp.where(lane <= t_row, lo_all[g * gr:(g + 1) * gr], NEG_INF)
            mx = lo
            for j in range(n_blocks):
                mask = jnp.broadcast_to(jnp.where(sel[:, j:j + 1], 0.0, NEG_INF), (gr, page_size))
                mx = jnp.maximum(mx, blk_ref[j, rs, :] + mask)
                blk_ref[j, rs, :] = mask
            m = jnp.max(mx, axis=1, keepdims=True)
            po = jnp.exp(lo - m)
            lsum = po
            for j in range(n_blocks):
                shift = blk_ref[j, rs, :] - m
                for pg in range(j * pages_per_block, (j + 1) * pages_per_block):
                    p = jnp.exp(s_ref[pg, rs, :] + shift)
                    lsum = lsum + p
                    p_ref[pg, rs, :] = p.astype(BF16)
            pown_ref[rs, :] = po.astype(BF16)
            l_ref[rs, :] = jnp.broadcast_to(jnp.sum(lsum, axis=1, keepdims=True), (gr, page_size))
        acc_ref[...] = jnp.zeros_like(acc_ref)

    def value_chunk(r, slot):
        acc = acc_ref[...]
        for i in range(0, ppc, 2):
            pg = (r - chunks) * ppc + i
            p2 = jnp.concatenate([p_ref[pg], p_ref[pg + 1]], axis=1)
            acc = acc + lax.dot_general(p2, page_pair(slot, i), _NT, preferred_element_type=F32)
        acc_ref[...] = acc

    def item(r):
        slot = r % ring
        if r + ahead < items:
            for cp in chunk_copies(b, r + ahead):
                cp.start()
        else:
            @pl.when(b + 1 < n_seq)
            def _next_sequence():
                for cp in chunk_copies(b + 1, r + ahead - items):
                    cp.start()
        for cp in chunk_copies(b, r):
            cp.wait()
        if r < chunks:
            key_chunk(r, slot)
        else:
            if r == chunks:
                select_and_softmax()
            value_chunk(r, slot)

    def finish():
        acc = acc_ref[...] + jnp.dot(pown_ref[...], pad_rows(vn_ref[0]).astype(BF16), preferred_element_type=F32)
        l = jnp.concatenate([l_ref[...]] * (D_ATTN // page_size), axis=1)
        o_ref[0] = _head_diagonal(acc / l, t, N_HEADS_A, HEAD_DIM)

    return prologue, item, finish


def _moba_sample_scratch(t, n_pages, pages_per_chunk, page_size):
    rows = N_HEADS_A * t
    return [pltpu.VMEM((_PAGE_RING, pages_per_chunk, D_ATTN, page_size), F32),
            pltpu.SemaphoreType.DMA((_PAGE_RING,)),
            pltpu.VMEM((n_pages, rows, page_size), F32),
            pltpu.VMEM((n_pages, rows, page_size), BF16),
            pltpu.VMEM((n_pages * page_size // MOBA_BLOCK, rows, page_size), F32),
            pltpu.VMEM((rows, D_ATTN), BF16),
            pltpu.VMEM((rows, page_size), BF16),
            pltpu.VMEM((rows, page_size), F32),
            pltpu.VMEM((rows, D_ATTN), F32)]


_N_PROMPT_SCRATCH = 8


def _moba_seq_schedule(ti, tiles_per_head):
    singles = max(2, tiles_per_head // 4)
    two_from = tiles_per_head - (tiles_per_head - singles) // 2
    one_from = two_from - singles
    late = jnp.maximum(ti - two_from, 0)
    first = jnp.where(ti < two_from, jnp.maximum(ti - one_from, 0), singles + 2 * late)
    second = singles + 1 + 2 * late
    return first, second, ti >= one_from, ti >= two_from


def _moba_kernel(pt_ref, q_ref, k_ref, v_ref, kmean_ref, qsa_ref, kna_ref, vna_ref, qsb_ref, knb_ref, vnb_ref,
                 ck_ref, cv_ref, o_ref, os_ref, *scratch,
                 tiles_per_head, n_seq, pages_per_chunk, chunks, page_size):
    n = pl.program_id(0)
    ti = lax.rem(n, tiles_per_head)
    base = n - ti
    first, second, has_first, has_second = _moba_seq_schedule(ti, tiles_per_head)
    p_setup, p_pairs, p_finish = _moba_prompt_parts(
        ti, q_ref, k_ref, v_ref, kmean_ref, o_ref, *scratch[:_N_PROMPT_SCRATCH])
    sample = functools.partial(_moba_sample_parts, pages_per_chunk=pages_per_chunk, chunks=chunks,
                               page_size=page_size)
    sample_scratch = scratch[_N_PROMPT_SCRATCH:]
    osa_ref = os_ref.at[0, pl.ds(first & 1, 1)]
    osb_ref = os_ref.at[0, pl.ds(1, 1)]
    seqs = ((has_first, sample(base + first, n_seq, pt_ref, qsa_ref, kna_ref, vna_ref, ck_ref, cv_ref, osa_ref,
                               *sample_scratch)),
            (has_second, sample(base + second, n_seq, pt_ref, qsb_ref, knb_ref, vnb_ref, ck_ref, cv_ref, osb_ref,
                                *sample_scratch)))

    p_setup()
    pairs = ti * (q_ref.shape[1] // MOBA_BLOCK) // 2
    items = 2 * chunks
    every = 2
    per_seq = items // every
    shift = jnp.where(has_second, int(math.log2(2 * per_seq)), int(math.log2(per_seq)))

    def segment_start(g):
        return jnp.minimum(lax.shift_right_logical(g * pairs, shift), pairs)

    for i, (active, (s_prologue, s_item, s_finish)) in enumerate(seqs):
        pl.when(active)(s_prologue)
        for r in range(items):
            pl.when(active)(functools.partial(s_item, r))
            if r % every == every - 1:
                g = i * per_seq + r // every
                p_pairs(segment_start(g), segment_start(g + 1))
        pl.when(active)(s_finish)
    p_finish()


def _moba(q, kb, vb, kmean, q_s, k_new, v_new, cache_k, cache_v, page_table, pages_per_chunk):
    b, s, _ = q.shape
    nb = s // MOBA_BLOCK
    tq = _MOBA_Q_BLOCKS * MOBA_BLOCK
    assert nb <= LANES // 2 and nb % SUBLANES == 0 and s % tq == 0 and _MOBA_Q_BLOCKS % 2 == 0
    head_pairs = D_ATTN // LANES
    tiles_per_head = s // tq
    n_tiles = b * head_pairs * tiles_per_head

    db, t, _ = q_s.shape
    n_pages = page_table.shape[1]
    page_size = cache_k.shape[2]
    chunks = n_pages // pages_per_chunk
    assert (n_pages * page_size) % MOBA_BLOCK == 0 and MOBA_BLOCK % page_size == 0
    assert n_pages * page_size // MOBA_BLOCK >= MOBA_TOPK and t <= page_size and page_size == LANES
    assert n_pages % pages_per_chunk == 0 and pages_per_chunk % 2 == 0
    assert db == n_tiles and tiles_per_head % 4 == 0

    def tile_map(n, pt):
        return (n // (head_pairs * tiles_per_head), n % tiles_per_head, (n // tiles_per_head) % head_pairs)

    def head_map(n, pt):
        return (n // (head_pairs * tiles_per_head), 0, (n // tiles_per_head) % head_pairs)

    def first_map(n, pt):
        ti = n % tiles_per_head
        return (n - ti + _moba_seq_schedule(ti, tiles_per_head)[0], 0, 0)

    def second_map(n, pt):
        ti = n % tiles_per_head
        return (n - ti + _moba_seq_schedule(ti, tiles_per_head)[1], 0, 0)

    def pair_map(n, pt):
        return (first_map(n, pt)[0] // 2, 0, 0, 0)

    seq_block = (1, t, D_ATTN)
    grid_spec = pltpu.PrefetchScalarGridSpec(
        num_scalar_prefetch=1,
        grid=(n_tiles,),
        in_specs=[pl.BlockSpec((1, tq, LANES), tile_map),
                  pl.BlockSpec((1, s, LANES), head_map),
                  pl.BlockSpec((1, s, LANES), head_map),
                  pl.BlockSpec((1, nb, LANES), head_map)]
                 + [pl.BlockSpec(seq_block, first_map)] * 3
                 + [pl.BlockSpec(seq_block, second_map)] * 3
                 + [pl.BlockSpec(memory_space=pl.ANY)] * 2,
        out_specs=(pl.BlockSpec((1, tq, LANES), tile_map), pl.BlockSpec((1, 2, t, D_ATTN), pair_map)),
        scratch_shapes=_moba_prompt_scratch(s, tq) + _moba_sample_scratch(t, n_pages, pages_per_chunk, page_size))
    attn, attn_s = pl.pallas_call(
        functools.partial(_moba_kernel, tiles_per_head=tiles_per_head, n_seq=db,
                          pages_per_chunk=pages_per_chunk, chunks=chunks, page_size=page_size),
        grid_spec=grid_spec,
        out_shape=(jax.ShapeDtypeStruct((b, s, D_ATTN), BF16), jax.ShapeDtypeStruct((db // 2, 2, t, D_ATTN), F32)),
        compiler_params=pltpu.CompilerParams(dimension_semantics=("arbitrary",),
                                             vmem_limit_bytes=52 * 2**20),
        name="moba",
    )(page_table, q, kb, vb, kmean, q_s, k_new, v_new, q_s, k_new, v_new, cache_k, cache_v)
    return attn, attn_s.reshape(db, t, D_ATTN)


def kernel(x_prompt, x_sample, mem_prompt, cache_k, cache_v, state_ssm_re, state_ssm_im,
           cache_mem_k, cache_mem_v, page_table, w_in, w_mem_kv, a_re, a_im, log_dt,
           b_re, b_im, c_re, c_im, d_skip, w_glu, b_glu, w_out, ln_g, ln_b):
    batch, seq, _ = x_prompt.shape
    db, t_new, _ = x_sample.shape
    n_mem = mem_prompt.shape[1]
    n_phys, page_size = cache_k.shape[:2]
    n_seg = SSM_CHAINS // batch
    seg_len = seq // n_seg
    tm = min(512, seg_len)

    w_in_bf = w_in.astype(BF16)
    w_mem_bf = w_mem_kv.astype(BF16)
    w_out_bf = w_out.astype(BF16)
    wg_bf = w_glu.astype(BF16)
    bg_row = b_glu.astype(F32).reshape(1, 2 * D_SSM)
    d_row = d_skip.astype(F32).reshape(1, D_SSM)
    g_row = ln_g.astype(F32).reshape(1, D_MODEL)
    b_row = ln_b.astype(F32).reshape(1, D_MODEL)
    a_row, b_mat, c_mat = _ssm_operands(a_re, a_im, log_dt, b_re, b_im, c_re, c_im)
    b_bf = b_mat.astype(BF16)
    c_bf = c_mat.astype(BF16)

    x2d = x_prompt.reshape(batch * seq, D_MODEL)
    q, k_t, v_t, u_chain, qm, gate, kb, vb, kmean = _in_proj_prompt(x2d, w_in_bf, batch, seq, tm)
    xs2d = x_sample.reshape(db * t_new, D_MODEL)
    tms = min(512, db * t_new)
    q_s, k_s, v_s, u_s, qm_s, gate_s = _in_proj_sample(xs2d, w_in_bf, tms)
    attn, attn_s = _moba(q.reshape(batch, seq, D_ATTN), kb.reshape(batch, seq, D_ATTN),
                         vb.reshape(batch, seq, D_ATTN), kmean.reshape(batch, seq // MOBA_BLOCK, D_ATTN),
                         q_s.reshape(db, t_new, D_ATTN), k_s.reshape(db, t_new, D_ATTN),
                         v_s.reshape(db, t_new, D_ATTN),
                         cache_k.transpose(0, 2, 3, 1).reshape(n_phys, D_ATTN, page_size),
                         cache_v.transpose(0, 2, 3, 1).reshape(n_phys, D_ATTN, page_size), page_table,
                         pages_per_chunk=min(16, page_table.shape[1] // 2))

    y_chain, hfin = _ssm_prompt(u_chain.reshape(seg_len * SSM_CHAINS, D_SSM), a_row, b_bf, c_bf, d_row,
                                wg_bf, bg_row, n_seg, steps_per_chunk=min(256, seg_len))
    mk, mv = _mem_kv(mem_prompt.reshape(batch * n_mem, D_MODEL), w_mem_bf)
    tmm = min(1024, seg_len)
    mem_o = _mem_attn_prompt(qm, mk, mv, batch, seq, tmm)
    tiles_per_seq = seq // tmm
    tiles_per_seg = seg_len // tmm

    def chain_map(i):
        b = i // tiles_per_seq
        ti = i % tiles_per_seq
        return (ti % tiles_per_seg, b * n_seg + ti // tiles_per_seg)

    y_prompt = _merge(x2d, attn.reshape(batch * seq, D_ATTN), y_chain.reshape(seg_len, SSM_CHAINS * D_SSM),
                      mem_o, gate, w_out_bf, g_row, b_row, tmm, chain_map)
    h_last = hfin[n_seg - 1::n_seg]
    ssm_re_p = h_last[:, :SSM_LANES].reshape(batch, SSM_GROUPS, SSM_STATE)
    ssm_im_p = h_last[:, SSM_LANES:].reshape(batch, SSM_GROUPS, SSM_STATE)

    h0 =jnp.concatenate([state_ssm_re.astype(F32).reshape(db, SSM_LANES),
                          state_ssm_im.astype(F32).reshape(db, SSM_LANES)], axis=1)
    y_tb, hfin_s = _ssm_sample(u_s.reshape(db, t_new, D_SSM).transpose(1, 0, 2), h0, a_row, b_mat, c_bf,
                               d_row, wg_bf, bg_row)
    ssm_s = y_tb.transpose(1, 0, 2).reshape(db * t_new, D_SSM)
    mem_s = _mem_attn_sample(qm_s.reshape(db, t_new, D_MEM),
                             cache_mem_k.transpose(0, 2, 3, 1).reshape(db, D_MEM, n_mem),
                             cache_mem_v.transpose(0, 2, 3, 1).reshape(db, D_MEM, n_mem), group=min(8, db))
    y_sample = _merge(xs2d, attn_s.reshape(db * t_new, D_ATTN), ssm_s, mem_s.reshape(db * t_new, D_MEM),
                      gate_s, w_out_bf, g_row, b_row, tms, lambda i: (i, 0))

    heads = (N_HEADS_A, HEAD_DIM)
    mem_heads = (MEM_HEADS, D_MEM // MEM_HEADS)
    return (y_prompt.reshape(batch, seq, D_MODEL), y_sample.reshape(db, t_new, D_MODEL),
            k_t.reshape(batch, *heads, seq).transpose(0, 3, 1, 2),
            v_t.reshape(batch, *heads, seq).transpose(0, 3, 1, 2), ssm_re_p, ssm_im_p,
            mk.reshape(batch, n_mem, *mem_heads), mv.reshape(batch, n_mem, *mem_heads),
            k_s.reshape(db, t_new, *heads), v_s.reshape(db, t_new, *heads),
            hfin_s[:, :SSM_LANES].reshape(db, SSM_GROUPS, SSM_STATE),
            hfin_s[:, SSM_LANES:].reshape(db, SSM_GROUPS, SSM_STATE))
```
